```python
import math
import numpy as np
import jax
import jax.numpy as jnp
from jax import lax

D_MODEL = 1024
BATCH = 8
SEQ = 2048
DEPTH = 2
DEC_BATCH = 128
DEC_SEQ = 8
PAST_LEN = 2048
PAGE_SIZE = 128

HEAD_DIM = 64
MIX_WIDTH = D_MODEL
HEADS_A = MIX_WIDTH // 2 // (2 * HEAD_DIM)
HEADS_B = MIX_WIDTH // 4 // HEAD_DIM
HEADS_C = MIX_WIDTH // 4 // HEAD_DIM
D_FF = 2816
ROPE_THETA = 10000.0
MOBA_BLOCK = 256
MOBA_TOPK = 3
CMP_LEN = 32
CMP_STRIDE = 16
CMP_HIDDEN = 4 * HEAD_DIM
SEL_BLOCK = 64
SEL_TOPK = 16
WINDOW = 512
Q_BLOCK = 128
DENSE_ROWS = 1024
MOBA_ROWS = 256
NSA_ROWS = 512
RMS_EPS = 1e-6

A_W = HEADS_A * 2 * HEAD_DIM
B_W = HEADS_B * HEAD_DIM
C_W = HEADS_C * HEAD_DIM
PROJ_SIZES = (A_W, A_W, A_W, B_W, B_W, B_W, C_W, 6 * HEAD_DIM, 3 * HEADS_C)
PROJ_WIDTH = 3 * A_W + 3 * B_W + C_W + 6 * HEAD_DIM + 3 * HEADS_C

N_PAGES = PAST_LEN // PAGE_SIZE
N_PHYS_PAGES = (5 * DEC_BATCH * N_PAGES + 3) // 4
WIN_BUF = min(WINDOW, PAST_LEN)

kernel_name = "hybrid_diff_moba_nsa_macaron_step"


def _rms_norm(x, g):
    x32 = x.astype(jnp.float32)
    y = x32 * lax.rsqrt(jnp.mean(x32 * x32, axis=-1, keepdims=True) + RMS_EPS)
    return (y * g.astype(jnp.float32)).astype(x.dtype)


def _swiglu(x, wg, wu, wd):
    return (jax.nn.silu(x @ wg) * (x @ wu)) @ wd


def _rope_tables(pos, dtype):
    inv = ROPE_THETA ** (-jnp.arange(0, HEAD_DIM, 2, dtype=jnp.float32) / HEAD_DIM)
    ang = pos.astype(jnp.float32)[:, None] * inv[None, :]
    ang = jnp.concatenate([ang, ang], axis=-1)
    return jnp.cos(ang).astype(dtype), jnp.sin(ang).astype(dtype)


def _rope(x, cos, sin):
    shape = (1, x.shape[1]) + (1,) * (x.ndim - 3) + (HEAD_DIM,)
    c, s = cos.reshape(shape), sin.reshape(shape)
    x1, x2 = jnp.split(x, 2, axis=-1)
    return x * c + jnp.concatenate([-x2, x1], axis=-1) * s


def _masked_softmax(s, mask):
    s = jnp.where(mask, s, -jnp.inf)
    m = jnp.max(s, axis=-1, keepdims=True)
    m = jnp.where(jnp.isfinite(m), m, 0.0)
    e = jnp.exp(s - m)
    d = jnp.sum(e, axis=-1, keepdims=True)
    return e / jnp.where(d > 0, d, 1.0)


def _block_len(n_q, n_rows, row_budget):
    cap = max(1, min(Q_BLOCK, row_budget // n_rows))
    return max(d for d in range(1, cap + 1) if n_q % d == 0)


def _map_query_blocks(fn, arrays, pos, qb):
    n_blk = pos.shape[0] // qb
    split = lambda a: jnp.moveaxis(a.reshape((a.shape[0], n_blk, qb) + a.shape[2:]), 1, 0)
    xs = tuple(split(a) for a in arrays) + (pos.reshape(n_blk, qb),)
    out = lax.map(lambda args: fn(*args), xs)
    out = jnp.moveaxis(out, 0, 1)
    return out.reshape((out.shape[0], n_blk * qb) + out.shape[3:])


def _gather_pages(pool, page_table):
    g = pool[page_table]
    return g.reshape((g.shape[0], g.shape[1] * g.shape[2]) + g.shape[3:])


def _diff_attention(q, k, v, pos, lam):
    n, t_len = k.shape[0], k.shape[1]
    kpos = jnp.arange(t_len, dtype=jnp.int32)
    scale = HEAD_DIM ** -0.5

    def block(qc, pc):
        s = jnp.einsum('nqhcd,nkhcd->nchqk', qc, k).astype(jnp.float32) * scale
        p = _masked_softmax(s, kpos[None, :] <= pc[:, None])
        w = p[:, 0] - lam * p[:, 1]
        return jnp.einsum('nhqk,nkhe->nqhe', w.astype(v.dtype), v)

    return _map_query_blocks(block, (q,), pos, _block_len(q.shape[1], n, DENSE_ROWS))


def _moba_attention(q, k, v, pos):
    n, t_len, h, d = k.shape
    nb = -(-t_len // MOBA_BLOCK)
    padw = ((0, 0), (0, nb * MOBA_BLOCK - t_len), (0, 0), (0, 0))
    kb = jnp.pad(k, padw).reshape(n, nb, MOBA_BLOCK, h, d).transpose(0, 3, 1, 2, 4)
    vb = jnp.pad(v, padw).reshape(n, nb, MOBA_BLOCK, h, d).transpose(0, 3, 1, 2, 4)
    kmean = jnp.mean(kb.astype(jnp.float32), axis=3)
    n_top = min(MOBA_TOPK, nb)
    blk = jnp.arange(nb, dtype=jnp.int32)
    offs = jnp.arange(MOBA_BLOCK, dtype=jnp.int32)
    ni = jnp.arange(n)[:, None, None, None]
    hi = jnp.arange(h)[None, :, None, None]
    scale = HEAD_DIM ** -0.5

    def block(qc, pc):
        own = pc // MOBA_BLOCK
        own4 = own[None, None, :, None]
        g = jnp.einsum('nqhd,nhbd->nhqb', qc.astype(jnp.float32), kmean)
        g = jnp.where(blk[None, None, None, :] < own4, g, -jnp.inf)
        _, top = lax.top_k(g, n_top)
        own_b = jnp.broadcast_to(own4, (n, h, own.shape[0], 1))
        sel = jnp.concatenate([top, own_b], axis=-1)
        blk_ok = jnp.concatenate([top < own4, jnp.ones(own_b.shape, dtype=bool)], axis=-1)
        ks = kb[ni, hi, sel]
        vs = vb[ni, hi, sel]
        kpos = sel[..., None] * MOBA_BLOCK + offs
        mask = blk_ok[..., None] & (kpos <= pc[None, None, :, None, None])
        s = jnp.einsum('nqhd,nhqsmd->nhqsm', qc, ks).astype(jnp.float32) * scale
        shp = s.shape
        p = _masked_softmax(s.reshape(shp[:3] + (-1,)), mask.reshape(shp[:3] + (-1,))).reshape(shp)
        return jnp.einsum('nhqsm,nhqsmd->nqhd', p.astype(v.dtype), vs)

    return _map_query_blocks(block, (q,), pos, _block_len(q.shape[1], n, MOBA_ROWS))


def _compress(xb, pos_emb, w1, w2):
    n, c = xb.shape[0], xb.shape[1]
    hcat = (xb + pos_emb).reshape(n, c, CMP_LEN * HEAD_DIM)
    return jax.nn.gelu(hcat @ w1) @ w2


def _nsa_attention(q, gates, kc, vc, ks, vs, kw, vw, kw_start, pos, w1, w2, pos_emb):
    n, t_len, d = kc.shape
    dt = q.dtype
    n_cmp = (t_len - CMP_LEN) // CMP_STRIDE + 1
    starts = np.arange(n_cmp) * CMP_STRIDE
    gidx = starts[:, None] + np.arange(CMP_LEN)[None, :]
    kcmp = _compress(kc[:, gidx], pos_emb[0], w1[0], w2[0])
    vcmp = _compress(vc[:, gidx], pos_emb[1], w1[1], w2[1])
    cmp_end = jnp.asarray(starts + CMP_LEN - 1, dtype=jnp.int32)
    nsb = -(-t_len // SEL_BLOCK)
    sb_start = np.arange(nsb) * SEL_BLOCK
    ov = np.clip(np.minimum(starts[:, None] + CMP_LEN, sb_start[None, :] + SEL_BLOCK)
                 - np.maximum(starts[:, None], sb_start[None, :]), 0, None) / CMP_STRIDE
    overlap = jnp.asarray(ov, dtype=jnp.float32)
    pad = ((0, 0), (0, nsb * SEL_BLOCK - t_len), (0, 0))
    ks_b = jnp.pad(ks, pad).reshape(n, nsb, SEL_BLOCK, d)
    vs_b = jnp.pad(vs, pad).reshape(n, nsb, SEL_BLOCK, d)
    n_sel = min(SEL_TOPK, nsb)
    sblk = jnp.arange(nsb, dtype=jnp.int32)
    soffs = jnp.arange(SEL_BLOCK, dtype=jnp.int32)
    ni = jnp.arange(n)[:, None, None]
    kw_p = jnp.pad(kw, ((0, 0), (WINDOW, 0), (0, 0)))
    vw_p = jnp.pad(vw, ((0, 0), (WINDOW, 0), (0, 0)))
    scale = HEAD_DIM ** -0.5

    def block(qc, gc, pc):
        qb = pc.shape[0]
        s_c = jnp.einsum('nqhd,ncd->nhqc', qc, kcmp).astype(jnp.float32) * scale
        p_c = _masked_softmax(s_c, cmp_end[None, :] <= pc[:, None])
        o_cmp = jnp.einsum('nhqc,ncd->nqhd', p_c.astype(dt), vcmp)
        imp = jnp.einsum('nhqc,cb->nqb', p_c, overlap)
        own = (pc // SEL_BLOCK)[None, :, None]
        b = sblk[None, None, :]
        forced = (b == 0) | (b == own) | (b == own - 1)
        imp = jnp.where(b > own, -jnp.inf, jnp.where(forced, jnp.inf, imp))
        _, sel = lax.top_k(imp, n_sel)
        ksel = ks_b[ni, sel]
        vsel = vs_b[ni, sel]
        kpos = sel[..., None] * SEL_BLOCK + soffs
        m_s = (sel <= own)[..., None] & (kpos <= pc[None, :, None, None])
        s_s = jnp.einsum('nqhd,nqsmd->nhqsm', qc, ksel).astype(jnp.float32) * scale
        shp = s_s.shape
        p_s = _masked_softmax(s_s.reshape(shp[:3] + (-1,)), m_s.reshape(n, 1, qb, -1)).reshape(shp)
        o_sel = jnp.einsum('nhqsm,nqsmd->nqhd', p_s.astype(dt), vsel)
        c0 = pc[0]
        kwin = lax.dynamic_slice_in_dim(kw_p, c0 - kw_start, WINDOW + qb, axis=1)
        vwin = lax.dynamic_slice_in_dim(vw_p, c0 - kw_start, WINDOW + qb, axis=1)
        wpos = c0 - WINDOW + jnp.arange(WINDOW + qb, dtype=jnp.int32)
        m_w = ((wpos[None, :] >= kw_start) & (wpos[None, :] <= pc[:, None])
               & (wpos[None, :] >= pc[:, None] - WINDOW))
        s_w = jnp.einsum('nqhd,nkd->nhqk', qc, kwin).astype(jnp.float32) * scale
        p_w = _masked_softmax(s_w, m_w)
        o_win = jnp.einsum('nhqk,nkd->nqhd', p_w.astype(dt), vwin)
        return gc[..., 0:1] * o_cmp + gc[..., 1:2] * o_sel + gc[..., 2:3] * o_win

    return _map_query_blocks(block, (q, gates), pos, _block_len(q.shape[1], n, NSA_ROWS))


def _layer(x, q0, lidx, past, w_in, w_out, g_mix, g_ffn, wg, wu, wd, lam_p, g_head, w_cmp1, w_cmp2, cmp_pos):
    n, sq, _ = x.shape
    pos = q0 + jnp.arange(sq, dtype=jnp.int32)
    cos, sin = _rope_tables(pos, x.dtype)
    h = x + 0.5 * _swiglu(_rms_norm(x, g_ffn[0]), wg[0], wu[0], wd[0])
    u = _rms_norm(h, g_mix)
    offsets = np.cumsum(PROJ_SIZES)[:-1].tolist()
    qa, ka, va, qb_, kb_, vb_, qc, kvc, gc = jnp.split(u @ w_in, offsets, axis=-1)
    qa = _rope(qa.reshape(n, sq, HEADS_A, 2, HEAD_DIM), cos, sin)
    ka = _rope(ka.reshape(n, sq, HEADS_A, 2, HEAD_DIM), cos, sin)
    va = va.reshape(n, sq, HEADS_A, 2 * HEAD_DIM)
    qb_ = _rope(qb_.reshape(n, sq, HEADS_B, HEAD_DIM), cos, sin)
    kb_ = _rope(kb_.reshape(n, sq, HEADS_B, HEAD_DIM), cos, sin)
    vb_ = vb_.reshape(n, sq, HEADS_B, HEAD_DIM)
    qc = _rope(qc.reshape(n, sq, HEADS_C, HEAD_DIM), cos, sin)
    kvc = kvc.reshape(n, sq, 3, 2, HEAD_DIM)
    kc3 = _rope(kvc[:, :, :, 0], cos, sin)
    vc3 = kvc[:, :, :, 1]
    gates = jax.nn.sigmoid(gc.reshape(n, sq, HEADS_C, 3))

    new_b_kv = jnp.stack([kb_, vb_], axis=2)
    new_c_kv = jnp.stack([kc3[:, :, 0], vc3[:, :, 0], kc3[:, :, 1], vc3[:, :, 1]], axis=2)
    win_new = jnp.stack([kc3[:, :, 2], vc3[:, :, 2]], axis=2)
    if past is None:
        ka_all, va_all, bkv_all, ckv_all, win_all, win_start = ka, va, new_b_kv, new_c_kv, win_new, 0
    else:
        p_ak, p_av, p_bkv, p_ckv, p_win = past
        ka_all = jnp.concatenate([p_ak, ka], axis=1)
        va_all = jnp.concatenate([p_av, va], axis=1)
        bkv_all = jnp.concatenate([p_bkv, new_b_kv], axis=1)
        ckv_all = jnp.concatenate([p_ckv, new_c_kv], axis=1)
        win_all = jnp.concatenate([p_win, win_new], axis=1)
        win_start = q0 - p_win.shape[1]

    lam_init = 0.8 - 0.6 * math.exp(-0.3 * lidx)
    lp = lam_p.astype(jnp.float32)
    lam = jnp.exp(jnp.sum(lp[0] * lp[1])) - jnp.exp(jnp.sum(lp[2] * lp[3])) + lam_init
    o_a = _diff_attention(qa, ka_all, va_all, pos, lam)
    o_a = _rms_norm(o_a, g_head) * (1.0 - lam_init)
    o_b = _moba_attention(qb_, bkv_all[:, :, 0], bkv_all[:, :, 1], pos)
    o_c = _nsa_attention(qc, gates, ckv_all[:, :, 0], ckv_all[:, :, 1], ckv_all[:, :, 2], ckv_all[:, :, 3],
                         win_all[:, :, 0], win_all[:, :, 1], win_start, pos, w_cmp1, w_cmp2, cmp_pos)
    mix = jnp.concatenate([o_a.reshape(n, sq, A_W), o_b.reshape(n, sq, B_W), o_c.reshape(n, sq, C_W)], axis=-1)
    h = h + mix @ w_out
    y = h + 0.5 * _swiglu(_rms_norm(h, g_ffn[1]), wg[1], wu[1], wd[1])
    win_keep = min(WINDOW, win_all.shape[1])
    return y, (ka, va, new_b_kv, new_c_kv, win_all[:, win_all.shape[1] - win_keep:])


def setup_inputs(seed: int = 0) -> dict:
    key = jax.random.key(seed)
    ks = jax.random.split(key, 24)
    f32 = jnp.float32
    nrm = lambda k, shape, scale: scale * jax.random.normal(k, shape, f32)
    page_table = jax.random.permutation(ks[7], N_PHYS_PAGES)[:DEC_BATCH * N_PAGES]
    return {
        "x_prompt": nrm(ks[0], (BATCH, SEQ, D_MODEL), 1.0),
        "x_sample": nrm(ks[1], (DEC_BATCH, DEC_SEQ, D_MODEL), 1.0),
        "cache_a_k": nrm(ks[2], (DEPTH, N_PHYS_PAGES, PAGE_SIZE, HEADS_A, 2, HEAD_DIM), 1.0),
        "cache_a_v": nrm(ks[3], (DEPTH, N_PHYS_PAGES, PAGE_SIZE, HEADS_A, 2 * HEAD_DIM), 1.0),
        "cache_b_kv": nrm(ks[4], (DEPTH, N_PHYS_PAGES, PAGE_SIZE, 2, HEADS_B, HEAD_DIM), 1.0),
        "cache_c_kv": nrm(ks[5], (DEPTH, N_PHYS_PAGES, PAGE_SIZE, 4, HEAD_DIM), 1.0),
        "state_c_win": nrm(ks[6], (DEPTH, DEC_BATCH, WIN_BUF, 2, HEAD_DIM), 1.0),
        "page_table": page_table.reshape(DEC_BATCH, N_PAGES).astype(jnp.int32),
        "w_in": nrm(ks[8], (DEPTH, D_MODEL, PROJ_WIDTH), D_MODEL ** -0.5),
        "w_out": nrm(ks[9], (DEPTH, MIX_WIDTH, D_MODEL), MIX_WIDTH ** -0.5),
        "g_mix": 1.0 + nrm(ks[10], (DEPTH, D_MODEL), 0.05),
        "g_ffn": 1.0 + nrm(ks[11], (DEPTH, 2, D_MODEL), 0.05),
        "w_ffn_gate": nrm(ks[12], (DEPTH, 2, D_MODEL, D_FF), D_MODEL ** -0.5),
        "w_ffn_up": nrm(ks[13], (DEPTH, 2, D_MODEL, D_FF), D_MODEL ** -0.5),
        "w_ffn_down": nrm(ks[14], (DEPTH, 2, D_FF, D_MODEL), D_FF ** -0.5),
        "diff_lambda": nrm(ks[15], (DEPTH, 4, HEAD_DIM), 0.1),
        "g_diff_head": 1.0 + nrm(ks[16], (DEPTH, 2 * HEAD_DIM), 0.05),
        "w_cmp1": nrm(ks[17], (DEPTH, 2, CMP_LEN * HEAD_DIM, CMP_HIDDEN), (CMP_LEN * HEAD_DIM) ** -0.5),
        "w_cmp2": nrm(ks[18], (DEPTH, 2, CMP_HIDDEN, HEAD_DIM), CMP_HIDDEN ** -0.5),
        "cmp_pos": nrm(ks[19], (DEPTH, 2, CMP_LEN, HEAD_DIM), 0.5),
        "g_final": 1.0 + nrm(ks[20], (D_MODEL,), 0.05),
    }


def reference(x_prompt, x_sample, cache_a_k, cache_a_v, cache_b_kv, cache_c_kv, state_c_win, page_table,
              w_in, w_out, g_mix, g_ffn, w_ffn_gate, w_ffn_up, w_ffn_down, diff_lambda, g_diff_head,
              w_cmp1, w_cmp2, cmp_pos, g_final):
    past_len = page_table.shape[1] * cache_a_k.shape[2]
    hp, hs = x_prompt, x_sample
    ent_p, ent_s = [], []
    for l in range(DEPTH):
        w = (w_in[l], w_out[l], g_mix[l], g_ffn[l], w_ffn_gate[l], w_ffn_up[l], w_ffn_down[l],
             diff_lambda[l], g_diff_head[l], w_cmp1[l], w_cmp2[l], cmp_pos[l])
        hp, ep = _layer(hp, 0, l, None, *w)
        past = (_gather_pages(cache_a_k[l], page_table), _gather_pages(cache_a_v[l], page_table),
                _gather_pages(cache_b_kv[l], page_table), _gather_pages(cache_c_kv[l], page_table),
                state_c_win[l])
        hs, es = _layer(hs, past_len, l, past, *w)
        ent_p.append(ep)
        ent_s.append(es)
    y_prompt = _rms_norm(hp, g_final)
    y_sample = _rms_norm(hs, g_final)
    st = lambda ents, i: jnp.stack([e[i] for e in ents], axis=0)
    return (y_prompt, y_sample,
            st(ent_p, 0), st(ent_s, 0), st(ent_p, 1), st(ent_s, 1),
            st(ent_p, 2), st(ent_s, 2), st(ent_p, 3), st(ent_s, 3),
            st(ent_p, 4), st(ent_s, 4))
```

```python
import functools
import math

import numpy as np
import jax
import jax.numpy as jnp
from jax import lax
from jax.experimental import pallas as pl
from jax.experimental.pallas import tpu as pltpu

F32 = jnp.float32
BF16 = jnp.bfloat16

D_MODEL = 1024
DEPTH = 2
HEAD_DIM = 64
HEADS_A = 4
HEADS_B = 4
HEADS_C = 4
D_FF = 2816
ROPE_THETA = 10000.0
MOBA_BLOCK = 256
MOBA_TOPK = 3
CMP_LEN = 32
CMP_STRIDE = 16
CMP_HIDDEN = 4 * HEAD_DIM
SEL_BLOCK = 64
SEL_TOPK = 16
WINDOW = 512
RMS_EPS = 1e-6
PAGE_SIZE = 128

A_W = HEADS_A * 2 * HEAD_DIM
B_W = HEADS_B * HEAD_DIM
C_W = HEADS_C * HEAD_DIM
KVC_W = 6 * HEAD_DIM
GATE_W = 3 * HEADS_C
MAIN_W = 3 * A_W + 3 * B_W + C_W + KVC_W
LANES = 128
QK_SCALE = HEAD_DIM ** -0.5
NEG_INF = float("-inf")
VMEM_LIMIT = 56 * 1024 * 1024

TM_FFN = 512
TF_FFN = 256
TM_PROJ = 512
TQ = 256


def _nn(a, b):
    return jnp.dot(a, b, preferred_element_type=F32)


def _nt(a, b):
    return lax.dot_general(a, b, (((1,), (1,)), ((), ())), preferred_element_type=F32)


def _split3(x):
    hi = x.astype(BF16)
    r1 = x - hi.astype(F32)
    mid = r1.astype(BF16)
    lo = (r1 - mid.astype(F32)).astype(BF16)
    return hi, mid, lo


def _nt_precise(a, b):
    a_hi, a_mid, _ = _split3(a)
    b_hi, b_mid, _ = _split3(b)
    return _nt(a_hi, b_hi) + (_nt(a_hi, b_mid) + _nt(a_mid, b_hi))


def _rms(x):
    return x * lax.rsqrt(jnp.mean(x * x, axis=-1, keepdims=True) + RMS_EPS)


def _iota(shape, dim):
    return lax.broadcasted_iota(jnp.int32, shape, dim)


def _params(n_axes):
    return pltpu.CompilerParams(dimension_semantics=("arbitrary",) * n_axes,
                                vmem_limit_bytes=VMEM_LIMIT)


def _lam_value(lam_ref, lam_init):
    lp = lam_ref[...]
    a = jnp.sum(lp[0:1] * lp[1:2], axis=1, keepdims=True)
    b = jnp.sum(lp[2:3] * lp[3:4], axis=1, keepdims=True)
    return jnp.exp(a) - jnp.exp(b) + lam_init


def _lam_init(lidx):
    return 0.8 - 0.6 * math.exp(-0.3 * lidx)


def _ffn_body(x_ref, g_ref, wg_ref, wu_ref, wd_ref, *rest, final):
    if final:
        gf_ref, o_ref = rest
    else:
        (o_ref,) = rest
    x = x_ref[...]
    xn = (_rms(x) * g_ref[...]).astype(BF16)
    acc = jnp.zeros_like(x)
    for f in range(D_FF // TF_FFN):
        sl = slice(f * TF_FFN, (f + 1) * TF_FFN)
        g = _nn(xn, wg_ref[:, sl])
        u = _nn(xn, wu_ref[:, sl])
        a = (g * jax.nn.sigmoid(g) * u).astype(BF16)
        acc = acc + _nn(a, wd_ref[sl, :])
    y = x + 0.5 * acc
    if final:
        y = _rms(y) * gf_ref[...]
    o_ref[...] = y


def _ffn(x, g, wg, wu, wd, g_final=None):
    m = x.shape[0]
    row = pl.BlockSpec((TM_FFN, D_MODEL), lambda i: (i, 0))
    vec = pl.BlockSpec((1, D_MODEL), lambda i: (0, 0))
    full = lambda shape: pl.BlockSpec(shape, lambda i: (0, 0))
    in_specs = [row, vec, full((D_MODEL, D_FF)), full((D_MODEL, D_FF)), full((D_FF, D_MODEL))]
    args = [x, g.reshape(1, D_MODEL), wg, wu, wd]
    if g_final is not None:
        in_specs.append(vec)
        args.append(g_final.reshape(1, D_MODEL))
    return pl.pallas_call(
        functools.partial(_ffn_body, final=g_final is not None),
        out_shape=jax.ShapeDtypeStruct((m, D_MODEL), F32),
        grid=(m // TM_FFN,),
        in_specs=in_specs,
        out_specs=row,
        compiler_params=_params(1),
        name="ffn_half",
    )(*args)


def _proj_body(h_ref, g_ref, w_ref, wgate_ref, cos_ref, sin_ref,
               ka_ref, va_ref, bkv_ref, ckv_ref, win_ref,
               qa16_ref, ka16_ref, va16_ref, qb16_ref, qb32_ref, bkv16_ref,
               qcz16_ref, ckv16_ref, win16_ref, gate_ref):
    tm = h_ref.shape[0]
    u = (_rms(h_ref[...]) * g_ref[...]).astype(BF16)
    cos = cos_ref[...]
    sin = sin_ref[...]
    lane = _iota((tm, LANES), 1)
    lo32 = (lane % HEAD_DIM) < (HEAD_DIM // 2)
    lo64 = lane < HEAD_DIM

    def rope(x):
        sh = jnp.where(lo32, pltpu.roll(x, LANES - HEAD_DIM // 2, 1), pltpu.roll(x, HEAD_DIM // 2, 1))
        return x * cos + sh * sin

    def cols(a, b):
        return _nn(u, w_ref[:, a:b])

    def slab(p, k):
        return p[:, k * LANES:(k + 1) * LANES]

    o = 0
    p = cols(o, o + A_W)
    for k in range(A_W // LANES):
        qa16_ref[:, k * LANES:(k + 1) * LANES] = (rope(slab(p, k)) * QK_SCALE).astype(BF16)
    o += A_W
    p = cols(o, o + A_W)
    for k in range(A_W // LANES):
        r = rope(slab(p, k))
        ka_ref[:, k * LANES:(k + 1) * LANES] = r
        ka16_ref[:, k * LANES:(k + 1) * LANES] = r.astype(BF16)
    o += A_W
    p = cols(o, o + A_W)
    va_ref[...] = p
    va16_ref[...] = p.astype(BF16)
    o += A_W
    p = cols(o, o + B_W)
    for k in range(B_W // LANES):
        r = rope(slab(p, k)) * QK_SCALE
        qb32_ref[:, k * LANES:(k + 1) * LANES] = r
        qb16_ref[:, k * LANES:(k + 1) * LANES] = r.astype(BF16)
    o += B_W
    p = cols(o, o + B_W)
    for k in range(B_W // LANES):
        r = rope(slab(p, k))
        bkv_ref[:, k * LANES:(k + 1) * LANES] = r
        bkv16_ref[:, k * LANES:(k + 1) * LANES] = r.astype(BF16)
    o += B_W
    p = cols(o, o + B_W)
    bkv_ref[:, B_W:2 * B_W] = p
    bkv16_ref[:, B_W:2 * B_W] = p.astype(BF16)
    o += B_W
    p = cols(o, o + C_W)
    for k in range(C_W // LANES):
        r = rope(slab(p, k)) * QK_SCALE
        even = jnp.where(lo64, r, 0.0)
        odd = jnp.where(lo64, pltpu.roll(r, HEAD_DIM, 1), 0.0)
        qcz16_ref[:, (2 * k) * LANES:(2 * k + 1) * LANES] = even.astype(BF16)
        qcz16_ref[:, (2 * k + 1) * LANES:(2 * k + 2) * LANES] = odd.astype(BF16)
    o += C_W
    p = cols(o, o + KVC_W)
    for k in range(KVC_W // LANES):
        x = slab(p, k)
        r = jnp.where(lo64, rope(x), x)
        if k < 2:
            ckv_ref[:, k * LANES:(k + 1) * LANES] = r
            ckv16_ref[:, k * LANES:(k + 1) * LANES] = r.astype(BF16)
        else:
            win_ref[...] = r
            win16_ref[...] = r.astype(BF16)
    gate_ref[...] = jax.nn.sigmoid(_nn(u, wgate_ref[...]))


def _proj(h, g_mix, w_main, w_gate, cos, sin, cos_index):
    m = h.shape[0]
    tm = TM_PROJ
    row = lambda w: pl.BlockSpec((tm, w), lambda i: (i, 0))
    full = lambda shape: pl.BlockSpec(shape, lambda i: (0, 0))
    tab = pl.BlockSpec((tm, LANES), cos_index)
    widths_f32 = (A_W, A_W, 2 * B_W, 4 * HEAD_DIM, 2 * HEAD_DIM)
    outs16 = ((A_W, BF16), (A_W, BF16), (A_W, BF16), (B_W, BF16), (B_W, F32), (2 * B_W, BF16),
              (2 * C_W, BF16), (4 * HEAD_DIM, BF16), (2 * HEAD_DIM, BF16), (LANES, F32))
    out_shape = [jax.ShapeDtypeStruct((m, w), F32) for w in widths_f32]
    out_shape += [jax.ShapeDtypeStruct((m, w), dt) for w, dt in outs16]
    out_specs = [row(w) for w in widths_f32] + [row(w) for w, _ in outs16]
    return pl.pallas_call(
        _proj_body,
        out_shape=out_shape,
        grid=(m // tm,),
        in_specs=[row(D_MODEL), full((1, D_MODEL)), full((D_MODEL, MAIN_W)), full((D_MODEL, LANES)), tab, tab],
        out_specs=out_specs,
        compiler_params=_params(1),
        name="in_proj_rope",
    )(h, g_mix.reshape(1, D_MODEL), w_main, w_gate, cos, sin)


def _outproj_body(h_ref, oa_ref, ob_ref, oc_ref, w_ref, o_ref):
    y = h_ref[...] + _nn(oa_ref[...], w_ref[0:A_W, :])
    y = y + _nn(ob_ref[...], w_ref[A_W:A_W + B_W, :])
    y = y + _nn(oc_ref[...], w_ref[A_W + B_W:, :])
    o_ref[...] = y


def _outproj(h, oa, ob, oc, w_out):
    m = h.shape[0]
    tm = TM_PROJ
    row = lambda w: pl.BlockSpec((tm, w), lambda i: (i, 0))
    return pl.pallas_call(
        _outproj_body,
        out_shape=jax.ShapeDtypeStruct((m, D_MODEL), F32),
        grid=(m // tm,),
        in_specs=[row(D_MODEL), row(A_W), row(B_W), row(C_W),
                  pl.BlockSpec((D_MODEL, D_MODEL), lambda i: (0, 0))],
        out_specs=row(D_MODEL),
        compiler_params=_params(1),
        name="out_proj",
    )(h, oa, ob, oc, w_out)


def _softmax_start(s):
    m = jnp.max(s, axis=-1, keepdims=True)
    p = jnp.exp(s - m)
    return m, jnp.sum(p, axis=-1, keepdims=True), p


def _softmax_step(s, m, l):
    m_new = jnp.maximum(m, jnp.max(s, axis=-1, keepdims=True))
    alpha = jnp.exp(m - m_new)
    p = jnp.exp(s - m_new)
    return m_new, alpha, alpha * l + jnp.sum(p, axis=-1, keepdims=True), p


def _rank_lower(x, n, width_iota):
    rank = jnp.zeros(x.shape, F32)
    for bp in range(n):
        col = x[:, bp:bp + 1]
        before = (col > x) | ((col == x) & (bp < width_iota))
        rank = rank + jnp.where(before, 1.0, 0.0)
    return rank


def _head_rms_scale(o, gh, lam_init):
    return _rms(o) * gh * (1.0 - lam_init)


def _attn_a_prompt_body(q_ref, k_ref, v_ref, lam_ref, gh_ref, o_ref, *, lam_init):
    i = pl.program_id(1)
    tq = q_ref.shape[0]
    lam = _lam_value(lam_ref, lam_init)
    gh = gh_ref[...]
    lane = _iota((tq, LANES), 1)
    causal = _iota((tq, tq), 1) <= _iota((tq, tq), 0)
    for h in range(HEADS_A):
        sl = slice(h * LANES, (h + 1) * LANES)
        q2 = q_ref[:, sl]
        outs = []
        for c in range(2):
            in_c = (lane >= c * HEAD_DIM) & (lane < (c + 1) * HEAD_DIM)
            qm = jnp.where(in_c, q2, jnp.zeros_like(q2))
            kd = k_ref[pl.ds(pl.multiple_of(i * tq, tq), tq), sl]
            vd = v_ref[pl.ds(pl.multiple_of(i * tq, tq), tq), sl]
            s = jnp.where(causal, _nt(qm, kd), NEG_INF)
            m, l, p = _softmax_start(s)
            acc = _nn(p.astype(BF16), vd)

            def body(j, carry, qm=qm, sl=sl):
                m, l, acc = carry
                kj = k_ref[pl.ds(pl.multiple_of(j * tq, tq), tq), sl]
                vj = v_ref[pl.ds(pl.multiple_of(j * tq, tq), tq), sl]
                m, alpha, l, p = _softmax_step(_nt(qm, kj), m, l)
                return m, l, alpha * acc + _nn(p.astype(BF16), vj)

            m, l, acc = lax.fori_loop(0, i, body, (m, l, acc))
            outs.append(acc / l)
        o = outs[0] - lam * outs[1]
        o_ref[:, sl] = _head_rms_scale(o, gh, lam_init).astype(BF16)


def _attn_a_prompt(q16, k16, v16, lam_p, g_head, n, s, lidx):
    nq = s // TQ
    qspec = pl.BlockSpec((TQ, A_W), lambda b, i: (b * nq + i, 0))
    kspec = pl.BlockSpec((s, A_W), lambda b, i: (b, 0))
    return pl.pallas_call(
        functools.partial(_attn_a_prompt_body, lam_init=_lam_init(lidx)),
        out_shape=jax.ShapeDtypeStruct((n * s, A_W), BF16),
        grid=(n, nq),
        in_specs=[qspec, kspec, kspec,
                  pl.BlockSpec((4, HEAD_DIM), lambda b, i: (0, 0)),
                  pl.BlockSpec((1, 2 * HEAD_DIM), lambda b, i: (0, 0))],
        out_specs=qspec,
        compiler_params=_params(2),
        name="diff_attn_prompt",
    )(q16, k16, v16, lam_p, g_head.reshape(1, 2 * HEAD_DIM))


def _attn_b_prompt_body(q_ref, q32_ref, k32_ref, kv_ref, o_ref, kmean_ref):
    i = pl.program_id(1)
    tq = q_ref.shape[0]
    nb = kmean_ref.shape[0]

    @pl.when(i == 0)
    def _():
        for b in range(nb):
            blk = k32_ref[b * MOBA_BLOCK:(b + 1) * MOBA_BLOCK, :]
            kmean_ref[b:b + 1, :] = jnp.sum(blk, axis=0, keepdims=True) * (1.0 / MOBA_BLOCK)

    lane = _iota((tq, LANES), 1)
    causal = _iota((tq, tq), 1) <= _iota((tq, tq), 0)
    blk_id = _iota((tq, nb), 1)
    for pair in range(HEADS_B // 2):
        sl = slice(pair * LANES, (pair + 1) * LANES)
        vsl = slice(B_W + pair * LANES, B_W + (pair + 1) * LANES)
        q2 = q_ref[:, sl]
        q2f = q32_ref[:, sl]
        km = kmean_ref[:, sl]
        outs = []
        for e in range(2):
            in_h = (lane >= e * HEAD_DIM) & (lane < (e + 1) * HEAD_DIM)
            qm = jnp.where(in_h, q2, jnp.zeros_like(q2))
            gate = _nt_precise(jnp.where(in_h, q2f, 0.0), km)
            past = blk_id < i
            gate = jnp.where(past, gate, NEG_INF)
            chosen = jnp.where((_rank_lower(gate, nb, blk_id) < MOBA_TOPK) & past, 1.0, 0.0)

            kd = kv_ref[pl.ds(pl.multiple_of(i * tq, tq), tq), sl]
            vd = kv_ref[pl.ds(pl.multiple_of(i * tq, tq), tq), vsl]
            s = jnp.where(causal, _nt(qm, kd), NEG_INF)
            m, l, p = _softmax_start(s)
            acc = _nn(p.astype(BF16), vd)

            def body(j, carry, qm=qm, chosen=chosen, sl=sl, vsl=vsl):
                m, l, acc = carry
                kj = kv_ref[pl.ds(pl.multiple_of(j * tq, tq), tq), sl]
                vj = kv_ref[pl.ds(pl.multiple_of(j * tq, tq), tq), vsl]
                use = jnp.max(jnp.where(blk_id == j, chosen, 0.0), axis=1, keepdims=True)
                s = jnp.where(use > 0.5, _nt(qm, kj), NEG_INF)
                m, alpha, l, p = _softmax_step(s, m, l)
                return m, l, alpha * acc + _nn(p.astype(BF16), vj)

            m, l, acc = lax.fori_loop(0, i, body, (m, l, acc))
            outs.append(acc / l)
        o_ref[:, sl] = jnp.where(lane < HEAD_DIM, outs[0], outs[1]).astype(BF16)


def _attn_b_prompt(q16, q32, bkv32, bkv16, n, s):
    nq = s // TQ
    assert TQ == MOBA_BLOCK
    qspec = pl.BlockSpec((TQ, B_W), lambda b, i: (b * nq + i, 0))
    return pl.pallas_call(
        _attn_b_prompt_body,
        out_shape=jax.ShapeDtypeStruct((n * s, B_W), BF16),
        grid=(n, nq),
        in_specs=[qspec, qspec,
                  pl.BlockSpec((s, B_W), lambda b, i: (b, 0)),
                  pl.BlockSpec((s, 2 * B_W), lambda b, i: (b, 0))],
        out_specs=qspec,
        scratch_shapes=[pltpu.VMEM((s // MOBA_BLOCK, B_W), F32)],
        compiler_params=_params(2),
        name="moba_attn_prompt",
    )(q16, q32, bkv32, bkv16)


def _compress_core(slab, pe_ref, w1_ref, w2_ref, o_ref):
    half = CMP_LEN // 2
    acc = [None, None]
    for r in range(CMP_LEN):
        x = (slab(r % half) + pe_ref[r:r + 1, :]).astype(BF16)
        t = _nn(x, w1_ref[r])
        acc[r // half] = t if acc[r // half] is None else acc[r // half] + t
    n_chunk = acc[0].shape[0]
    pre = acc[0] + pltpu.roll(acc[1], n_chunk - 1, 0)
    hid = jax.nn.gelu(pre).astype(BF16)
    o_ref[...] = _nn(hid, w2_ref[...]).astype(BF16)


def _compress_prompt_body(x_ref, pe_ref, w1_ref, w2_ref, o_ref):
    n_chunk = x_ref.shape[0] // CMP_STRIDE
    slab = lambda r: x_ref[pl.ds(r, n_chunk, stride=CMP_STRIDE), :]
    _compress_core(slab, pe_ref, w1_ref, w2_ref, o_ref)


def _compress_pages_body(pt_ref, pe_ref, w1_ref, w2_ref, *refs):
    pages, o_ref = refs[:-1], refs[-1]
    per_page = PAGE_SIZE // CMP_STRIDE
    slab = lambda r: jnp.concatenate(
        [pg[pl.ds(r, per_page, stride=CMP_STRIDE), :] for pg in pages], axis=0)
    _compress_core(slab, pe_ref, w1_ref, w2_ref, o_ref)


def _compress_weights(w_cmp1, w_cmp2, cmp_pos):
    w1 = w_cmp1.reshape(2, CMP_LEN, HEAD_DIM, CMP_HIDDEN)
    z1 = jnp.zeros((CMP_LEN, HEAD_DIM, CMP_HIDDEN), F32)
    top = jnp.concatenate([w1[0], z1], axis=2)
    bot = jnp.concatenate([z1, w1[1]], axis=2)
    w1c = jnp.concatenate([top, bot], axis=1).astype(BF16)
    z2 = jnp.zeros((CMP_HIDDEN, HEAD_DIM), F32)
    w2c = jnp.concatenate([jnp.concatenate([w_cmp2[0], z2], axis=1),
                           jnp.concatenate([z2, w_cmp2[1]], axis=1)], axis=0).astype(BF16)
    pe = jnp.concatenate([cmp_pos[0], cmp_pos[1]], axis=1)
    return pe, w1c, w2c


_CMP_ROWS = 128


def _const_specs(index):
    return [pl.BlockSpec((CMP_LEN, LANES), index(2)),
            pl.BlockSpec((CMP_LEN, LANES, 2 * CMP_HIDDEN), index(3)),
            pl.BlockSpec((2 * CMP_HIDDEN, LANES), index(2))]


def _compress_prompt(ckv32, cw, n, s):
    assert s // CMP_STRIDE == _CMP_ROWS
    zero = lambda nd: (lambda b: (0,) * nd)
    return pl.pallas_call(
        _compress_prompt_body,
        out_shape=jax.ShapeDtypeStruct((n * _CMP_ROWS, LANES), BF16),
        grid=(n,),
        in_specs=[pl.BlockSpec((s, LANES), lambda b: (b, 0))] + _const_specs(zero),
        out_specs=pl.BlockSpec((_CMP_ROWS, LANES), lambda b: (b, 0)),
        compiler_params=_params(1),
        name="nsa_compress_prompt",
    )(ckv32, *cw)


def _page_index(l, j, b, pt):
    return (l, pt[b, j], 0, 0)


def _page_specs(l, n_pages, width):
    return [pl.BlockSpec((None, None, PAGE_SIZE, width), functools.partial(_page_index, l, j))
            for j in range(n_pages)]


def _compress_pages(cache_c, page_table, cw, l):
    n, n_pages = page_table.shape
    assert n_pages * PAGE_SIZE // CMP_STRIDE == _CMP_ROWS
    zero = lambda nd: (lambda b, pt: (0,) * nd)
    grid_spec = pltpu.PrefetchScalarGridSpec(
        num_scalar_prefetch=1, grid=(n,),
        in_specs=_const_specs(zero) + _page_specs(l, n_pages, LANES),
        out_specs=pl.BlockSpec((_CMP_ROWS, LANES), lambda b, pt: (b, 0)))
    return pl.pallas_call(
        _compress_pages_body,
        out_shape=jax.ShapeDtypeStruct((n * _CMP_ROWS, LANES), BF16),
        grid_spec=grid_spec,
        compiler_params=_params(1),
        name="nsa_compress_pages",
    )(page_table, *cw, *([cache_c] * n_pages))


def _overlap_matrix(t_len):
    n_cmp = (t_len - CMP_LEN) // CMP_STRIDE + 1
    nsb = -(-t_len // SEL_BLOCK)
    starts = np.arange(n_cmp) * CMP_STRIDE
    sb = np.arange(nsb) * SEL_BLOCK
    ov = np.clip(np.minimum(starts[:, None] + CMP_LEN, sb[None, :] + SEL_BLOCK)
                 - np.maximum(starts[:, None], sb[None, :]), 0, None) / CMP_STRIDE
    out = np.zeros((_CMP_ROWS, LANES), np.float32)
    out[:n_cmp, :nsb] = ov
    return jnp.asarray(out, BF16), nsb


def _expand_matrix(s):
    nt = s // TQ
    e = np.zeros((nt, LANES, TQ), np.float32)
    for j in range(nt):
        for k in range(TQ):
            e[j, (j * TQ + k) // SEL_BLOCK, k] = 1.0
    return jnp.asarray(e, BF16)


def _nsa_flags(p_sum, ovl, own, lane, nsb):
    imp = None
    for part in _split3(p_sum):
        t = _nn(part, ovl)
        imp = t if imp is None else imp + t
    forced = (lane == 0) | (lane == own) | (lane == own - 1)
    imp = jnp.where(lane > own, NEG_INF, jnp.where(forced, jnp.inf, imp))
    rank = _rank_lower(imp, nsb, lane)
    return jnp.where((rank < SEL_TOPK) & (lane <= own), 1.0, 0.0)


def _masked_probs(s, mask):
    s = jnp.where(mask, s, NEG_INF)
    m = jnp.max(s, axis=-1, keepdims=True)
    m = jnp.where(m > NEG_INF, m, 0.0)
    e = jnp.exp(s - m)
    d = jnp.sum(e, axis=-1, keepdims=True)
    return e / jnp.where(d > 0, d, 1.0)


def _attn_c_prompt_body(qz_ref, gate_ref, ckv_ref, win_ref, kvc_ref, ovl_ref, exp_ref, o_ref, *, nsb):
    i = pl.program_id(1)
    tq = gate_ref.shape[0]
    nh = HEADS_C
    q4 = jnp.concatenate([qz_ref[:, h * LANES:(h + 1) * LANES] for h in range(nh)], axis=0)
    lane = _iota((tq, LANES), 1)
    pos = i * tq + _iota((tq, LANES), 0)
    rr = _iota((tq, tq), 0)
    cc = _iota((tq, tq), 1)
    causal = cc <= rr

    kvc = kvc_ref[...]
    cmp_ok = (CMP_STRIDE * lane + (CMP_LEN - 1)) <= pos
    s = _nt(q4, kvc).reshape(nh, tq, LANES)
    p = _masked_probs(s, cmp_ok[None])
    o_cmp = _nn(p.reshape(nh * tq, LANES).astype(BF16), kvc)
    p_sum = p[0] + p[1] + p[2] + p[3]

    own = pos // SEL_BLOCK
    flag = _nsa_flags(p_sum, ovl_ref[...], own, lane, nsb).astype(BF16)

    def kv_tile(ref, j, lo, hi):
        return ref[pl.ds(pl.multiple_of(j * tq, tq), tq), lo:hi]

    kd = kv_tile(ckv_ref, i, LANES, 2 * LANES)
    ok = (_nn(flag, exp_ref[i]) > 0.5) & causal
    s = jnp.where(ok[None], _nt(q4, kd).reshape(nh, tq, tq), NEG_INF)
    m, l, p = _softmax_start(s)
    acc = _nn(p.reshape(nh * tq, tq).astype(BF16), kd).reshape(nh, tq, LANES)

    def body(j, carry):
        m, l, acc = carry
        kj = kv_tile(ckv_ref, j, LANES, 2 * LANES)
        ok = _nn(flag, exp_ref[j]) > 0.5
        s = jnp.where(ok[None], _nt(q4, kj).reshape(nh, tq, tq), NEG_INF)
        m, alpha, l, p = _softmax_step(s, m, l)
        pv = _nn(p.reshape(nh * tq, tq).astype(BF16), kj).reshape(nh, tq, LANES)
        return m, l, alpha * acc + pv

    m, l, acc = lax.fori_loop(0, i, body, (m, l, acc))
    o_sel = acc / l

    assert WINDOW == 2 * tq
    w2 = kv_tile(win_ref, jnp.maximum(i - 2, 0), 0, LANES)
    w1 = kv_tile(win_ref, jnp.maximum(i - 1, 0), 0, LANES)
    w0 = kv_tile(win_ref, i, 0, LANES)
    s2 = jnp.where(((cc >= rr) & (i >= 2))[None], _nt(q4, w2).reshape(nh, tq, tq), NEG_INF)
    s1 = jnp.where(i >= 1, _nt(q4, w1).reshape(nh, tq, tq), NEG_INF)
    s0 = jnp.where(causal[None], _nt(q4, w0).reshape(nh, tq, tq), NEG_INF)
    m = jnp.maximum(jnp.maximum(jnp.max(s2, -1, keepdims=True), jnp.max(s1, -1, keepdims=True)),
                    jnp.max(s0, -1, keepdims=True))
    e2, e1, e0 = jnp.exp(s2 - m), jnp.exp(s1 - m), jnp.exp(s0 - m)
    d = jnp.sum(e2, -1, keepdims=True) + jnp.sum(e1, -1, keepdims=True) + jnp.sum(e0, -1, keepdims=True)
    flat = lambda e: e.reshape(nh * tq, tq).astype(BF16)
    o_win = (_nn(flat(e2), w2) + _nn(flat(e1), w1) + _nn(flat(e0), w0)).reshape(nh, tq, LANES) / d

    o_cmp = o_cmp.reshape(nh, tq, LANES)
    g = gate_ref[...]
    heads = []
    for h in range(nh):
        heads.append(g[:, 3 * h:3 * h + 1] * o_cmp[h] + g[:, 3 * h + 1:3 * h + 2] * o_sel[h]
                     + g[:, 3 * h + 2:3 * h + 3] * o_win[h])
    for pair in range(nh // 2):
        both = jnp.where(lane < HEAD_DIM, pltpu.roll(heads[2 * pair], HEAD_DIM, 1), heads[2 * pair + 1])
        o_ref[:, pair * LANES:(pair + 1) * LANES] = both.astype(BF16)


def _attn_c_prompt(qz16, gates, ckv16, win16, kvcmp16, n, s):
    nq = s // TQ
    ovl, nsb = _overlap_matrix(s)
    expand = _expand_matrix(s)
    qrow = lambda w: pl.BlockSpec((TQ, w), lambda b, i: (b * nq + i, 0))
    seq = lambda rows, w: pl.BlockSpec((rows, w), lambda b, i: (b, 0))
    return pl.pallas_call(
        functools.partial(_attn_c_prompt_body, nsb=nsb),
        out_shape=jax.ShapeDtypeStruct((n * s, C_W), BF16),
        grid=(n, nq),
        in_specs=[qrow(2 * C_W), qrow(LANES), seq(s, 4 * HEAD_DIM), seq(s, 2 * HEAD_DIM),
                  seq(_CMP_ROWS, LANES),
                  pl.BlockSpec((_CMP_ROWS, LANES), lambda b, i: (0, 0)),
                  pl.BlockSpec((nq, LANES, TQ), lambda b, i: (0, 0, 0))],
        out_specs=qrow(C_W),
        compiler_params=_params(2),
        name="nsa_attn_prompt",
    )(qz16, gates, ckv16, win16, kvcmp16, ovl, expand)


def _pad_page(x):
    rows, w = x.shape
    return jnp.concatenate([x, jnp.zeros((PAGE_SIZE - rows, w), x.dtype)], axis=0)


def _new_page_mask(n_rows, nq):
    r = _iota((n_rows, PAGE_SIZE), 0) % nq
    t = _iota((n_rows, PAGE_SIZE), 1)
    return t <= r


def _dec_a_body(pt_ref, q_ref, kn_ref, vn_ref, lam_ref, gh_ref, *refs, lam_init):
    n_pages = (len(refs) - 1) // 2
    kp, vp, o_ref = refs[:n_pages], refs[n_pages:2 * n_pages], refs[-1]
    nq = q_ref.shape[0]
    rows = 2 * HEADS_A * nq
    lam = _lam_value(lam_ref, lam_init)
    qt = jnp.concatenate([q_ref[...]] * (2 * HEADS_A), axis=0)
    diag = (_iota((rows, A_W), 0) // nq) == (_iota((rows, A_W), 1) // HEAD_DIM)
    qbd = jnp.where(diag, qt, jnp.zeros_like(qt))
    kn = _pad_page(kn_ref[...]).astype(BF16)
    vn = _pad_page(vn_ref[...]).astype(BF16)
    s_new = jnp.where(_new_page_mask(rows, nq), _nt(qbd, kn), NEG_INF)
    scores = [_nt(qbd, kp[j][...].astype(BF16)) for j in range(n_pages)]
    m = jnp.max(s_new, axis=1, keepdims=True)
    for s in scores:
        m = jnp.maximum(m, jnp.max(s, axis=1, keepdims=True))
    e = jnp.exp(s_new - m)
    d = jnp.sum(e, axis=1, keepdims=True)
    acc = _nn(e.astype(BF16), vn)
    for j in range(n_pages):
        e = jnp.exp(scores[j] - m)
        d = d + jnp.sum(e, axis=1, keepdims=True)
        acc = acc + _nn(e.astype(BF16), vp[j][...].astype(BF16))
    on = acc / d
    head_of_lane = _iota((nq, A_W), 1) // (2 * HEAD_DIM)
    w0 = jnp.zeros((nq, A_W), F32)
    w1 = jnp.zeros((nq, A_W), F32)
    for h in range(HEADS_A):
        w0 = w0 + jnp.where(head_of_lane == h, on[2 * h * nq:(2 * h + 1) * nq], 0.0)
        w1 = w1 + jnp.where(head_of_lane == h, on[(2 * h + 1) * nq:(2 * h + 2) * nq], 0.0)
    o = w0 - lam * w1
    gh = gh_ref[...]
    for h in range(HEADS_A):
        sl = slice(h * LANES, (h + 1) * LANES)
        o_ref[:, sl] = _head_rms_scale(o[:, sl], gh, lam_init).astype(BF16)


def _dec_a(q16, kn32, vn32, lam_p, g_head, cache_k, cache_v, page_table, l, nq):
    n, n_pages = page_table.shape
    row = lambda w: pl.BlockSpec((nq, w), lambda b, pt: (b, 0))
    grid_spec = pltpu.PrefetchScalarGridSpec(
        num_scalar_prefetch=1, grid=(n,),
        in_specs=[row(A_W), row(A_W), row(A_W),
                  pl.BlockSpec((4, HEAD_DIM), lambda b, pt: (0, 0)),
                  pl.BlockSpec((1, 2 * HEAD_DIM), lambda b, pt: (0, 0))]
                 + _page_specs(l, n_pages, A_W) + _page_specs(l, n_pages, A_W),
        out_specs=row(A_W))
    return pl.pallas_call(
        functools.partial(_dec_a_body, lam_init=_lam_init(l)),
        out_shape=jax.ShapeDtypeStruct((n * nq, A_W), BF16),
        grid_spec=grid_spec,
        compiler_params=_params(1),
        name="diff_attn_decode",
    )(page_table, q16, kn32, vn32, lam_p, g_head.reshape(1, 2 * HEAD_DIM),
      *([cache_k] * n_pages), *([cache_v] * n_pages))


def _dec_b_body(pt_ref, q_ref, q32_ref, kvn_ref, *refs):
    pages, o_ref = refs[:-1], refs[-1]
    n_pages = len(pages)
    nq = q_ref.shape[0]
    rows = HEADS_B * nq
    pages_per_blk = MOBA_BLOCK // PAGE_SIZE
    nb = n_pages // pages_per_blk
    diag = (_iota((rows, B_W), 0) // nq) == (_iota((rows, B_W), 1) // HEAD_DIM)
    qt = jnp.concatenate([q_ref[...]] * HEADS_B, axis=0)
    qbd = jnp.where(diag, qt, jnp.zeros_like(qt))
    qbd32 = jnp.where(diag, jnp.concatenate([q32_ref[...]] * HEADS_B, axis=0), 0.0)

    means = []
    for b in range(nb):
        tot = None
        for j in range(b * pages_per_blk, (b + 1) * pages_per_blk):
            t = jnp.sum(pages[j][:, 0:B_W], axis=0, keepdims=True)
            tot = t if tot is None else tot + t
        means.append(tot * (1.0 / MOBA_BLOCK))
    kmean = jnp.concatenate(means, axis=0)
    gate = _nt_precise(qbd32, kmean)
    chosen = jnp.where(_rank_lower(gate, nb, _iota((rows, nb), 1)) < MOBA_TOPK, 1.0, 0.0)

    kvn = _pad_page(kvn_ref[...]).astype(BF16)
    s_new = jnp.where(_new_page_mask(rows, nq), _nt(qbd, kvn[:, 0:B_W]), NEG_INF)
    scores = []
    for j in range(n_pages):
        b = j // pages_per_blk
        s = _nt(qbd, pages[j][:, 0:B_W].astype(BF16))
        scores.append(jnp.where(chosen[:, b:b + 1] > 0.5, s, NEG_INF))
    m = jnp.max(s_new, axis=1, keepdims=True)
    for s in scores:
        m = jnp.maximum(m, jnp.max(s, axis=1, keepdims=True))
    e = jnp.exp(s_new - m)
    d = jnp.sum(e, axis=1, keepdims=True)
    acc = _nn(e.astype(BF16), kvn[:, B_W:2 * B_W])
    for j in range(n_pages):
        e = jnp.exp(scores[j] - m)
        d = d + jnp.sum(e, axis=1, keepdims=True)
        acc = acc + _nn(e.astype(BF16), pages[j][:, B_W:2 * B_W].astype(BF16))
    on = acc / d
    head_of_lane = _iota((nq, B_W), 1) // HEAD_DIM
    o = jnp.zeros((nq, B_W), F32)
    for h in range(HEADS_B):
        o = o + jnp.where(head_of_lane == h, on[h * nq:(h + 1) * nq], 0.0)
    o_ref[...] = o.astype(BF16)


def _dec_b(q16, q32, kvn32, cache_kv, page_table, l, nq):
    n, n_pages = page_table.shape
    row = lambda w: pl.BlockSpec((nq, w), lambda b, pt: (b, 0))
    grid_spec = pltpu.PrefetchScalarGridSpec(
        num_scalar_prefetch=1, grid=(n,),
        in_specs=[row(B_W), row(B_W), row(2 * B_W)] + _page_specs(l, n_pages, 2 * B_W),
        out_specs=row(B_W))
    return pl.pallas_call(
        _dec_b_body,
        out_shape=jax.ShapeDtypeStruct((n * nq, B_W), BF16),
        grid_spec=grid_spec,
        compiler_params=_params(1),
        name="moba_attn_decode",
    )(page_table, q16, q32, kvn32, *([cache_kv] * n_pages))


def _dec_c_body(pt_ref, qz_ref, gate_ref, ckvn_ref, winn_ref, kvc_ref, ovl_ref, st_ref, *refs, nsb, q0):
    pages, o_ref = refs[:-1], refs[-1]
    n_pages = len(pages)
    nq = gate_ref.shape[0]
    nh = HEADS_C
    rows = nh * nq
    q4 = jnp.concatenate([qz_ref[:, h * LANES:(h + 1) * LANES] for h in range(nh)], axis=0)
    lane = _iota((nq, LANES), 1)
    pos = q0 + _iota((nq, LANES), 0)
    lane4 = _iota((rows, LANES), 1)
    qrow4 = _iota((rows, LANES), 0) % nq
    new_ok = _new_page_mask(rows, nq)

    kvc = kvc_ref[...]
    cmp_ok = (CMP_STRIDE * lane4 + (CMP_LEN - 1)) <= (q0 + qrow4)
    p = _masked_probs(_nt(q4, kvc), cmp_ok)
    o_cmp = _nn(p.astype(BF16), kvc)
    p_sum = p[0:nq]
    for h in range(1, nh):
        p_sum = p_sum + p[h * nq:(h + 1) * nq]

    own = pos // SEL_BLOCK
    flag = _nsa_flags(p_sum, ovl_ref[...], own, lane, nsb)
    flag4 = jnp.concatenate([flag] * nh, axis=0)
    blk_per_page = PAGE_SIZE // SEL_BLOCK
    assert blk_per_page == 2

    ckvn = _pad_page(ckvn_ref[...]).astype(BF16)
    kn = ckvn[:, LANES:2 * LANES]
    own_blk = n_pages * blk_per_page
    s_new = jnp.where(new_ok & (flag4[:, own_blk:own_blk + 1] > 0.5), _nt(q4, kn), NEG_INF)
    tiles = [pages[j][:, LANES:2 * LANES].astype(BF16) for j in range(n_pages)]
    scores = []
    for j in range(n_pages):
        ok = jnp.where(lane4 < SEL_BLOCK, flag4[:, 2 * j:2 * j + 1], flag4[:, 2 * j + 1:2 * j + 2]) > 0.5
        scores.append(jnp.where(ok, _nt(q4, tiles[j]), NEG_INF))
    m = jnp.max(s_new, axis=1, keepdims=True)
    for s in scores:
        m = jnp.maximum(m, jnp.max(s, axis=1, keepdims=True))
    e = jnp.exp(s_new - m)
    d = jnp.sum(e, axis=1, keepdims=True)
    acc = _nn(e.astype(BF16), kn)
    for j in range(n_pages):
        e = jnp.exp(scores[j] - m)
        d = d + jnp.sum(e, axis=1, keepdims=True)
        acc = acc + _nn(e.astype(BF16), tiles[j])
    o_sel = acc / d

    wn = _pad_page(winn_ref[...]).astype(BF16)
    s_new = jnp.where(new_ok, _nt(q4, wn), NEG_INF)
    n_chunks = st_ref.shape[0] // PAGE_SIZE
    chunks = [st_ref[c * PAGE_SIZE:(c + 1) * PAGE_SIZE, :].astype(BF16) for c in range(n_chunks)]
    scores = []
    for c in range(n_chunks):
        ok = (c * PAGE_SIZE + lane4) >= qrow4
        scores.append(jnp.where(ok, _nt(q4, chunks[c]), NEG_INF))
    m = jnp.max(s_new, axis=1, keepdims=True)
    for s in scores:
        m = jnp.maximum(m, jnp.max(s, axis=1, keepdims=True))
    e = jnp.exp(s_new - m)
    d = jnp.sum(e, axis=1, keepdims=True)
    acc = _nn(e.astype(BF16), wn)
    for c in range(n_chunks):
        e = jnp.exp(scores[c] - m)
        d = d + jnp.sum(e, axis=1, keepdims=True)
        acc = acc + _nn(e.astype(BF16), chunks[c])
    o_win = acc / d

    g = gate_ref[...]
    heads = []
    for h in range(nh):
        r = slice(h * nq, (h + 1) * nq)
        heads.append(g[:, 3 * h:3 * h + 1] * o_cmp[r] + g[:, 3 * h + 1:3 * h + 2] * o_sel[r]
                     + g[:, 3 * h + 2:3 * h + 3] * o_win[r])
    for pair in range(nh // 2):
        both = jnp.where(lane < HEAD_DIM, pltpu.roll(heads[2 * pair], HEAD_DIM, 1), heads[2 * pair + 1])
        o_ref[:, pair * LANES:(pair + 1) * LANES] = both.astype(BF16)


def _dec_c(qz16, gates, ckvn32, winn32, kvcmp16, state_win, cache_c, page_table, l, nq, q0):
    n, n_pages = page_table.shape
    assert state_win.shape[2] == WINDOW and q0 >= WINDOW
    ovl, nsb = _overlap_matrix(q0 + nq)
    row = lambda w: pl.BlockSpec((nq, w), lambda b, pt: (b, 0))
    grid_spec = pltpu.PrefetchScalarGridSpec(
        num_scalar_prefetch=1, grid=(n,),
        in_specs=[row(2 * C_W), row(LANES), row(4 * HEAD_DIM), row(2 * HEAD_DIM),
                  pl.BlockSpec((_CMP_ROWS, LANES), lambda b, pt: (b, 0)),
                  pl.BlockSpec((_CMP_ROWS, LANES), lambda b, pt: (0, 0)),
                  pl.BlockSpec((None, None, WINDOW, 2 * HEAD_DIM), lambda b, pt: (l, b, 0, 0))]
                 + _page_specs(l, n_pages, 4 * HEAD_DIM),
        out_specs=row(C_W))
    return pl.pallas_call(
        functools.partial(_dec_c_body, nsb=nsb, q0=q0),
        out_shape=jax.ShapeDtypeStruct((n * nq, C_W), BF16),
        grid_spec=grid_spec,
        compiler_params=_params(1),
        name="nsa_attn_decode",
    )(page_table, qz16, gates, ckvn32, winn32, kvcmp16, ovl, state_win, *([cache_c] * n_pages))


def _rope_tables(n_pos):
    inv = ROPE_THETA ** (-jnp.arange(0, HEAD_DIM, 2, dtype=F32) / HEAD_DIM)
    ang = jnp.arange(n_pos, dtype=F32)[:, None] * inv[None, :]
    cos, sin = jnp.cos(ang), jnp.sin(ang)
    cos128 = jnp.concatenate([cos, cos, cos, cos], axis=-1)
    sin128 = jnp.concatenate([-sin, sin, -sin, sin], axis=-1)
    return cos128, sin128


def kernel(x_prompt, x_sample, cache_a_k, cache_a_v, cache_b_kv, cache_c_kv, state_c_win, page_table,
           w_in, w_out, g_mix, g_ffn, w_ffn_gate, w_ffn_up, w_ffn_down, diff_lambda, g_diff_head,
           w_cmp1, w_cmp2, cmp_pos, g_final):
    n_p, s_p, _ = x_prompt.shape
    n_s, s_s, _ = x_sample.shape
    n_pages = page_table.shape[1]
    past_len = n_pages * cache_a_k.shape[2]
    n_phys = cache_a_k.shape[1]
    assert cache_a_k.shape[2] == PAGE_SIZE and s_p % TM_PROJ == 0 and TM_PROJ % s_s == 0

    cos, sin = _rope_tables(past_len + s_s)
    cos_p, sin_p = cos[:s_p], sin[:s_p]
    reps = TM_PROJ // s_s
    cos_s = jnp.tile(cos[past_len:past_len + s_s], (reps, 1))
    sin_s = jnp.tile(sin[past_len:past_len + s_s], (reps, 1))
    tab_blocks = s_p // TM_PROJ
    idx_p = lambda i: (i % tab_blocks, 0)
    idx_s = lambda i: (0, 0)

    ck = cache_a_k.reshape(DEPTH, n_phys, PAGE_SIZE, A_W)
    cv = cache_a_v.reshape(DEPTH, n_phys, PAGE_SIZE, A_W)
    cb = cache_b_kv.reshape(DEPTH, n_phys, PAGE_SIZE, 2 * B_W)
    cc = cache_c_kv.reshape(DEPTH, n_phys, PAGE_SIZE, 4 * HEAD_DIM)
    st = state_c_win.reshape(DEPTH, n_s, WINDOW, 2 * HEAD_DIM)

    hp = x_prompt.reshape(n_p * s_p, D_MODEL)
    hs = x_sample.reshape(n_s * s_s, D_MODEL)
    ent_p, ent_s = [], []
    for l in range(DEPTH):
        wg = w_ffn_gate[l].astype(BF16)
        wu = w_ffn_up[l].astype(BF16)
        wd = w_ffn_down[l].astype(BF16)
        w_main = w_in[l][:, :MAIN_W].astype(BF16)
        w_gate = jnp.pad(w_in[l][:, MAIN_W:], ((0, 0), (0, LANES - GATE_W))).astype(BF16)
        wo = w_out[l].astype(BF16)
        cw = _compress_weights(w_cmp1[l], w_cmp2[l], cmp_pos[l])
        last = l == DEPTH - 1

        hp = _ffn(hp, g_ffn[l, 0], wg[0], wu[0], wd[0])
        (ka, va, bkv, ckv, win, qa16, ka16, va16, qb16, qb32, bkv16, qcz16, ckv16, win16, gates) = _proj(
            hp, g_mix[l], w_main, w_gate, cos_p, sin_p, idx_p)
        oa = _attn_a_prompt(qa16, ka16, va16, diff_lambda[l], g_diff_head[l], n_p, s_p, l)
        ob = _attn_b_prompt(qb16, qb32, bkv, bkv16, n_p, s_p)
        kvcmp = _compress_prompt(ckv, cw, n_p, s_p)
        oc = _attn_c_prompt(qcz16, gates, ckv16, win16, kvcmp, n_p, s_p)
        hp = _outproj(hp, oa, ob, oc, wo)
        hp = _ffn(hp, g_ffn[l, 1], wg[1], wu[1], wd[1], g_final if last else None)
        win_keep = min(WINDOW, s_p)
        ent_p.append((ka.reshape(n_p, s_p, HEADS_A, 2, HEAD_DIM), va.reshape(n_p, s_p, HEADS_A, 2 * HEAD_DIM),
                      bkv.reshape(n_p, s_p, 2, HEADS_B, HEAD_DIM), ckv.reshape(n_p, s_p, 4, HEAD_DIM),
                      win.reshape(n_p, s_p, 2, HEAD_DIM)[:, s_p - win_keep:]))

        hs = _ffn(hs, g_ffn[l, 0], wg[0], wu[0], wd[0])
        (ka, va, bkv, ckv, win, qa16, _, _, qb16, qb32, _, qcz16, _, _, gates) = _proj(
            hs, g_mix[l], w_main, w_gate, cos_s, sin_s, idx_s)
        oa = _dec_a(qa16, ka, va, diff_lambda[l], g_diff_head[l], ck, cv, page_table, l, s_s)
        ob = _dec_b(qb16, qb32, bkv, cb, page_table, l, s_s)
        kvcmp = _compress_pages(cc, page_table, cw, l)
        oc = _dec_c(qcz16, gates, ckv, win, kvcmp, st, cc, page_table, l, s_s, past_len)
        hs = _outproj(hs, oa, ob, oc, wo)
        hs = _ffn(hs, g_ffn[l, 1], wg[1], wu[1], wd[1], g_final if last else None)
        win_new = win.reshape(n_s, s_s, 2, HEAD_DIM)
        win_all = jnp.concatenate([state_c_win[l], win_new], axis=1)
        ent_s.append((ka.reshape(n_s, s_s, HEADS_A, 2, HEAD_DIM), va.reshape(n_s, s_s, HEADS_A, 2 * HEAD_DIM),
                      bkv.reshape(n_s, s_s, 2, HEADS_B, HEAD_DIM), ckv.reshape(n_s, s_s, 4, HEAD_DIM),
                      win_all[:, win_all.shape[1] - min(WINDOW, win_all.shape[1]):]))

    st_ = lambda ents, i: jnp.stack([e[i] for e in ents], axis=0)
    return (hp.reshape(n_p, s_p, D_MODEL), hs.reshape(n_s, s_s, D_MODEL),
            st_(ent_p, 0), st_(ent_s, 0), st_(ent_p, 1), st_(ent_s, 1),
            st_(ent_p, 2), st_(ent_s, 2), st_(ent_p, 3), st_(ent_s, 3),
            st_(ent_p, 4), st_(ent_s, 4))
```

```python
import functools
import math

import numpy as np
import jax
import jax.numpy as jnp
from jax import lax
from jax.experimental import pallas as pl
from jax.experimental.pallas import tpu as pltpu

F32 = jnp.float32
BF16 = jnp.bfloat16

D_MODEL = 1024
DEPTH = 2
HEAD_DIM = 64
HEADS_A = 4
HEADS_B = 4
HEADS_C = 4
D_FF = 2816
ROPE_THETA = 10000.0
MOBA_BLOCK = 256
MOBA_TOPK = 3
CMP_LEN = 32
CMP_STRIDE = 16
CMP_HIDDEN = 4 * HEAD_DIM
SEL_BLOCK = 64
SEL_TOPK = 16
WINDOW = 512
RMS_EPS = 1e-6
PAGE_SIZE = 128

A_W = HEADS_A * 2 * HEAD_DIM
B_W = HEADS_B * HEAD_DIM
C_W = HEADS_C * HEAD_DIM
KVC_W = 6 * HEAD_DIM
GATE_W = 3 * HEADS_C
MAIN_W = 3 * A_W + 3 * B_W + C_W + KVC_W
LANES = 128
QK_SCALE = HEAD_DIM ** -0.5
NEG_INF = float("-inf")
VMEM_LIMIT = 56 * 1024 * 1024

TM_FFN = 512
TF_FFN = 256
TM_PROJ = 512
TQ = 256
CMP_GROUP = 4

_O_QA, _O_KA, _O_VA = 0, A_W, 2 * A_W
_O_QB = 3 * A_W
_O_KB, _O_VB = _O_QB + B_W, _O_QB + 2 * B_W
_O_QC = _O_QB + 3 * B_W
_O_KVC = _O_QC + C_W


def _nn(a, b):
    return jnp.dot(a, b, preferred_element_type=F32)


def _nt(a, b):
    return lax.dot_general(a, b, (((1,), (1,)), ((), ())), preferred_element_type=F32)


def _split3(x):
    hi = x.astype(BF16)
    r1 = x - hi.astype(F32)
    mid = r1.astype(BF16)
    lo = (r1 - mid.astype(F32)).astype(BF16)
    return hi, mid, lo


def _nn_precise(a, b):
    a_hi, a_mid, _ = _split3(a)
    b_hi, b_mid, _ = _split3(b)
    return _nn(a_hi, b_hi) + (_nn(a_hi, b_mid) + _nn(a_mid, b_hi))


def _rms(x):
    return x * lax.rsqrt(jnp.mean(x * x, axis=-1, keepdims=True) + RMS_EPS)


def _iota(shape, dim):
    return lax.broadcasted_iota(jnp.int32, shape, dim)


def _params(n_axes):
    return pltpu.CompilerParams(dimension_semantics=("arbitrary",) * n_axes,
                                vmem_limit_bytes=VMEM_LIMIT)


def _lam_value(lam_ref, lam_init):
    lp = lam_ref[...]
    a = jnp.sum(lp[0:1] * lp[1:2], axis=1, keepdims=True)
    b = jnp.sum(lp[2:3] * lp[3:4], axis=1, keepdims=True)
    return jnp.exp(a) - jnp.exp(b) + lam_init


def _lam_init(lidx):
    return 0.8 - 0.6 * math.exp(-0.3 * lidx)


def _ffn_body(x_ref, g_ref, wg_ref, wu_ref, wd_ref, *rest, final):
    if final:
        gf_ref, o_ref = rest
    else:
        (o_ref,) = rest
    x = x_ref[...]
    xn = (_rms(x) * g_ref[...]).astype(BF16)
    acc = jnp.zeros_like(x)
    for f in range(D_FF // TF_FFN):
        sl = slice(f * TF_FFN, (f + 1) * TF_FFN)
        g = _nn(xn, wg_ref[:, sl])
        u = _nn(xn, wu_ref[:, sl])
        a = (g * jax.nn.sigmoid(g) * u).astype(BF16)
        acc = acc + _nn(a, wd_ref[sl, :])
    y = x + 0.5 * acc
    if final:
        y = _rms(y) * gf_ref[...]
    o_ref[...] = y


def _ffn(x, g, wg, wu, wd, g_final=None):
    m = x.shape[0]
    row = pl.BlockSpec((TM_FFN, D_MODEL), lambda i: (i, 0))
    vec = pl.BlockSpec((1, D_MODEL), lambda i: (0, 0))
    full = lambda shape: pl.BlockSpec(shape, lambda i: (0, 0))
    in_specs = [row, vec, full((D_MODEL, D_FF)), full((D_MODEL, D_FF)), full((D_FF, D_MODEL))]
    args = [x, g.reshape(1, D_MODEL), wg, wu, wd]
    if g_final is not None:
        in_specs.append(vec)
        args.append(g_final.reshape(1, D_MODEL))
    return pl.pallas_call(
        functools.partial(_ffn_body, final=g_final is not None),
        out_shape=jax.ShapeDtypeStruct((m, D_MODEL), F32),
        grid=(m // TM_FFN,),
        in_specs=in_specs,
        out_specs=row,
        compiler_params=_params(1),
        name="ffn_half",
    )(*args)


def _rope_rows(lane_shape):
    lane = _iota(lane_shape, 1)
    return (lane % HEAD_DIM) < (HEAD_DIM // 2), lane < HEAD_DIM


def _rope_lanes(x, cos, sin, lo32):
    sh = jnp.where(lo32, pltpu.roll(x, LANES - HEAD_DIM // 2, 1), pltpu.roll(x, HEAD_DIM // 2, 1))
    return x * cos + sh * sin


def _proj_queries(u, w_ref, cos, sin, lo32, lo64, qa16_ref, qb16_ref, qb32_ref, qcz16_ref):
    p = _nn(u, w_ref[:, _O_QA:_O_QA + A_W])
    for k in range(A_W // LANES):
        qa16_ref[:, k * LANES:(k + 1) * LANES] = (
            _rope_lanes(p[:, k * LANES:(k + 1) * LANES], cos, sin, lo32) * QK_SCALE).astype(BF16)
    p = _nn(u, w_ref[:, _O_QB:_O_QB + B_W])
    for k in range(B_W // LANES):
        r = _rope_lanes(p[:, k * LANES:(k + 1) * LANES], cos, sin, lo32) * QK_SCALE
        qb32_ref[:, k * LANES:(k + 1) * LANES] = r
        qb16_ref[:, k * LANES:(k + 1) * LANES] = r.astype(BF16)
    p = _nn(u, w_ref[:, _O_QC:_O_QC + C_W])
    for k in range(C_W // LANES):
        r = _rope_lanes(p[:, k * LANES:(k + 1) * LANES], cos, sin, lo32) * QK_SCALE
        even = jnp.where(lo64, r, 0.0)
        odd = jnp.where(lo64, pltpu.roll(r, HEAD_DIM, 1), 0.0)
        qcz16_ref[:, (2 * k) * LANES:(2 * k + 1) * LANES] = even.astype(BF16)
        qcz16_ref[:, (2 * k + 1) * LANES:(2 * k + 2) * LANES] = odd.astype(BF16)


def _proj_sample_body(h_ref, g_ref, w_ref, wgate_ref, cos_ref, sin_ref,
                      ka_ref, va_ref, bkv_ref, ckv_ref, win_ref,
                      qa16_ref, qb16_ref, qb32_ref, qcz16_ref, gate_ref):
    tm = h_ref.shape[0]
    u = (_rms(h_ref[...]) * g_ref[...]).astype(BF16)
    cos = cos_ref[...]
    sin = sin_ref[...]
    lo32, lo64 = _rope_rows((tm, LANES))
    _proj_queries(u, w_ref, cos, sin, lo32, lo64, qa16_ref, qb16_ref, qb32_ref, qcz16_ref)
    p = _nn(u, w_ref[:, _O_KA:_O_KA + A_W])
    for k in range(A_W // LANES):
        ka_ref[:, k * LANES:(k + 1) * LANES] = _rope_lanes(p[:, k * LANES:(k + 1) * LANES], cos, sin, lo32)
    va_ref[...] = _nn(u, w_ref[:, _O_VA:_O_VA + A_W])
    p = _nn(u, w_ref[:, _O_KB:_O_KB + B_W])
    for k in range(B_W // LANES):
        bkv_ref[:, k * LANES:(k + 1) * LANES] = _rope_lanes(p[:, k * LANES:(k + 1) * LANES], cos, sin, lo32)
    bkv_ref[:, B_W:2 * B_W] = _nn(u, w_ref[:, _O_VB:_O_VB + B_W])
    p = _nn(u, w_ref[:, _O_KVC:_O_KVC + KVC_W])
    for k in range(KVC_W // LANES):
        x = p[:, k * LANES:(k + 1) * LANES]
        r = jnp.where(lo64, _rope_lanes(x, cos, sin, lo32), x)
        if k < 2:
            ckv_ref[:, k * LANES:(k + 1) * LANES] = r
        else:
            win_ref[...] = r
    gate_ref[...] = jax.nn.sigmoid(_nn(u, wgate_ref[...]))


def _proj_sample(h, g_mix, w_main, w_gate, cos, sin):
    m = h.shape[0]
    tm = TM_PROJ
    row = lambda w: pl.BlockSpec((tm, w), lambda i: (i, 0))
    full = lambda shape: pl.BlockSpec(shape, lambda i: (0, 0))
    outs = ((A_W, F32), (A_W, F32), (2 * B_W, F32), (4 * HEAD_DIM, F32), (2 * HEAD_DIM, F32),
            (A_W, BF16), (B_W, BF16), (B_W, F32), (2 * C_W, BF16), (LANES, F32))
    return pl.pallas_call(
        _proj_sample_body,
        out_shape=[jax.ShapeDtypeStruct((m, w), dt) for w, dt in outs],
        grid=(m // tm,),
        in_specs=[row(D_MODEL), full((1, D_MODEL)), full((D_MODEL, MAIN_W)), full((D_MODEL, LANES)),
                  full((tm, LANES)), full((tm, LANES))],
        out_specs=[row(w) for w, _ in outs],
        compiler_params=_params(1),
        name="in_proj_rope_sample",
    )(h, g_mix.reshape(1, D_MODEL), w_main, w_gate, cos, sin)


_KT_ROWS = A_W + 2 * B_W + KVC_W


def _proj_prompt_body(h_ref, g_ref, w_ref, wt_ref, wgate_ref, cos_ref, sin_ref, cos_t_ref, sin_t_ref,
                      kat_ref, va_ref, bkvt_ref, ckvt_ref, wint_ref,
                      qa16_ref, kat16_ref, va16_ref, qb16_ref, qb32_ref, bkvt16_ref,
                      qcz16_ref, selt16_ref, wint16_ref, gate_ref):
    tm = h_ref.shape[0]
    half = HEAD_DIM // 2
    u = (_rms(h_ref[...]) * g_ref[...]).astype(BF16)
    lo32, lo64 = _rope_rows((tm, LANES))
    _proj_queries(u, w_ref, cos_ref[...], sin_ref[...], lo32, lo64, qa16_ref, qb16_ref, qb32_ref, qcz16_ref)

    p = _nn(u, w_ref[:, _O_VA:_O_VA + A_W])
    va16_ref[...] = p.astype(BF16)
    for h in range(HEADS_A):
        va_ref[pl.ds(h, tm, stride=HEADS_A), :] = p[:, h * LANES:(h + 1) * LANES]

    cos_t = cos_t_ref[...]
    sin_t = sin_t_ref[...]

    def rope_t(x):
        x1, x2 = x[0:half], x[half:HEAD_DIM]
        return jnp.concatenate([x1 * cos_t - x2 * sin_t, x2 * cos_t + x1 * sin_t], axis=0)

    def store_t(f32_ref, b16_ref, row0, val):
        rows = val.shape[0]
        if f32_ref is not None:
            f32_ref[row0:row0 + rows, :] = val
        if b16_ref is not None:
            for t in range(tm // TQ):
                b16_ref[t, row0:row0 + rows, :] = val[:, t * TQ:(t + 1) * TQ].astype(BF16)

    pt = _nt(wt_ref[0:A_W, :], u)
    for g in range(A_W // HEAD_DIM):
        store_t(kat_ref, kat16_ref, g * HEAD_DIM, rope_t(pt[g * HEAD_DIM:(g + 1) * HEAD_DIM]))
    pt = _nt(wt_ref[A_W:A_W + 2 * B_W, :], u)
    for g in range(B_W // HEAD_DIM):
        store_t(bkvt_ref, bkvt16_ref, g * HEAD_DIM, rope_t(pt[g * HEAD_DIM:(g + 1) * HEAD_DIM]))
    store_t(bkvt_ref, bkvt16_ref, B_W, pt[B_W:2 * B_W])
    pt = _nt(wt_ref[A_W + 2 * B_W:_KT_ROWS, :], u)
    for g in range(KVC_W // HEAD_DIM):
        x = pt[g * HEAD_DIM:(g + 1) * HEAD_DIM]
        if g % 2 == 0:
            x = rope_t(x)
        if g < 2:
            store_t(ckvt_ref, None, g * HEAD_DIM, x)
        elif g < 4:
            store_t(ckvt_ref, None, g * HEAD_DIM, x)
            store_t(None, selt16_ref, (g - 2) * HEAD_DIM, x)
        else:
            store_t(wint_ref, wint16_ref, (g - 4) * HEAD_DIM, x)
    gate_ref[...] = jax.nn.sigmoid(_nn(u, wgate_ref[...]))


def _proj_prompt(h, g_mix, w_main, w_t, w_gate, tabs, n, s):
    cos, sin, cos_t, sin_t = tabs
    m = n * s
    tm = TM_PROJ
    per_seq = s // tm
    nt = tm // TQ
    row = lambda w: pl.BlockSpec((tm, w), lambda i: (i, 0))
    full = lambda shape: pl.BlockSpec(shape, lambda i: (0,) * len(shape))
    tab = pl.BlockSpec((tm, LANES), lambda i: (i % per_seq, 0))
    tab_t = pl.BlockSpec((HEAD_DIM // 2, tm), lambda i: (0, i % per_seq))
    feat = lambda w: pl.BlockSpec((None, w, tm), lambda i: (i // per_seq, 0, i % per_seq))
    tiles = lambda w: pl.BlockSpec((None, nt, w, TQ), lambda i: (i // per_seq, i % per_seq, 0, 0))
    sds = jax.ShapeDtypeStruct
    out_shape = [sds((n, A_W, s), F32), sds((m * HEADS_A, LANES), F32), sds((n, 2 * B_W, s), F32),
                 sds((n, 4 * HEAD_DIM, s), F32), sds((n, 2 * HEAD_DIM, s), F32),
                 sds((m, A_W), BF16), sds((n, s // TQ, A_W, TQ), BF16), sds((m, A_W), BF16),
                 sds((m, B_W), BF16), sds((m, B_W), F32), sds((n, s // TQ, 2 * B_W, TQ), BF16),
                 sds((m, 2 * C_W), BF16), sds((n, s // TQ, 2 * HEAD_DIM, TQ), BF16),
                 sds((n, s // TQ, 2 * HEAD_DIM, TQ), BF16), sds((m, LANES), F32)]
    out_specs = [feat(A_W), pl.BlockSpec((tm * HEADS_A, LANES), lambda i: (i, 0)), feat(2 * B_W),
                 feat(4 * HEAD_DIM), feat(2 * HEAD_DIM),
                 row(A_W), tiles(A_W), row(A_W), row(B_W), row(B_W), tiles(2 * B_W),
                 row(2 * C_W), tiles(2 * HEAD_DIM), tiles(2 * HEAD_DIM), row(LANES)]
    return pl.pallas_call(
        _proj_prompt_body,
        out_shape=out_shape,
        grid=(m // tm,),
        in_specs=[row(D_MODEL), full((1, D_MODEL)), full((D_MODEL, MAIN_W)), full((_KT_ROWS, D_MODEL)),
                  full((D_MODEL, LANES)), tab, tab, tab_t, tab_t],
        out_specs=out_specs,
        compiler_params=_params(1),
        name="in_proj_rope_prompt",
    )(h, g_mix.reshape(1, D_MODEL), w_main, w_t, w_gate, cos, sin, cos_t, sin_t)


def _outproj_body(h_ref, oa_ref, ob_ref, oc_ref, w_ref, o_ref):
    y = h_ref[...] + _nn(oa_ref[...], w_ref[0:A_W, :])
    y = y + _nn(ob_ref[...], w_ref[A_W:A_W + B_W, :])
    y = y + _nn(oc_ref[...], w_ref[A_W + B_W:, :])
    o_ref[...] = y


def _outproj(h, oa, ob, oc, w_out):
    m = h.shape[0]
    tm = TM_PROJ
    row = lambda w: pl.BlockSpec((tm, w), lambda i: (i, 0))
    return pl.pallas_call(
        _outproj_body,
        out_shape=jax.ShapeDtypeStruct((m, D_MODEL), F32),
        grid=(m // tm,),
        in_specs=[row(D_MODEL), row(A_W), row(B_W), row(C_W),
                  pl.BlockSpec((D_MODEL, D_MODEL), lambda i: (0, 0))],
        out_specs=row(D_MODEL),
        compiler_params=_params(1),
        name="out_proj",
    )(h, oa, ob, oc, w_out)


def _softmax_start(s):
    m = jnp.max(s, axis=-1, keepdims=True)
    p = jnp.exp(s - m)
    return m, jnp.sum(p, axis=-1, keepdims=True), p


def _softmax_step(s, m, l):
    m_new = jnp.maximum(m, jnp.max(s, axis=-1, keepdims=True))
    alpha = jnp.exp(m - m_new)
    p = jnp.exp(s - m_new)
    return m_new, alpha, alpha * l + jnp.sum(p, axis=-1, keepdims=True), p


def _rank_lower(x, n, width_iota):
    rank = jnp.zeros(x.shape, F32)
    for bp in range(n):
        col = x[:, bp:bp + 1]
        before = (col > x) | ((col == x) & (bp < width_iota))
        rank = rank + jnp.where(before, 1.0, 0.0)
    return rank


def _head_rms_scale(o, gh, lam_init):
    return _rms(o) * gh * (1.0 - lam_init)


def _attn_a_prompt_body(q_ref, kt_ref, v_ref, lam_ref, gh_ref, o_ref, *, lam_init):
    i = pl.program_id(1)
    tq = q_ref.shape[0]
    lam = _lam_value(lam_ref, lam_init)
    gh = gh_ref[...]
    lane = _iota((tq, LANES), 1)
    causal = _iota((tq, tq), 1) <= _iota((tq, tq), 0)
    for h in range(HEADS_A):
        sl = slice(h * LANES, (h + 1) * LANES)
        q2 = q_ref[:, sl]
        outs = []
        for c in range(2):
            in_c = (lane >= c * HEAD_DIM) & (lane < (c + 1) * HEAD_DIM)
            qm = jnp.where(in_c, q2, jnp.zeros_like(q2))
            vd = v_ref[pl.ds(pl.multiple_of(i * tq, tq), tq), sl]
            s = jnp.where(causal, _nn(qm, kt_ref[i, sl, :]), NEG_INF)
            m, l, p = _softmax_start(s)
            acc = _nn(p.astype(BF16), vd)

            def body(j, carry, qm=qm, sl=sl):
                m, l, acc = carry
                vj = v_ref[pl.ds(pl.multiple_of(j * tq, tq), tq), sl]
                m, alpha, l, p = _softmax_step(_nn(qm, kt_ref[j, sl, :]), m, l)
                return m, l, alpha * acc + _nn(p.astype(BF16), vj)

            m, l, acc = lax.fori_loop(0, i, body, (m, l, acc))
            outs.append(acc / l)
        o = outs[0] - lam * outs[1]
        o_ref[:, sl] = _head_rms_scale(o, gh, lam_init).astype(BF16)


def _attn_a_prompt(q16, kt16, v16, lam_p, g_head, n, s, lidx):
    nq = s // TQ
    qspec = pl.BlockSpec((TQ, A_W), lambda b, i: (b * nq + i, 0))
    return pl.pallas_call(
        functools.partial(_attn_a_prompt_body, lam_init=_lam_init(lidx)),
        out_shape=jax.ShapeDtypeStruct((n * s, A_W), BF16),
        grid=(n, nq),
        in_specs=[qspec,
                  pl.BlockSpec((None, nq, A_W, TQ), lambda b, i: (b, 0, 0, 0)),
                  pl.BlockSpec((s, A_W), lambda b, i: (b, 0)),
                  pl.BlockSpec((4, HEAD_DIM), lambda b, i: (0, 0)),
                  pl.BlockSpec((1, 2 * HEAD_DIM), lambda b, i: (0, 0))],
        out_specs=qspec,
        compiler_params=_params(2),
        name="diff_attn_prompt",
    )(q16, kt16, v16, lam_p, g_head.reshape(1, 2 * HEAD_DIM))


def _block_means_t(blocks):
    feats = blocks[0].shape[0]
    lane = _iota((feats, LANES), 1)
    out = jnp.zeros((feats, LANES), F32)
    for b, blk in enumerate(blocks):
        out = jnp.where(lane == b, jnp.sum(blk, axis=1, keepdims=True) * (1.0 / MOBA_BLOCK), out)
    return out


def _attn_b_prompt_body(q_ref, q32_ref, k32t_ref, kvt_ref, o_ref, kmean_ref):
    i = pl.program_id(1)
    tq = q_ref.shape[0]
    nb = k32t_ref.shape[1] // MOBA_BLOCK

    @pl.when(i == 0)
    def _():
        kmean_ref[...] = _block_means_t([k32t_ref[:, b * MOBA_BLOCK:(b + 1) * MOBA_BLOCK] for b in range(nb)])

    lane = _iota((tq, LANES), 1)
    causal = _iota((tq, tq), 1) <= _iota((tq, tq), 0)
    past = lane < i
    for pair in range(HEADS_B // 2):
        sl = slice(pair * LANES, (pair + 1) * LANES)
        vsl = slice(B_W + pair * LANES, B_W + (pair + 1) * LANES)
        q2 = q_ref[:, sl]
        q2f = q32_ref[:, sl]
        km = kmean_ref[sl, :]
        outs = []
        for e in range(2):
            in_h = (lane >= e * HEAD_DIM) & (lane < (e + 1) * HEAD_DIM)
            qm = jnp.where(in_h, q2, jnp.zeros_like(q2))
            gate = _nn_precise(jnp.where(in_h, q2f, 0.0), km)
            gate = jnp.where(past, gate, NEG_INF)
            chosen = jnp.where((_rank_lower(gate, nb, lane) < MOBA_TOPK) & past, 1.0, 0.0)

            s = jnp.where(causal, _nn(qm, kvt_ref[i, sl, :]), NEG_INF)
            m, l, p = _softmax_start(s)
            acc = _nt(p.astype(BF16), kvt_ref[i, vsl, :])

            def body(j, carry, qm=qm, chosen=chosen, sl=sl, vsl=vsl):
                m, l, acc = carry
                use = jnp.max(jnp.where(lane == j, chosen, 0.0), axis=1, keepdims=True)
                s = jnp.where(use > 0.5, _nn(qm, kvt_ref[j, sl, :]), NEG_INF)
                m, alpha, l, p = _softmax_step(s, m, l)
                return m, l, alpha * acc + _nt(p.astype(BF16), kvt_ref[j, vsl, :])

            m, l, acc = lax.fori_loop(0, i, body, (m, l, acc))
            outs.append(acc / l)
        o_ref[:, sl] = jnp.where(lane < HEAD_DIM, outs[0], outs[1]).astype(BF16)


def _attn_b_prompt(q16, q32, bkvt32, bkvt16, n, s):
    nq = s // TQ
    assert TQ == MOBA_BLOCK and s // MOBA_BLOCK <= LANES
    qspec = pl.BlockSpec((TQ, B_W), lambda b, i: (b * nq + i, 0))
    return pl.pallas_call(
        _attn_b_prompt_body,
        out_shape=jax.ShapeDtypeStruct((n * s, B_W), BF16),
        grid=(n, nq),
        in_specs=[qspec, qspec,
                  pl.BlockSpec((None, B_W, s), lambda b, i: (b, 0, 0)),
                  pl.BlockSpec((None, nq, 2 * B_W, TQ), lambda b, i: (b, 0, 0, 0))],
        out_specs=qspec,
        scratch_shapes=[pltpu.VMEM((B_W, LANES), F32)],
        compiler_params=_params(2),
        name="moba_attn_prompt",
    )(q16, q32, bkvt32, bkvt16)


_CMP_ROWS = 128


def _compress_core(xs_ref, pe_ref, w1_ref, w2_ref, o_ref):
    half = CMP_LEN // 2
    n_chunk = xs_ref.shape[0] // CMP_STRIDE
    acc = [None, None]
    for r in range(CMP_LEN):
        x = (xs_ref[pl.ds(r % half, n_chunk, stride=CMP_STRIDE), :] + pe_ref[r:r + 1, :]).astype(BF16)
        t = _nn(x, w1_ref[r])
        acc[r // half] = t if acc[r // half] is None else acc[r // half] + t
    pre = acc[0] + pltpu.roll(acc[1], n_chunk - 1, 0)
    hid = jax.nn.gelu(pre).astype(BF16)
    o_ref[...] = _nn(hid, w2_ref[...]).astype(BF16)


def _compress_prompt_body(xt_ref, pe_ref, w1_ref, w2_ref, o_ref, xs_ref):
    for j in range(xt_ref.shape[1] // LANES):
        xs_ref[j * LANES:(j + 1) * LANES, :] = xt_ref[:, j * LANES:(j + 1) * LANES].T
    _compress_core(xs_ref, pe_ref, w1_ref, w2_ref, o_ref)


def _compress_pages_body(pt_ref, pe_ref, w1_ref, w2_ref, *refs):
    pages, o_ref, xs_ref = refs[:-2], refs[-2], refs[-1]
    for j, pg in enumerate(pages):
        xs_ref[j * PAGE_SIZE:(j + 1) * PAGE_SIZE, :] = pg[...].T
    _compress_core(xs_ref, pe_ref, w1_ref, w2_ref, o_ref)


def _compress_weights(w_cmp1, w_cmp2, cmp_pos):
    w1 = w_cmp1.reshape(2, CMP_LEN, HEAD_DIM, CMP_HIDDEN)
    z1 = jnp.zeros((CMP_LEN, HEAD_DIM, CMP_HIDDEN), F32)
    top = jnp.concatenate([w1[0], z1], axis=2)
    bot = jnp.concatenate([z1, w1[1]], axis=2)
    w1c = jnp.concatenate([top, bot], axis=1).astype(BF16)
    z2 = jnp.zeros((CMP_HIDDEN, HEAD_DIM), F32)
    w2c = jnp.concatenate([jnp.concatenate([w_cmp2[0], z2], axis=1),
                           jnp.concatenate([z2, w_cmp2[1]], axis=1)], axis=0).astype(BF16)
    pe = jnp.concatenate([cmp_pos[0], cmp_pos[1]], axis=1)
    return pe, w1c, w2c


def _const_specs(index):
    return [pl.BlockSpec((CMP_LEN, LANES), index(2)),
            pl.BlockSpec((CMP_LEN, LANES, 2 * CMP_HIDDEN), index(3)),
            pl.BlockSpec((2 * CMP_HIDDEN, LANES), index(2))]


def _compress_prompt(ckvt32, cw, n, s):
    assert s // CMP_STRIDE == _CMP_ROWS
    zero = lambda nd: (lambda b: (0,) * nd)
    return pl.pallas_call(
        _compress_prompt_body,
        out_shape=jax.ShapeDtypeStruct((n * _CMP_ROWS, LANES), BF16),
        grid=(n,),
        in_specs=[pl.BlockSpec((None, LANES, s), lambda b: (b, 0, 0))] + _const_specs(zero),
        out_specs=pl.BlockSpec((_CMP_ROWS, LANES), lambda b: (b, 0)),
        scratch_shapes=[pltpu.VMEM((s, LANES), F32)],
        compiler_params=_params(1),
        name="nsa_compress_prompt",
    )(ckvt32, *cw)


def _page_index(l, g, group, j, row_block, b, pt):
    return (l, pt[b * group + g, j], row_block, 0)


def _page_specs(l, n_pages, rows, row_block=0, group=1):
    return [pl.BlockSpec((None, None, rows, PAGE_SIZE), functools.partial(_page_index, l, g, group, j, row_block))
            for g in range(group) for j in range(n_pages)]


def _compress_pages(cache_ct, page_table, cw, l):
    n, n_pages = page_table.shape
    group = CMP_GROUP
    assert n_pages * PAGE_SIZE // CMP_STRIDE == _CMP_ROWS and n % group == 0
    zero = lambda nd: (lambda b, pt: (0,) * nd)
    grid_spec = pltpu.PrefetchScalarGridSpec(
        num_scalar_prefetch=1, grid=(n // group,),
        in_specs=_const_specs(zero) + _page_specs(l, n_pages, LANES, 0, group),
        out_specs=pl.BlockSpec((group * _CMP_ROWS, LANES), lambda b, pt: (b, 0)),
        scratch_shapes=[pltpu.VMEM((group * n_pages * PAGE_SIZE, LANES), F32)])
    return pl.pallas_call(
        _compress_pages_body,
        out_shape=jax.ShapeDtypeStruct((n * _CMP_ROWS, LANES), BF16),
        grid_spec=grid_spec,
        compiler_params=_params(1),
        name="nsa_compress_pages",
    )(page_table, *cw, *([cache_ct] * (n_pages * group)))


def _overlap_matrix(t_len):
    n_cmp = (t_len - CMP_LEN) // CMP_STRIDE + 1
    nsb = -(-t_len // SEL_BLOCK)
    starts = np.arange(n_cmp) * CMP_STRIDE
    sb = np.arange(nsb) * SEL_BLOCK
    ov = np.clip(np.minimum(starts[:, None] + CMP_LEN, sb[None, :] + SEL_BLOCK)
                 - np.maximum(starts[:, None], sb[None, :]), 0, None) / CMP_STRIDE
    out = np.zeros((_CMP_ROWS, LANES), np.float32)
    out[:n_cmp, :nsb] = ov
    return jnp.asarray(out, BF16), nsb


def _expand_matrix(s):
    nt = s // TQ
    e = np.zeros((nt, LANES, TQ), np.float32)
    for j in range(nt):
        for k in range(TQ):
            e[j, (j * TQ + k) // SEL_BLOCK, k] = 1.0
    return jnp.asarray(e, BF16)


def _nsa_flags(p_sum, ovl, own, lane, nsb):
    imp = None
    for part in _split3(p_sum):
        t = _nn(part, ovl)
        imp = t if imp is None else imp + t
    forced = (lane == 0) | (lane == own) | (lane == own - 1)
    imp = jnp.where(lane > own, NEG_INF, jnp.where(forced, jnp.inf, imp))
    rank = _rank_lower(imp, nsb, lane)
    return jnp.where((rank < SEL_TOPK) & (lane <= own), 1.0, 0.0)


def _masked_probs(s, mask):
    s = jnp.where(mask, s, NEG_INF)
    m = jnp.max(s, axis=-1, keepdims=True)
    m = jnp.where(m > NEG_INF, m, 0.0)
    e = jnp.exp(s - m)
    d = jnp.sum(e, axis=-1, keepdims=True)
    return e / jnp.where(d > 0, d, 1.0)


def _attn_c_prompt_body(qz_ref, gate_ref, selt_ref, wint_ref, kvc_ref, ovl_ref, exp_ref, o_ref, *, nsb):
    i = pl.program_id(1)
    tq = gate_ref.shape[0]
    nh = HEADS_C
    q4 = jnp.concatenate([qz_ref[:, h * LANES:(h + 1) * LANES] for h in range(nh)], axis=0)
    lane = _iota((tq, LANES), 1)
    pos = i * tq + _iota((tq, LANES), 0)
    rr = _iota((tq, tq), 0)
    cc = _iota((tq, tq), 1)
    causal = cc <= rr

    kvc = kvc_ref[...]
    cmp_ok = (CMP_STRIDE * lane + (CMP_LEN - 1)) <= pos
    s = _nt(q4, kvc).reshape(nh, tq, LANES)
    p = _masked_probs(s, cmp_ok[None])
    o_cmp = _nn(p.reshape(nh * tq, LANES).astype(BF16), kvc)
    p_sum = p[0] + p[1] + p[2] + p[3]

    own = pos // SEL_BLOCK
    flag = _nsa_flags(p_sum, ovl_ref[...], own, lane, nsb).astype(BF16)

    kd = selt_ref[i]
    ok = (_nn(flag, exp_ref[i]) > 0.5) & causal
    s = jnp.where(ok[None], _nn(q4, kd).reshape(nh, tq, tq), NEG_INF)
    m, l, p = _softmax_start(s)
    acc = _nt(p.reshape(nh * tq, tq).astype(BF16), kd).reshape(nh, tq, LANES)

    def body(j, carry):
        m, l, acc = carry
        kj = selt_ref[j]
        ok = _nn(flag, exp_ref[j]) > 0.5
        s = jnp.where(ok[None], _nn(q4, kj).reshape(nh, tq, tq), NEG_INF)
        m, alpha, l, p = _softmax_step(s, m, l)
        pv = _nt(p.reshape(nh * tq, tq).astype(BF16), kj).reshape(nh, tq, LANES)
        return m, l, alpha * acc + pv

    m, l, acc = lax.fori_loop(0, i, body, (m, l, acc))
    o_sel = acc / l

    assert WINDOW == 2 * tq
    w2 = wint_ref[jnp.maximum(i - 2, 0)]
    w1 = wint_ref[jnp.maximum(i - 1, 0)]
    w0 = wint_ref[i]
    s2 = jnp.where(((cc >= rr) & (i >= 2))[None], _nn(q4, w2).reshape(nh, tq, tq), NEG_INF)
    s1 = jnp.where(i >= 1, _nn(q4, w1).reshape(nh, tq, tq), NEG_INF)
    s0 = jnp.where(causal[None], _nn(q4, w0).reshape(nh, tq, tq), NEG_INF)
    m = jnp.maximum(jnp.maximum(jnp.max(s2, -1, keepdims=True), jnp.max(s1, -1, keepdims=True)),
                    jnp.max(s0, -1, keepdims=True))
    e2, e1, e0 = jnp.exp(s2 - m), jnp.exp(s1 - m), jnp.exp(s0 - m)
    d = jnp.sum(e2, -1, keepdims=True) + jnp.sum(e1, -1, keepdims=True) + jnp.sum(e0, -1, keepdims=True)
    flat = lambda e: e.reshape(nh * tq, tq).astype(BF16)
    o_win = (_nt(flat(e2), w2) + _nt(flat(e1), w1) + _nt(flat(e0), w0)).reshape(nh, tq, LANES) / d

    o_cmp = o_cmp.reshape(nh, tq, LANES)
    g = gate_ref[...]
    heads = []
    for h in range(nh):
        heads.append(g[:, 3 * h:3 * h + 1] * o_cmp[h] + g[:, 3 * h + 1:3 * h + 2] * o_sel[h]
                     + g[:, 3 * h + 2:3 * h + 3] * o_win[h])
    for pair in range(nh // 2):
        both = jnp.where(lane < HEAD_DIM, pltpu.roll(heads[2 * pair], HEAD_DIM, 1), heads[2 * pair + 1])
        o_ref[:, pair * LANES:(pair + 1) * LANES] = both.astype(BF16)


def _attn_c_prompt(qz16, gates, selt16, wint16, kvcmp16, n, s):
    nq = s // TQ
    ovl, nsb = _overlap_matrix(s)
    expand = _expand_matrix(s)
    qrow = lambda w: pl.BlockSpec((TQ, w), lambda b, i: (b * nq + i, 0))
    tiles = pl.BlockSpec((None, nq, 2 * HEAD_DIM, TQ), lambda b, i: (b, 0, 0, 0))
    return pl.pallas_call(
        functools.partial(_attn_c_prompt_body, nsb=nsb),
        out_shape=jax.ShapeDtypeStruct((n * s, C_W), BF16),
        grid=(n, nq),
        in_specs=[qrow(2 * C_W), qrow(LANES), tiles, tiles,
                  pl.BlockSpec((_CMP_ROWS, LANES), lambda b, i: (b, 0)),
                  pl.BlockSpec((_CMP_ROWS, LANES), lambda b, i: (0, 0)),
                  pl.BlockSpec((nq, LANES, TQ), lambda b, i: (0, 0, 0))],
        out_specs=qrow(C_W),
        compiler_params=_params(2),
        name="nsa_attn_prompt",
    )(qz16, gates, selt16, wint16, kvcmp16, ovl, expand)


def _pad_page(x):
    rows, w = x.shape
    return jnp.concatenate([x, jnp.zeros((PAGE_SIZE - rows, w), x.dtype)], axis=0)


def _new_page_mask(n_rows, nq):
    r = _iota((n_rows, PAGE_SIZE), 0) % nq
    t = _iota((n_rows, PAGE_SIZE), 1)
    return t <= r


def _dec_a_body(pt_ref, q_ref, kn_ref, vn_ref, lam_ref, gh_ref, *refs, lam_init):
    n_pages = (len(refs) - 1) // 2
    kp, vp, o_ref = refs[:n_pages], refs[n_pages:2 * n_pages], refs[-1]
    nq = q_ref.shape[0]
    rows = 2 * HEADS_A * nq
    per_head = 2 * nq
    lam = _lam_value(lam_ref, lam_init)
    qt = jnp.concatenate([q_ref[...]] * (2 * HEADS_A), axis=0)
    diag = (_iota((rows, A_W), 0) // nq) == (_iota((rows, A_W), 1) // HEAD_DIM)
    qbd = jnp.where(diag, qt, jnp.zeros_like(qt))
    kn = _pad_page(kn_ref[...]).astype(BF16)
    vn = _pad_page(vn_ref[...]).astype(BF16)
    s_new = jnp.where(_new_page_mask(rows, nq), _nt(qbd, kn), NEG_INF)
    scores = [_nn(qbd, kp[j][...].astype(BF16)) for j in range(n_pages)]
    m = jnp.max(s_new, axis=1, keepdims=True)
    for s in scores:
        m = jnp.maximum(m, jnp.max(s, axis=1, keepdims=True))
    e = jnp.exp(s_new - m)
    d = jnp.sum(e, axis=1, keepdims=True)
    e = e.astype(BF16)
    accs = [_nn(e[h * per_head:(h + 1) * per_head], vn[:, h * LANES:(h + 1) * LANES]) for h in range(HEADS_A)]
    for j in range(n_pages):
        e = jnp.exp(scores[j] - m)
        d = d + jnp.sum(e, axis=1, keepdims=True)
        e = e.astype(BF16)
        for h in range(HEADS_A):
            vh = vp[j][pl.ds(h, PAGE_SIZE, stride=HEADS_A), :].astype(BF16)
            accs[h] = accs[h] + _nn(e[h * per_head:(h + 1) * per_head], vh)
    gh = gh_ref[...]
    for h in range(HEADS_A):
        on = accs[h] / d[h * per_head:(h + 1) * per_head]
        o = on[0:nq] - lam * on[nq:2 * nq]
        o_ref[:, h * LANES:(h + 1) * LANES] = _head_rms_scale(o, gh, lam_init).astype(BF16)


def _dec_a(q16, kn32, vn32, lam_p, g_head, cache_kt, cache_v4, page_table, l, nq):
    n, n_pages = page_table.shape
    row = lambda w: pl.BlockSpec((nq, w), lambda b, pt: (b, 0))
    grid_spec = pltpu.PrefetchScalarGridSpec(
        num_scalar_prefetch=1, grid=(n,),
        in_specs=[row(A_W), row(A_W), row(A_W),
                  pl.BlockSpec((4, HEAD_DIM), lambda b, pt: (0, 0)),
                  pl.BlockSpec((1, 2 * HEAD_DIM), lambda b, pt: (0, 0))]
                 + _page_specs(l, n_pages, A_W) + _page_specs(l, n_pages, A_W),
        out_specs=row(A_W))
    return pl.pallas_call(
        functools.partial(_dec_a_body, lam_init=_lam_init(l)),
        out_shape=jax.ShapeDtypeStruct((n * nq, A_W), BF16),
        grid_spec=grid_spec,
        compiler_params=_params(1),
        name="diff_attn_decode",
    )(page_table, q16, kn32, vn32, lam_p, g_head.reshape(1, 2 * HEAD_DIM),
      *([cache_kt] * n_pages), *([cache_v4] * n_pages))


def _dec_b_body(pt_ref, q_ref, q32_ref, kvn_ref, *refs):
    pages, o_ref = refs[:-1], refs[-1]
    n_pages = len(pages)
    nq = q_ref.shape[0]
    rows = HEADS_B * nq
    pages_per_blk = MOBA_BLOCK // PAGE_SIZE
    nb = n_pages // pages_per_blk
    diag = (_iota((rows, B_W), 0) // nq) == (_iota((rows, B_W), 1) // HEAD_DIM)
    qt = jnp.concatenate([q_ref[...]] * HEADS_B, axis=0)
    qbd = jnp.where(diag, qt, jnp.zeros_like(qt))
    qbd32 = jnp.where(diag, jnp.concatenate([q32_ref[...]] * HEADS_B, axis=0), 0.0)

    blocks = []
    for b in range(nb):
        blocks.append(jnp.concatenate([pages[j][0:B_W, :] for j in range(b * pages_per_blk, (b + 1) * pages_per_blk)],
                                      axis=1))
    kmean_t = _block_means_t(blocks)
    lane = _iota((rows, LANES), 1)
    gate = _nn_precise(qbd32, kmean_t)
    gate = jnp.where(lane < nb, gate, NEG_INF)
    chosen = jnp.where((_rank_lower(gate, nb, lane) < MOBA_TOPK) & (lane < nb), 1.0, 0.0)

    kvn = _pad_page(kvn_ref[...]).astype(BF16)
    s_new = jnp.where(_new_page_mask(rows, nq), _nt(qbd, kvn[:, 0:B_W]), NEG_INF)
    scores = []
    for j in range(n_pages):
        b = j // pages_per_blk
        s = _nn(qbd, pages[j][0:B_W, :].astype(BF16))
        scores.append(jnp.where(chosen[:, b:b + 1] > 0.5, s, NEG_INF))
    m = jnp.max(s_new, axis=1, keepdims=True)
    for s in scores:
        m = jnp.maximum(m, jnp.max(s, axis=1, keepdims=True))
    e = jnp.exp(s_new - m)
    d = jnp.sum(e, axis=1, keepdims=True)
    acc = _nn(e.astype(BF16), kvn[:, B_W:2 * B_W])
    for j in range(n_pages):
        e = jnp.exp(scores[j] - m)
        d = d + jnp.sum(e, axis=1, keepdims=True)
        acc = acc + _nt(e.astype(BF16), pages[j][B_W:2 * B_W, :].astype(BF16))
    on = acc / d
    head_of_lane = _iota((nq, B_W), 1) // HEAD_DIM
    o = jnp.zeros((nq, B_W), F32)
    for h in range(HEADS_B):
        o = o + jnp.where(head_of_lane == h, on[h * nq:(h + 1) * nq], 0.0)
    o_ref[...] = o.astype(BF16)


def _dec_b(q16, q32, kvn32, cache_kvt, page_table, l, nq):
    n, n_pages = page_table.shape
    row = lambda w: pl.BlockSpec((nq, w), lambda b, pt: (b, 0))
    grid_spec = pltpu.PrefetchScalarGridSpec(
        num_scalar_prefetch=1, grid=(n,),
        in_specs=[row(B_W), row(B_W), row(2 * B_W)] + _page_specs(l, n_pages, 2 * B_W),
        out_specs=row(B_W))
    return pl.pallas_call(
        _dec_b_body,
        out_shape=jax.ShapeDtypeStruct((n * nq, B_W), BF16),
        grid_spec=grid_spec,
        compiler_params=_params(1),
        name="moba_attn_decode",
    )(page_table, q16, q32, kvn32, *([cache_kvt] * n_pages))


def _dec_c_body(pt_ref, qz_ref, gate_ref, ckvn_ref, winn_ref, kvc_ref, ovl_ref, st_ref, *refs, nsb, q0):
    pages, o_ref = refs[:-1], refs[-1]
    n_pages = len(pages)
    nq = gate_ref.shape[0]
    nh = HEADS_C
    rows = nh * nq
    q4 = jnp.concatenate([qz_ref[:, h * LANES:(h + 1) * LANES] for h in range(nh)], axis=0)
    lane = _iota((nq, LANES), 1)
    pos = q0 + _iota((nq, LANES), 0)
    lane4 = _iota((rows, LANES), 1)
    qrow4 = _iota((rows, LANES), 0) % nq
    new_ok = _new_page_mask(rows, nq)

    kvc = kvc_ref[...]
    cmp_ok = (CMP_STRIDE * lane4 + (CMP_LEN - 1)) <= (q0 + qrow4)
    p = _masked_probs(_nt(q4, kvc), cmp_ok)
    o_cmp = _nn(p.astype(BF16), kvc)
    p_sum = p[0:nq]
    for h in range(1, nh):
        p_sum = p_sum + p[h * nq:(h + 1) * nq]

    own = pos // SEL_BLOCK
    flag = _nsa_flags(p_sum, ovl_ref[...], own, lane, nsb)
    flag4 = jnp.concatenate([flag] * nh, axis=0)
    blk_per_page = PAGE_SIZE // SEL_BLOCK
    assert blk_per_page == 2

    ckvn = _pad_page(ckvn_ref[...]).astype(BF16)
    kn = ckvn[:, LANES:2 * LANES]
    own_blk = n_pages * blk_per_page
    s_new = jnp.where(new_ok & (flag4[:, own_blk:own_blk + 1] > 0.5), _nt(q4, kn), NEG_INF)
    tiles = [pages[j][...].astype(BF16) for j in range(n_pages)]
    scores = []
    for j in range(n_pages):
        ok = jnp.where(lane4 < SEL_BLOCK, flag4[:, 2 * j:2 * j + 1], flag4[:, 2 * j + 1:2 * j + 2]) > 0.5
        scores.append(jnp.where(ok, _nn(q4, tiles[j]), NEG_INF))
    m = jnp.max(s_new, axis=1, keepdims=True)
    for s in scores:
        m = jnp.maximum(m, jnp.max(s, axis=1, keepdims=True))
    e = jnp.exp(s_new - m)
    d = jnp.sum(e, axis=1, keepdims=True)
    acc = _nn(e.astype(BF16), kn)
    for j in range(n_pages):
        e = jnp.exp(scores[j] - m)
        d = d + jnp.sum(e, axis=1, keepdims=True)
        acc = acc + _nt(e.astype(BF16), tiles[j])
    o_sel = acc / d

    wn = _pad_page(winn_ref[...]).astype(BF16)
    s_new = jnp.where(new_ok, _nt(q4, wn), NEG_INF)
    st = st_ref[...].astype(BF16)
    key = _iota((rows, WINDOW), 1)
    ok = key >= (_iota((rows, WINDOW), 0) % nq)
    s_old = jnp.where(ok, _nn(q4, st), NEG_INF)
    m = jnp.maximum(jnp.max(s_new, axis=1, keepdims=True), jnp.max(s_old, axis=1, keepdims=True))
    e_new = jnp.exp(s_new - m)
    e_old = jnp.exp(s_old - m)
    d = jnp.sum(e_new, axis=1, keepdims=True) + jnp.sum(e_old, axis=1, keepdims=True)
    o_win = (_nn(e_new.astype(BF16), wn) + _nt(e_old.astype(BF16), st)) / d

    g = gate_ref[...]
    heads = []
    for h in range(nh):
        r = slice(h * nq, (h + 1) * nq)
        heads.append(g[:, 3 * h:3 * h + 1] * o_cmp[r] + g[:, 3 * h + 1:3 * h + 2] * o_sel[r]
                     + g[:, 3 * h + 2:3 * h + 3] * o_win[r])
    for pair in range(nh // 2):
        both = jnp.where(lane < HEAD_DIM, pltpu.roll(heads[2 * pair], HEAD_DIM, 1), heads[2 * pair + 1])
        o_ref[:, pair * LANES:(pair + 1) * LANES] = both.astype(BF16)


def _dec_c(qz16, gates, ckvn32, winn32, kvcmp16, state_t, cache_ct, page_table, l, nq, q0):
    n, n_pages = page_table.shape
    assert state_t.shape[3] == WINDOW and q0 >= WINDOW
    ovl, nsb = _overlap_matrix(q0 + nq)
    row = lambda w: pl.BlockSpec((nq, w), lambda b, pt: (b, 0))
    grid_spec = pltpu.PrefetchScalarGridSpec(
        num_scalar_prefetch=1, grid=(n,),
        in_specs=[row(2 * C_W), row(LANES), row(4 * HEAD_DIM), row(2 * HEAD_DIM),
                  pl.BlockSpec((_CMP_ROWS, LANES), lambda b, pt: (b, 0)),
                  pl.BlockSpec((_CMP_ROWS, LANES), lambda b, pt: (0, 0)),
                  pl.BlockSpec((None, None, 2 * HEAD_DIM, WINDOW), lambda b, pt: (l, b, 0, 0))]
                 + _page_specs(l, n_pages, 2 * HEAD_DIM, row_block=1),
        out_specs=row(C_W))
    return pl.pallas_call(
        functools.partial(_dec_c_body, nsb=nsb, q0=q0),
        out_shape=jax.ShapeDtypeStruct((n * nq, C_W), BF16),
        grid_spec=grid_spec,
        compiler_params=_params(1),
        name="nsa_attn_decode",
    )(page_table, qz16, gates, ckvn32, winn32, kvcmp16, ovl, state_t, *([cache_ct] * n_pages))


def _rope_tables(n_pos):
    inv = ROPE_THETA ** (-jnp.arange(0, HEAD_DIM, 2, dtype=F32) / HEAD_DIM)
    ang = jnp.arange(n_pos, dtype=F32)[:, None] * inv[None, :]
    cos, sin = jnp.cos(ang), jnp.sin(ang)
    cos128 = jnp.concatenate([cos, cos, cos, cos], axis=-1)
    sin128 = jnp.concatenate([-sin, sin, -sin, sin], axis=-1)
    return cos128, sin128, cos.T, sin.T


def kernel(x_prompt, x_sample, cache_a_k, cache_a_v, cache_b_kv, cache_c_kv, state_c_win, page_table,
           w_in, w_out, g_mix, g_ffn, w_ffn_gate, w_ffn_up, w_ffn_down, diff_lambda, g_diff_head,
           w_cmp1, w_cmp2, cmp_pos, g_final):
    n_p, s_p, _ = x_prompt.shape
    n_s, s_s, _ = x_sample.shape
    n_pages = page_table.shape[1]
    past_len = n_pages * cache_a_k.shape[2]
    n_phys = cache_a_k.shape[1]
    assert cache_a_k.shape[2] == PAGE_SIZE and s_p % TM_PROJ == 0 and TM_PROJ % s_s == 0

    cos, sin, cos_t, sin_t = _rope_tables(past_len + s_s)
    tabs_p = (cos[:s_p], sin[:s_p], cos_t[:, :s_p], sin_t[:, :s_p])
    reps = TM_PROJ // s_s
    cos_s = jnp.tile(cos[past_len:past_len + s_s], (reps, 1))
    sin_s = jnp.tile(sin[past_len:past_len + s_s], (reps, 1))

    ckt = cache_a_k.transpose(0, 1, 3, 4, 5, 2).reshape(DEPTH, n_phys, A_W, PAGE_SIZE)
    cv4 = cache_a_v.reshape(DEPTH, n_phys, PAGE_SIZE * HEADS_A, 2 * HEAD_DIM)
    cbt = cache_b_kv.transpose(0, 1, 3, 4, 5, 2).reshape(DEPTH, n_phys, 2 * B_W, PAGE_SIZE)
    cct = cache_c_kv.transpose(0, 1, 3, 4, 2).reshape(DEPTH, n_phys, 4 * HEAD_DIM, PAGE_SIZE)
    stt = state_c_win.transpose(0, 1, 3, 4, 2).reshape(DEPTH, n_s, 2 * HEAD_DIM, WINDOW)

    hp = x_prompt.reshape(n_p * s_p, D_MODEL)
    hs = x_sample.reshape(n_s * s_s, D_MODEL)
    ent_p, ent_s = [], []
    for l in range(DEPTH):
        wg = w_ffn_gate[l].astype(BF16)
        wu = w_ffn_up[l].astype(BF16)
        wd = w_ffn_down[l].astype(BF16)
        w_main = w_in[l][:, :MAIN_W].astype(BF16)
        w_t = jnp.concatenate([w_in[l][:, _O_KA:_O_KA + A_W], w_in[l][:, _O_KB:_O_KB + 2 * B_W],
                               w_in[l][:, _O_KVC:_O_KVC + KVC_W]], axis=1).T.astype(BF16)
        w_gate = jnp.pad(w_in[l][:, MAIN_W:], ((0, 0), (0, LANES - GATE_W))).astype(BF16)
        wo = w_out[l].astype(BF16)
        cw = _compress_weights(w_cmp1[l], w_cmp2[l], cmp_pos[l])
        last = l == DEPTH - 1

        hp = _ffn(hp, g_ffn[l, 0], wg[0], wu[0], wd[0])
        (kat, va, bkvt, ckvt, wint, qa16, kat16, va16, qb16, qb32, bkvt16, qcz16, selt16, wint16, gates) = \
            _proj_prompt(hp, g_mix[l], w_main, w_t, w_gate, tabs_p, n_p, s_p)
        oa = _attn_a_prompt(qa16, kat16, va16, diff_lambda[l], g_diff_head[l], n_p, s_p, l)
        ob = _attn_b_prompt(qb16, qb32, bkvt, bkvt16, n_p, s_p)
        kvcmp = _compress_prompt(ckvt, cw, n_p, s_p)
        oc = _attn_c_prompt(qcz16, gates, selt16, wint16, kvcmp, n_p, s_p)
        hp = _outproj(hp, oa, ob, oc, wo)
        hp = _ffn(hp, g_ffn[l, 1], wg[1], wu[1], wd[1], g_final if last else None)
        win_keep = min(WINDOW, s_p)
        ent_p.append((kat.reshape(n_p, HEADS_A, 2, HEAD_DIM, s_p).transpose(0, 4, 1, 2, 3),
                      va.reshape(n_p, s_p, HEADS_A, 2 * HEAD_DIM),
                      bkvt.reshape(n_p, 2, HEADS_B, HEAD_DIM, s_p).transpose(0, 4, 1, 2, 3),
                      ckvt.reshape(n_p, 4, HEAD_DIM, s_p).transpose(0, 3, 1, 2),
                      wint[:, :, s_p - win_keep:].reshape(n_p, 2, HEAD_DIM, win_keep).transpose(0, 3, 1, 2)))

        hs = _ffn(hs, g_ffn[l, 0], wg[0], wu[0], wd[0])
        (ka, va, bkv, ckv, win, qa16, qb16, qb32, qcz16, gates) = _proj_sample(
            hs, g_mix[l], w_main, w_gate, cos_s, sin_s)
        oa = _dec_a(qa16, ka, va, diff_lambda[l], g_diff_head[l], ckt, cv4, page_table, l, s_s)
        ob = _dec_b(qb16, qb32, bkv, cbt, page_table, l, s_s)
        kvcmp = _compress_pages(cct, page_table, cw, l)
        oc = _dec_c(qcz16, gates, ckv, win, kvcmp, stt, cct, page_table, l, s_s, past_len)
        hs = _outproj(hs, oa, ob, oc, wo)
        hs = _ffn(hs, g_ffn[l, 1], wg[1], wu[1], wd[1], g_final if last else None)
        win_new = win.reshape(n_s, s_s, 2, HEAD_DIM)
        win_all = jnp.concatenate([state_c_win[l], win_new], axis=1)
        ent_s.append((ka.reshape(n_s, s_s, HEADS_A, 2, HEAD_DIM), va.reshape(n_s, s_s, HEADS_A, 2 * HEAD_DIM),
                      bkv.reshape(n_s, s_s, 2, HEADS_B, HEAD_DIM), ckv.reshape(n_s, s_s, 4, HEAD_DIM),
                      win_all[:, win_all.shape[1] - min(WINDOW, win_all.shape[1]):]))

    st_ = lambda ents, i: jnp.stack([e[i] for e in ents], axis=0)
    return (hp.reshape(n_p, s_p, D_MODEL), hs.reshape(n_s, s_s, D_MODEL),
            st_(ent_p, 0), st_(ent_s, 0), st_(ent_p, 1), st_(ent_s, 1),
            st_(ent_p, 2), st_(ent_s, 2), st_(ent_p, 3), st_(ent_s, 3),
            st_(ent_p, 4), st_(ent_s, 4))
```

```python
import functools
import math

import numpy as np
import jax
import jax.numpy as jnp
from jax import lax
from jax.experimental import pallas as pl
from jax.experimental.pallas import tpu as pltpu

F32 = jnp.float32
BF16 = jnp.bfloat16

D_MODEL = 1024
DEPTH = 2
HEAD_DIM = 64
HEADS_A = 4
HEADS_B = 4
HEADS_C = 4
D_FF = 2816
ROPE_THETA = 10000.0
MOBA_BLOCK = 256
MOBA_TOPK = 3
CMP_LEN = 32
CMP_STRIDE = 16
CMP_HIDDEN = 4 * HEAD_DIM
SEL_BLOCK = 64
SEL_TOPK = 16
WINDOW = 512
RMS_EPS = 1e-6
PAGE_SIZE = 128

A_W = HEADS_A * 2 * HEAD_DIM
B_W = HEADS_B * HEAD_DIM
C_W = HEADS_C * HEAD_DIM
KVC_W = 6 * HEAD_DIM
GATE_W = 3 * HEADS_C
MAIN_W = 3 * A_W + 3 * B_W + C_W + KVC_W
LANES = 128
QK_SCALE = HEAD_DIM ** -0.5
NEG_INF = float("-inf")
VMEM_LIMIT = 56 * 1024 * 1024

TM_FFN = 512
TF_FFN = 256
TM_PROJ = 512
TQ = 256
CMP_GROUP = 4
DEC_GROUP = 2

_O_QA, _O_KA, _O_VA = 0, A_W, 2 * A_W
_O_QB = 3 * A_W
_O_KB, _O_VB = _O_QB + B_W, _O_QB + 2 * B_W
_O_QC = _O_QB + 3 * B_W
_O_KVC = _O_QC + C_W


def _nn(a, b):
    return jnp.dot(a, b, preferred_element_type=F32)


def _nt(a, b):
    return lax.dot_general(a, b, (((1,), (1,)), ((), ())), preferred_element_type=F32)


def _split3(x):
    hi = x.astype(BF16)
    r1 = x - hi.astype(F32)
    mid = r1.astype(BF16)
    lo = (r1 - mid.astype(F32)).astype(BF16)
    return hi, mid, lo


def _nn_precise(a, b):
    a_hi, a_mid, _ = _split3(a)
    b_hi, b_mid, _ = _split3(b)
    return _nn(a_hi, b_hi) + (_nn(a_hi, b_mid) + _nn(a_mid, b_hi))


def _rms(x):
    return x * lax.rsqrt(jnp.mean(x * x, axis=-1, keepdims=True) + RMS_EPS)


def _iota(shape, dim):
    return lax.broadcasted_iota(jnp.int32, shape, dim)


def _params(n_axes):
    return pltpu.CompilerParams(dimension_semantics=("arbitrary",) * n_axes,
                                vmem_limit_bytes=VMEM_LIMIT)


def _lam_value(lam_ref, lam_init):
    lp = lam_ref[...]
    a = jnp.sum(lp[0:1] * lp[1:2], axis=1, keepdims=True)
    b = jnp.sum(lp[2:3] * lp[3:4], axis=1, keepdims=True)
    return jnp.exp(a) - jnp.exp(b) + lam_init


def _lam_init(lidx):
    return 0.8 - 0.6 * math.exp(-0.3 * lidx)


def _ffn_body(x_ref, g_ref, wg_ref, wu_ref, wd_ref, *rest, final):
    if final:
        gf_ref, o_ref = rest
    else:
        (o_ref,) = rest
    x = x_ref[...]
    xn = (_rms(x) * g_ref[...]).astype(BF16)
    acc = jnp.zeros_like(x)
    for f in range(D_FF // TF_FFN):
        sl = slice(f * TF_FFN, (f + 1) * TF_FFN)
        g = _nn(xn, wg_ref[:, sl])
        u = _nn(xn, wu_ref[:, sl])
        a = (g * jax.nn.sigmoid(g) * u).astype(BF16)
        acc = acc + _nn(a, wd_ref[sl, :])
    y = x + 0.5 * acc
    if final:
        y = _rms(y) * gf_ref[...]
    o_ref[...] = y


def _ffn(x, g, wg, wu, wd, g_final=None):
    m = x.shape[0]
    row = pl.BlockSpec((TM_FFN, D_MODEL), lambda i: (i, 0))
    vec = pl.BlockSpec((1, D_MODEL), lambda i: (0, 0))
    full = lambda shape: pl.BlockSpec(shape, lambda i: (0, 0))
    in_specs = [row, vec, full((D_MODEL, D_FF)), full((D_MODEL, D_FF)), full((D_FF, D_MODEL))]
    args = [x, g.reshape(1, D_MODEL), wg, wu, wd]
    if g_final is not None:
        in_specs.append(vec)
        args.append(g_final.reshape(1, D_MODEL))
    return pl.pallas_call(
        functools.partial(_ffn_body, final=g_final is not None),
        out_shape=jax.ShapeDtypeStruct((m, D_MODEL), F32),
        grid=(m // TM_FFN,),
        in_specs=in_specs,
        out_specs=row,
        compiler_params=_params(1),
        name="ffn_half",
    )(*args)


def _rope_rows(lane_shape):
    lane = _iota(lane_shape, 1)
    return (lane % HEAD_DIM) < (HEAD_DIM // 2), lane < HEAD_DIM


def _rope_lanes(x, cos, sin, lo32):
    sh = jnp.where(lo32, pltpu.roll(x, LANES - HEAD_DIM // 2, 1), pltpu.roll(x, HEAD_DIM // 2, 1))
    return x * cos + sh * sin


def _proj_queries(u, w_ref, cos, sin, lo32, lo64, qa16_ref, qb16_ref, qb32_ref, qcz16_ref):
    p = _nn(u, w_ref[:, _O_QA:_O_QA + A_W])
    for k in range(A_W // LANES):
        qa16_ref[:, k * LANES:(k + 1) * LANES] = (
            _rope_lanes(p[:, k * LANES:(k + 1) * LANES], cos, sin, lo32) * QK_SCALE).astype(BF16)
    p = _nn(u, w_ref[:, _O_QB:_O_QB + B_W])
    for k in range(B_W // LANES):
        r = _rope_lanes(p[:, k * LANES:(k + 1) * LANES], cos, sin, lo32) * QK_SCALE
        qb32_ref[:, k * LANES:(k + 1) * LANES] = r
        qb16_ref[:, k * LANES:(k + 1) * LANES] = r.astype(BF16)
    p = _nn(u, w_ref[:, _O_QC:_O_QC + C_W])
    for k in range(C_W // LANES):
        r = _rope_lanes(p[:, k * LANES:(k + 1) * LANES], cos, sin, lo32) * QK_SCALE
        even = jnp.where(lo64, r, 0.0)
        odd = jnp.where(lo64, pltpu.roll(r, HEAD_DIM, 1), 0.0)
        qcz16_ref[:, (2 * k) * LANES:(2 * k + 1) * LANES] = even.astype(BF16)
        qcz16_ref[:, (2 * k + 1) * LANES:(2 * k + 2) * LANES] = odd.astype(BF16)


def _proj_sample_body(h_ref, g_ref, w_ref, wgate_ref, cos_ref, sin_ref,
                      ka_ref, va_ref, bkv_ref, ckv_ref, win_ref,
                      qa16_ref, qb16_ref, qb32_ref, qcz16_ref, gate_ref):
    tm = h_ref.shape[0]
    u = (_rms(h_ref[...]) * g_ref[...]).astype(BF16)
    cos = cos_ref[...]
    sin = sin_ref[...]
    lo32, lo64 = _rope_rows((tm, LANES))
    _proj_queries(u, w_ref, cos, sin, lo32, lo64, qa16_ref, qb16_ref, qb32_ref, qcz16_ref)
    p = _nn(u, w_ref[:, _O_KA:_O_KA + A_W])
    for k in range(A_W // LANES):
        ka_ref[:, k * LANES:(k + 1) * LANES] = _rope_lanes(p[:, k * LANES:(k + 1) * LANES], cos, sin, lo32)
    va_ref[...] = _nn(u, w_ref[:, _O_VA:_O_VA + A_W])
    p = _nn(u, w_ref[:, _O_KB:_O_KB + B_W])
    for k in range(B_W // LANES):
        bkv_ref[:, k * LANES:(k + 1) * LANES] = _rope_lanes(p[:, k * LANES:(k + 1) * LANES], cos, sin, lo32)
    bkv_ref[:, B_W:2 * B_W] = _nn(u, w_ref[:, _O_VB:_O_VB + B_W])
    p = _nn(u, w_ref[:, _O_KVC:_O_KVC + KVC_W])
    for k in range(KVC_W // LANES):
        x = p[:, k * LANES:(k + 1) * LANES]
        r = jnp.where(lo64, _rope_lanes(x, cos, sin, lo32), x)
        if k < 2:
            ckv_ref[:, k * LANES:(k + 1) * LANES] = r
        else:
            win_ref[...] = r
    gate_ref[...] = jax.nn.sigmoid(_nn(u, wgate_ref[...]))


def _proj_sample(h, g_mix, w_main, w_gate, cos, sin):
    m = h.shape[0]
    tm = TM_PROJ
    row = lambda w: pl.BlockSpec((tm, w), lambda i: (i, 0))
    full = lambda shape: pl.BlockSpec(shape, lambda i: (0, 0))
    outs = ((A_W, F32), (A_W, F32), (2 * B_W, F32), (4 * HEAD_DIM, F32), (2 * HEAD_DIM, F32),
            (A_W, BF16), (B_W, BF16), (B_W, F32), (2 * C_W, BF16), (LANES, F32))
    return pl.pallas_call(
        _proj_sample_body,
        out_shape=[jax.ShapeDtypeStruct((m, w), dt) for w, dt in outs],
        grid=(m // tm,),
        in_specs=[row(D_MODEL), full((1, D_MODEL)), full((D_MODEL, MAIN_W)), full((D_MODEL, LANES)),
                  full((tm, LANES)), full((tm, LANES))],
        out_specs=[row(w) for w, _ in outs],
        compiler_params=_params(1),
        name="in_proj_rope_sample",
    )(h, g_mix.reshape(1, D_MODEL), w_main, w_gate, cos, sin)


_KT_ROWS = A_W + 2 * B_W + KVC_W


def _proj_prompt_body(h_ref, g_ref, w_ref, wt_ref, wgate_ref, cos_ref, sin_ref, cos_t_ref, sin_t_ref,
                      kat_ref, va_ref, bkvt_ref, ckvt_ref, wint_ref,
                      qa16_ref, kat16_ref, va16_ref, qb16_ref, qb32_ref, bkvt16_ref,
                      qcz16_ref, selt16_ref, wint16_ref, gate_ref):
    tm = h_ref.shape[0]
    half = HEAD_DIM // 2
    u = (_rms(h_ref[...]) * g_ref[...]).astype(BF16)
    lo32, lo64 = _rope_rows((tm, LANES))
    _proj_queries(u, w_ref, cos_ref[...], sin_ref[...], lo32, lo64, qa16_ref, qb16_ref, qb32_ref, qcz16_ref)

    p = _nn(u, w_ref[:, _O_VA:_O_VA + A_W])
    va16_ref[...] = p.astype(BF16)
    for h in range(HEADS_A):
        va_ref[pl.ds(h, tm, stride=HEADS_A), :] = p[:, h * LANES:(h + 1) * LANES]

    cos_t = cos_t_ref[...]
    sin_t = sin_t_ref[...]

    def rope_t(x):
        x1, x2 = x[0:half], x[half:HEAD_DIM]
        return jnp.concatenate([x1 * cos_t - x2 * sin_t, x2 * cos_t + x1 * sin_t], axis=0)

    def store_t(f32_ref, b16_ref, row0, val):
        rows = val.shape[0]
        if f32_ref is not None:
            f32_ref[row0:row0 + rows, :] = val
        if b16_ref is not None:
            for t in range(tm // TQ):
                b16_ref[t, row0:row0 + rows, :] = val[:, t * TQ:(t + 1) * TQ].astype(BF16)

    pt = _nt(wt_ref[0:A_W, :], u)
    for g in range(A_W // HEAD_DIM):
        store_t(kat_ref, kat16_ref, g * HEAD_DIM, rope_t(pt[g * HEAD_DIM:(g + 1) * HEAD_DIM]))
    pt = _nt(wt_ref[A_W:A_W + 2 * B_W, :], u)
    for g in range(B_W // HEAD_DIM):
        store_t(bkvt_ref, bkvt16_ref, g * HEAD_DIM, rope_t(pt[g * HEAD_DIM:(g + 1) * HEAD_DIM]))
    store_t(bkvt_ref, bkvt16_ref, B_W, pt[B_W:2 * B_W])
    pt = _nt(wt_ref[A_W + 2 * B_W:_KT_ROWS, :], u)
    for g in range(KVC_W // HEAD_DIM):
        x = pt[g * HEAD_DIM:(g + 1) * HEAD_DIM]
        if g % 2 == 0:
            x = rope_t(x)
        if g < 2:
            store_t(ckvt_ref, None, g * HEAD_DIM, x)
        elif g < 4:
            store_t(ckvt_ref, None, g * HEAD_DIM, x)
            store_t(None, selt16_ref, (g - 2) * HEAD_DIM, x)
        else:
            store_t(wint_ref, wint16_ref, (g - 4) * HEAD_DIM, x)
    gate_ref[...] = jax.nn.sigmoid(_nn(u, wgate_ref[...]))


def _proj_prompt(h, g_mix, w_main, w_t, w_gate, tabs, n, s):
    cos, sin, cos_t, sin_t = tabs
    m = n * s
    tm = TM_PROJ
    per_seq = s // tm
    nt = tm // TQ
    row = lambda w: pl.BlockSpec((tm, w), lambda i: (i, 0))
    full = lambda shape: pl.BlockSpec(shape, lambda i: (0,) * len(shape))
    tab = pl.BlockSpec((tm, LANES), lambda i: (i % per_seq, 0))
    tab_t = pl.BlockSpec((HEAD_DIM // 2, tm), lambda i: (0, i % per_seq))
    feat = lambda w: pl.BlockSpec((None, w, tm), lambda i: (i // per_seq, 0, i % per_seq))
    tiles = lambda w: pl.BlockSpec((None, nt, w, TQ), lambda i: (i // per_seq, i % per_seq, 0, 0))
    sds = jax.ShapeDtypeStruct
    out_shape = [sds((n, A_W, s), F32), sds((m * HEADS_A, LANES), F32), sds((n, 2 * B_W, s), F32),
                 sds((n, 4 * HEAD_DIM, s), F32), sds((n, 2 * HEAD_DIM, s), F32),
                 sds((m, A_W), BF16), sds((n, s // TQ, A_W, TQ), BF16), sds((m, A_W), BF16),
                 sds((m, B_W), BF16), sds((m, B_W), F32), sds((n, s // TQ, 2 * B_W, TQ), BF16),
                 sds((m, 2 * C_W), BF16), sds((n, s // TQ, 2 * HEAD_DIM, TQ), BF16),
                 sds((n, s // TQ, 2 * HEAD_DIM, TQ), BF16), sds((m, LANES), F32)]
    out_specs = [feat(A_W), pl.BlockSpec((tm * HEADS_A, LANES), lambda i: (i, 0)), feat(2 * B_W),
                 feat(4 * HEAD_DIM), feat(2 * HEAD_DIM),
                 row(A_W), tiles(A_W), row(A_W), row(B_W), row(B_W), tiles(2 * B_W),
                 row(2 * C_W), tiles(2 * HEAD_DIM), tiles(2 * HEAD_DIM), row(LANES)]
    return pl.pallas_call(
        _proj_prompt_body,
        out_shape=out_shape,
        grid=(m // tm,),
        in_specs=[row(D_MODEL), full((1, D_MODEL)), full((D_MODEL, MAIN_W)), full((_KT_ROWS, D_MODEL)),
                  full((D_MODEL, LANES)), tab, tab, tab_t, tab_t],
        out_specs=out_specs,
        compiler_params=_params(1),
        name="in_proj_rope_prompt",
    )(h, g_mix.reshape(1, D_MODEL), w_main, w_t, w_gate, cos, sin, cos_t, sin_t)


def _outproj_body(h_ref, oa_ref, ob_ref, oc_ref, w_ref, o_ref):
    y = h_ref[...] + _nn(oa_ref[...], w_ref[0:A_W, :])
    y = y + _nn(ob_ref[...], w_ref[A_W:A_W + B_W, :])
    y = y + _nn(oc_ref[...], w_ref[A_W + B_W:, :])
    o_ref[...] = y


def _outproj(h, oa, ob, oc, w_out):
    m = h.shape[0]
    tm = TM_PROJ
    row = lambda w: pl.BlockSpec((tm, w), lambda i: (i, 0))
    return pl.pallas_call(
        _outproj_body,
        out_shape=jax.ShapeDtypeStruct((m, D_MODEL), F32),
        grid=(m // tm,),
        in_specs=[row(D_MODEL), row(A_W), row(B_W), row(C_W),
                  pl.BlockSpec((D_MODEL, D_MODEL), lambda i: (0, 0))],
        out_specs=row(D_MODEL),
        compiler_params=_params(1),
        name="out_proj",
    )(h, oa, ob, oc, w_out)


def _softmax_start(s):
    m = jnp.max(s, axis=-1, keepdims=True)
    p = jnp.exp(s - m)
    return m, jnp.sum(p, axis=-1, keepdims=True), p


def _softmax_step(s, m, l):
    m_new = jnp.maximum(m, jnp.max(s, axis=-1, keepdims=True))
    alpha = jnp.exp(m - m_new)
    p = jnp.exp(s - m_new)
    return m_new, alpha, alpha * l + jnp.sum(p, axis=-1, keepdims=True), p


def _rank_lower(x, n, width_iota):
    rank = jnp.zeros(x.shape, F32)
    for bp in range(n):
        col = x[:, bp:bp + 1]
        tie = jnp.where(bp < width_iota, 1.0, 0.0)
        rank = rank + jnp.where(col > x, 1.0, jnp.where(col == x, tie, 0.0))
    return rank


def _rank_lower_t(x, n, row_iota):
    rank = jnp.zeros(x.shape, F32)
    for bp in range(n):
        row = x[bp:bp + 1, :]
        tie = jnp.where(bp < row_iota, 1.0, 0.0)
        rank = rank + jnp.where(row > x, 1.0, jnp.where(row == x, tie, 0.0))
    return rank


def _head_rms_scale(o, gh, lam_init):
    return _rms(o) * gh * (1.0 - lam_init)


def _attn_a_prompt_body(q_ref, kt_ref, v_ref, lam_ref, gh_ref, o_ref, *, lam_init):
    i = pl.program_id(1)
    tq = q_ref.shape[0]
    lam = _lam_value(lam_ref, lam_init)
    gh = gh_ref[...]
    lane = _iota((tq, LANES), 1)
    causal = _iota((tq, tq), 1) <= _iota((tq, tq), 0)
    head_sl = [slice(h * LANES, (h + 1) * LANES) for h in range(HEADS_A)]
    chains = [(h, c) for h in range(HEADS_A) for c in range(2)]
    qms = []
    for h, c in chains:
        q2 = q_ref[:, head_sl[h]]
        in_c = (lane >= c * HEAD_DIM) & (lane < (c + 1) * HEAD_DIM)
        qms.append(jnp.where(in_c, q2, jnp.zeros_like(q2)))

    def rows(j):
        return pl.ds(pl.multiple_of(j * tq, tq), tq)

    state = []
    for idx, (h, c) in enumerate(chains):
        s = jnp.where(causal, _nn(qms[idx], kt_ref[i, head_sl[h], :]), NEG_INF)
        m, l, p = _softmax_start(s)
        state += [m, l, _nn(p.astype(BF16), v_ref[rows(i), head_sl[h]])]

    def body(j, carry):
        out = []
        for idx, (h, c) in enumerate(chains):
            m, l, acc = carry[3 * idx:3 * idx + 3]
            m, alpha, l, p = _softmax_step(_nn(qms[idx], kt_ref[j, head_sl[h], :]), m, l)
            out += [m, l, alpha * acc + _nn(p.astype(BF16), v_ref[rows(j), head_sl[h]])]
        return tuple(out)

    state = lax.fori_loop(0, i, body, tuple(state))
    for h in range(HEADS_A):
        o0 = state[6 * h + 2] / state[6 * h + 1]
        o1 = state[6 * h + 5] / state[6 * h + 4]
        o_ref[:, head_sl[h]] = _head_rms_scale(o0 - lam * o1, gh, lam_init).astype(BF16)


def _attn_a_prompt(q16, kt16, v16, lam_p, g_head, n, s, lidx):
    nq = s // TQ
    qspec = pl.BlockSpec((TQ, A_W), lambda b, i: (b * nq + i, 0))
    return pl.pallas_call(
        functools.partial(_attn_a_prompt_body, lam_init=_lam_init(lidx)),
        out_shape=jax.ShapeDtypeStruct((n * s, A_W), BF16),
        grid=(n, nq),
        in_specs=[qspec,
                  pl.BlockSpec((None, nq, A_W, TQ), lambda b, i: (b, 0, 0, 0)),
                  pl.BlockSpec((s, A_W), lambda b, i: (b, 0)),
                  pl.BlockSpec((4, HEAD_DIM), lambda b, i: (0, 0)),
                  pl.BlockSpec((1, 2 * HEAD_DIM), lambda b, i: (0, 0))],
        out_specs=qspec,
        compiler_params=_params(2),
        name="diff_attn_prompt",
    )(q16, kt16, v16, lam_p, g_head.reshape(1, 2 * HEAD_DIM))


def _block_means_t(blocks):
    feats = blocks[0].shape[0]
    lane = _iota((feats, LANES), 1)
    out = jnp.zeros((feats, LANES), F32)
    for b, blk in enumerate(blocks):
        out = jnp.where(lane == b, jnp.sum(blk, axis=1, keepdims=True) * (1.0 / MOBA_BLOCK), out)
    return out


def _attn_b_prompt_body(q_ref, q32_ref, k32t_ref, kvt_ref, o_ref, kmean_ref):
    i = pl.program_id(1)
    tq = q_ref.shape[0]
    nb = k32t_ref.shape[1] // MOBA_BLOCK

    @pl.when(i == 0)
    def _():
        kmean_ref[...] = _block_means_t([k32t_ref[:, b * MOBA_BLOCK:(b + 1) * MOBA_BLOCK] for b in range(nb)])

    lane = _iota((tq, LANES), 1)
    causal = _iota((tq, tq), 1) <= _iota((tq, tq), 0)
    past = lane < i
    ksl = [slice((h // 2) * LANES, (h // 2 + 1) * LANES) for h in range(HEADS_B)]
    vsl = [slice(B_W + (h // 2) * LANES, B_W + (h // 2 + 1) * LANES) for h in range(HEADS_B)]
    qms, chosen, state = [], [], []
    for h in range(HEADS_B):
        q2 = q_ref[:, ksl[h]]
        in_h = (lane >= (h % 2) * HEAD_DIM) & (lane < (h % 2 + 1) * HEAD_DIM)
        qms.append(jnp.where(in_h, q2, jnp.zeros_like(q2)))
        gate = _nn_precise(jnp.where(in_h, q32_ref[:, ksl[h]], 0.0), kmean_ref[ksl[h], :])
        gate = jnp.where(past, gate, NEG_INF)
        chosen.append(jnp.where((_rank_lower(gate, nb, lane) < MOBA_TOPK) & past, 1.0, 0.0))
        s = jnp.where(causal, _nn(qms[h], kvt_ref[i, ksl[h], :]), NEG_INF)
        m, l, p = _softmax_start(s)
        state += [m, l, _nt(p.astype(BF16), kvt_ref[i, vsl[h], :])]

    def body(j, carry):
        out = []
        for h in range(HEADS_B):
            m, l, acc = carry[3 * h:3 * h + 3]
            use = jnp.max(jnp.where(lane == j, chosen[h], 0.0), axis=1, keepdims=True)
            s = jnp.where(use > 0.5, _nn(qms[h], kvt_ref[j, ksl[h], :]), NEG_INF)
            m, alpha, l, p = _softmax_step(s, m, l)
            out += [m, l, alpha * acc + _nt(p.astype(BF16), kvt_ref[j, vsl[h], :])]
        return tuple(out)

    state = lax.fori_loop(0, i, body, tuple(state))
    for pair in range(HEADS_B // 2):
        o0 = state[6 * pair + 2] / state[6 * pair + 1]
        o1 = state[6 * pair + 5] / state[6 * pair + 4]
        o_ref[:, ksl[2 * pair]] = jnp.where(lane < HEAD_DIM, o0, o1).astype(BF16)


def _attn_b_prompt(q16, q32, bkvt32, bkvt16, n, s):
    nq = s // TQ
    assert TQ == MOBA_BLOCK and s // MOBA_BLOCK <= LANES
    qspec = pl.BlockSpec((TQ, B_W), lambda b, i: (b * nq + i, 0))
    return pl.pallas_call(
        _attn_b_prompt_body,
        out_shape=jax.ShapeDtypeStruct((n * s, B_W), BF16),
        grid=(n, nq),
        in_specs=[qspec, qspec,
                  pl.BlockSpec((None, B_W, s), lambda b, i: (b, 0, 0)),
                  pl.BlockSpec((None, nq, 2 * B_W, TQ), lambda b, i: (b, 0, 0, 0))],
        out_specs=qspec,
        scratch_shapes=[pltpu.VMEM((B_W, LANES), F32)],
        compiler_params=_params(2),
        name="moba_attn_prompt",
    )(q16, q32, bkvt32, bkvt16)


_CMP_ROWS = 128


def _compress_core(xs_ref, pe_ref, w1_ref, w2_ref, o_ref):
    half = CMP_LEN // 2
    n_chunk = xs_ref.shape[0] // CMP_STRIDE
    acc = [None, None]
    def rows(r):
        return (xs_ref[pl.ds(r % half, n_chunk, stride=CMP_STRIDE), :] + pe_ref[r:r + 1, :]).astype(BF16)

    for r in range(0, CMP_LEN, 2):
        t = _nn(jnp.concatenate([rows(r), rows(r + 1)], axis=1), w1_ref[r // 2])
        acc[r // half] = t if acc[r // half] is None else acc[r // half] + t
    pre = acc[0] + pltpu.roll(acc[1], n_chunk - 1, 0)
    hid = jax.nn.gelu(pre).astype(BF16)
    o_ref[...] = _nn(hid, w2_ref[...]).astype(BF16)


def _compress_prompt_body(xt_ref, pe_ref, w1_ref, w2_ref, o_ref, xs_ref):
    for j in range(xt_ref.shape[1] // LANES):
        xs_ref[j * LANES:(j + 1) * LANES, :] = xt_ref[:, j * LANES:(j + 1) * LANES].T
    _compress_core(xs_ref, pe_ref, w1_ref, w2_ref, o_ref)


def _compress_pages_body(pt_ref, pe_ref, w1_ref, w2_ref, *refs):
    pages, o_ref, xs_ref = refs[:-2], refs[-2], refs[-1]
    for j, pg in enumerate(pages):
        xs_ref[j * PAGE_SIZE:(j + 1) * PAGE_SIZE, :] = pg[...].T
    _compress_core(xs_ref, pe_ref, w1_ref, w2_ref, o_ref)


def _compress_weights(w_cmp1, w_cmp2, cmp_pos):
    w1 = w_cmp1.reshape(2, CMP_LEN, HEAD_DIM, CMP_HIDDEN)
    z1 = jnp.zeros((CMP_LEN, HEAD_DIM, CMP_HIDDEN), F32)
    top = jnp.concatenate([w1[0], z1], axis=2)
    bot = jnp.concatenate([z1, w1[1]], axis=2)
    w1c = jnp.concatenate([top, bot], axis=1).astype(BF16)
    w1c = w1c.reshape(CMP_LEN // 2, 2 * LANES, 2 * CMP_HIDDEN)
    z2 = jnp.zeros((CMP_HIDDEN, HEAD_DIM), F32)
    w2c = jnp.concatenate([jnp.concatenate([w_cmp2[0], z2], axis=1),
                           jnp.concatenate([z2, w_cmp2[1]], axis=1)], axis=0).astype(BF16)
    pe = jnp.concatenate([cmp_pos[0], cmp_pos[1]], axis=1)
    return pe, w1c, w2c


def _const_specs(index):
    return [pl.BlockSpec((CMP_LEN, LANES), index(2)),
            pl.BlockSpec((CMP_LEN // 2, 2 * LANES, 2 * CMP_HIDDEN), index(3)),
            pl.BlockSpec((2 * CMP_HIDDEN, LANES), index(2))]


def _compress_prompt(ckvt32, cw, n, s):
    assert s // CMP_STRIDE == _CMP_ROWS
    zero = lambda nd: (lambda b: (0,) * nd)
    return pl.pallas_call(
        _compress_prompt_body,
        out_shape=jax.ShapeDtypeStruct((n * _CMP_ROWS, LANES), BF16),
        grid=(n,),
        in_specs=[pl.BlockSpec((None, LANES, s), lambda b: (b, 0, 0))] + _const_specs(zero),
        out_specs=pl.BlockSpec((_CMP_ROWS, LANES), lambda b: (b, 0)),
        scratch_shapes=[pltpu.VMEM((s, LANES), F32)],
        compiler_params=_params(1),
        name="nsa_compress_prompt",
    )(ckvt32, *cw)


def _page_index(l, g, group, j, row_block, b, pt):
    return (l, pt[b * group + g, j], row_block, 0)


def _page_specs(l, n_pages, rows, row_block=0, group=1):
    return [pl.BlockSpec((None, None, rows, PAGE_SIZE), functools.partial(_page_index, l, g, group, j, row_block))
            for g in range(group) for j in range(n_pages)]


def _compress_pages(cache_ct, page_table, cw, l):
    n, n_pages = page_table.shape
    group = CMP_GROUP
    assert n_pages * PAGE_SIZE // CMP_STRIDE == _CMP_ROWS and n % group == 0
    zero = lambda nd: (lambda b, pt: (0,) * nd)
    grid_spec = pltpu.PrefetchScalarGridSpec(
        num_scalar_prefetch=1, grid=(n // group,),
        in_specs=_const_specs(zero) + _page_specs(l, n_pages, LANES, 0, group),
        out_specs=pl.BlockSpec((group * _CMP_ROWS, LANES), lambda b, pt: (b, 0)),
        scratch_shapes=[pltpu.VMEM((group * n_pages * PAGE_SIZE, LANES), F32)])
    return pl.pallas_call(
        _compress_pages_body,
        out_shape=jax.ShapeDtypeStruct((n * _CMP_ROWS, LANES), BF16),
        grid_spec=grid_spec,
        compiler_params=_params(1),
        name="nsa_compress_pages",
    )(page_table, *cw, *([cache_ct] * (n_pages * group)))


def _overlap_matrix(t_len):
    n_cmp = (t_len - CMP_LEN) // CMP_STRIDE + 1
    nsb = -(-t_len // SEL_BLOCK)
    starts = np.arange(n_cmp) * CMP_STRIDE
    sb = np.arange(nsb) * SEL_BLOCK
    ov = np.clip(np.minimum(starts[:, None] + CMP_LEN, sb[None, :] + SEL_BLOCK)
                 - np.maximum(starts[:, None], sb[None, :]), 0, None) / CMP_STRIDE
    out = np.zeros((_CMP_ROWS, LANES), np.float32)
    out[:n_cmp, :nsb] = ov
    return jnp.asarray(out, BF16), nsb


def _expand_matrix(s):
    nt = s // TQ
    e = np.zeros((nt, LANES, TQ), np.float32)
    for j in range(nt):
        for k in range(TQ):
            e[j, (j * TQ + k) // SEL_BLOCK, k] = 1.0
    return jnp.asarray(e, BF16)


def _nsa_flags(p_sum, ovl, own, lane, nsb):
    imp = None
    for part in _split3(p_sum):
        t = _nn(part, ovl)
        imp = t if imp is None else imp + t
    forced = (lane == 0) | (lane == own) | (lane == own - 1)
    imp = jnp.where(lane > own, NEG_INF, jnp.where(forced, jnp.inf, imp))
    rank = _rank_lower(imp, nsb, lane)
    return jnp.where((rank < SEL_TOPK) & (lane <= own), 1.0, 0.0)


def _nsa_flags_t(p_sum, ovl, pos0, nsb):
    tq = p_sum.shape[0]
    imp = None
    for part in _split3(p_sum):
        t = _nn(part, ovl)
        imp = t if imp is None else imp + t
    rows = -(-nsb // 8) * 8
    imp_t = imp.T[0:rows]
    blk = _iota((rows, tq), 0)
    own = (pos0 + _iota((rows, tq), 1)) // SEL_BLOCK
    forced = (blk == 0) | (blk == own) | (blk == own - 1)
    imp_t = jnp.where(blk > own, NEG_INF, jnp.where(forced, jnp.inf, imp_t))
    rank = _rank_lower_t(imp_t, nsb, blk)
    flag_t = jnp.where((rank < SEL_TOPK) & (blk <= own), 1.0, 0.0)
    flag_t = jnp.concatenate([flag_t, jnp.zeros((LANES - rows, tq), F32)], axis=0)
    return flag_t.T


def _masked_probs(s, mask):
    s = jnp.where(mask, s, NEG_INF)
    m = jnp.max(s, axis=-1, keepdims=True)
    m = jnp.where(m > NEG_INF, m, 0.0)
    e = jnp.exp(s - m)
    d = jnp.sum(e, axis=-1, keepdims=True)
    return e / jnp.where(d > 0, d, 1.0)


def _attn_c_prompt_body(qz_ref, gate_ref, selt_ref, wint_ref, kvc_ref, ovl_ref, exp_ref, o_ref, *, nsb):
    i = pl.program_id(1)
    tq = gate_ref.shape[0]
    nh = HEADS_C
    q4 = jnp.concatenate([qz_ref[:, h * LANES:(h + 1) * LANES] for h in range(nh)], axis=0)
    lane = _iota((tq, LANES), 1)
    pos = i * tq + _iota((tq, LANES), 0)
    rr = _iota((tq, tq), 0)
    cc = _iota((tq, tq), 1)
    causal = cc <= rr

    kvc = kvc_ref[...]
    cmp_ok = (CMP_STRIDE * lane + (CMP_LEN - 1)) <= pos
    s = _nt(q4, kvc).reshape(nh, tq, LANES)
    p = _masked_probs(s, cmp_ok[None])
    o_cmp = _nn(p.reshape(nh * tq, LANES).astype(BF16), kvc)
    p_sum = p[0] + p[1] + p[2] + p[3]

    flag = _nsa_flags_t(p_sum, ovl_ref[...], i * tq, nsb).astype(BF16)

    kd = selt_ref[i]
    ok = (_nn(flag, exp_ref[i]) > 0.5) & causal
    s = jnp.where(ok[None], _nn(q4, kd).reshape(nh, tq, tq), NEG_INF)
    m, l, p = _softmax_start(s)
    acc = _nt(p.reshape(nh * tq, tq).astype(BF16), kd).reshape(nh, tq, LANES)

    def body(j, carry):
        m, l, acc = carry
        kj = selt_ref[j]
        ok = _nn(flag, exp_ref[j]) > 0.5
        s = jnp.where(ok[None], _nn(q4, kj).reshape(nh, tq, tq), NEG_INF)
        m, alpha, l, p = _softmax_step(s, m, l)
        pv = _nt(p.reshape(nh * tq, tq).astype(BF16), kj).reshape(nh, tq, LANES)
        return m, l, alpha * acc + pv

    m, l, acc = lax.fori_loop(0, i, body, (m, l, acc))
    o_sel = acc / l

    assert WINDOW == 2 * tq
    w2 = wint_ref[jnp.maximum(i - 2, 0)]
    w1 = wint_ref[jnp.maximum(i - 1, 0)]
    w0 = wint_ref[i]
    s2 = jnp.where(((cc >= rr) & (i >= 2))[None], _nn(q4, w2).reshape(nh, tq, tq), NEG_INF)
    s1 = jnp.where(i >= 1, _nn(q4, w1).reshape(nh, tq, tq), NEG_INF)
    s0 = jnp.where(causal[None], _nn(q4, w0).reshape(nh, tq, tq), NEG_INF)
    m = jnp.maximum(jnp.maximum(jnp.max(s2, -1, keepdims=True), jnp.max(s1, -1, keepdims=True)),
                    jnp.max(s0, -1, keepdims=True))
    e2, e1, e0 = jnp.exp(s2 - m), jnp.exp(s1 - m), jnp.exp(s0 - m)
    d = jnp.sum(e2, -1, keepdims=True) + jnp.sum(e1, -1, keepdims=True) + jnp.sum(e0, -1, keepdims=True)
    flat = lambda e: e.reshape(nh * tq, tq).astype(BF16)
    o_win = (_nt(flat(e2), w2) + _nt(flat(e1), w1) + _nt(flat(e0), w0)).reshape(nh, tq, LANES) / d

    o_cmp = o_cmp.reshape(nh, tq, LANES)
    g = gate_ref[...]
    heads = []
    for h in range(nh):
        heads.append(g[:, 3 * h:3 * h + 1] * o_cmp[h] + g[:, 3 * h + 1:3 * h + 2] * o_sel[h]
                     + g[:, 3 * h + 2:3 * h + 3] * o_win[h])
    for pair in range(nh // 2):
        both = jnp.where(lane < HEAD_DIM, pltpu.roll(heads[2 * pair], HEAD_DIM, 1), heads[2 * pair + 1])
        o_ref[:, pair * LANES:(pair + 1) * LANES] = both.astype(BF16)


def _attn_c_prompt(qz16, gates, selt16, wint16, kvcmp16, n, s):
    nq = s // TQ
    ovl, nsb = _overlap_matrix(s)
    expand = _expand_matrix(s)
    qrow = lambda w: pl.BlockSpec((TQ, w), lambda b, i: (b * nq + i, 0))
    tiles = pl.BlockSpec((None, nq, 2 * HEAD_DIM, TQ), lambda b, i: (b, 0, 0, 0))
    return pl.pallas_call(
        functools.partial(_attn_c_prompt_body, nsb=nsb),
        out_shape=jax.ShapeDtypeStruct((n * s, C_W), BF16),
        grid=(n, nq),
        in_specs=[qrow(2 * C_W), qrow(LANES), tiles, tiles,
                  pl.BlockSpec((_CMP_ROWS, LANES), lambda b, i: (b, 0)),
                  pl.BlockSpec((_CMP_ROWS, LANES), lambda b, i: (0, 0)),
                  pl.BlockSpec((nq, LANES, TQ), lambda b, i: (0, 0, 0))],
        out_specs=qrow(C_W),
        compiler_params=_params(2),
        name="nsa_attn_prompt",
    )(qz16, gates, selt16, wint16, kvcmp16, ovl, expand)


def _pad_page(x):
    rows, w = x.shape
    return jnp.concatenate([x, jnp.zeros((PAGE_SIZE - rows, w), x.dtype)], axis=0)


def _new_page_mask(n_rows, nq):
    r = _iota((n_rows, PAGE_SIZE), 0) % nq
    t = _iota((n_rows, PAGE_SIZE), 1)
    return t <= r


def _dec_a_body(pt_ref, q_ref, kn_ref, vn_ref, lam_ref, gh_ref, *refs, lam_init):
    n_pages = (len(refs) - 1) // 2
    kp, vp, o_ref = refs[:n_pages], refs[n_pages:2 * n_pages], refs[-1]
    nq = q_ref.shape[0]
    rows = 2 * HEADS_A * nq
    per_head = 2 * nq
    lam = _lam_value(lam_ref, lam_init)
    qt = jnp.concatenate([q_ref[...]] * (2 * HEADS_A), axis=0)
    diag = (_iota((rows, A_W), 0) // nq) == (_iota((rows, A_W), 1) // HEAD_DIM)
    qbd = jnp.where(diag, qt, jnp.zeros_like(qt))
    kn = _pad_page(kn_ref[...]).astype(BF16)
    vn = _pad_page(vn_ref[...]).astype(BF16)
    s_new = jnp.where(_new_page_mask(rows, nq), _nt(qbd, kn), NEG_INF)
    scores = [_nn(qbd, kp[j][...].astype(BF16)) for j in range(n_pages)]
    m = jnp.max(s_new, axis=1, keepdims=True)
    for s in scores:
        m = jnp.maximum(m, jnp.max(s, axis=1, keepdims=True))
    e = jnp.exp(s_new - m)
    d = jnp.sum(e, axis=1, keepdims=True)
    e = e.astype(BF16)
    accs = [_nn(e[h * per_head:(h + 1) * per_head], vn[:, h * LANES:(h + 1) * LANES]) for h in range(HEADS_A)]
    for j in range(n_pages):
        e = jnp.exp(scores[j] - m)
        d = d + jnp.sum(e, axis=1, keepdims=True)
        e = e.astype(BF16)
        for h in range(HEADS_A):
            vh = vp[j][pl.ds(h, PAGE_SIZE, stride=HEADS_A), :].astype(BF16)
            accs[h] = accs[h] + _nn(e[h * per_head:(h + 1) * per_head], vh)
    gh = gh_ref[...]
    for h in range(HEADS_A):
        on = accs[h] / d[h * per_head:(h + 1) * per_head]
        o = on[0:nq] - lam * on[nq:2 * nq]
        o_ref[:, h * LANES:(h + 1) * LANES] = _head_rms_scale(o, gh, lam_init).astype(BF16)


def _dec_a(q16, kn32, vn32, lam_p, g_head, cache_kt, cache_v4, page_table, l, nq):
    n, n_pages = page_table.shape
    row = lambda w: pl.BlockSpec((nq, w), lambda b, pt: (b, 0))
    grid_spec = pltpu.PrefetchScalarGridSpec(
        num_scalar_prefetch=1, grid=(n,),
        in_specs=[row(A_W), row(A_W), row(A_W),
                  pl.BlockSpec((4, HEAD_DIM), lambda b, pt: (0, 0)),
                  pl.BlockSpec((1, 2 * HEAD_DIM), lambda b, pt: (0, 0))]
                 + _page_specs(l, n_pages, A_W) + _page_specs(l, n_pages, A_W),
        out_specs=row(A_W))
    return pl.pallas_call(
        functools.partial(_dec_a_body, lam_init=_lam_init(l)),
        out_shape=jax.ShapeDtypeStruct((n * nq, A_W), BF16),
        grid_spec=grid_spec,
        compiler_params=_params(1),
        name="diff_attn_decode",
    )(page_table, q16, kn32, vn32, lam_p, g_head.reshape(1, 2 * HEAD_DIM),
      *([cache_kt] * n_pages), *([cache_v4] * n_pages))


def _dec_b_body(pt_ref, q_ref, q32_ref, kvn_ref, *refs, group):
    pages, o_ref = refs[:-1], refs[-1]
    n_pages = len(pages) // group
    nq = q_ref.shape[0] // group
    for g in range(group):
        r = slice(g * nq, (g + 1) * nq)
        o_ref[r, :] = _dec_b_one(q_ref[r, :], q32_ref[r, :], kvn_ref[r, :],
                                 pages[g * n_pages:(g + 1) * n_pages]).astype(BF16)


def _dec_b_one(q, q32, kvn_rows, pages):
    n_pages = len(pages)
    nq = q.shape[0]
    rows = HEADS_B * nq
    pages_per_blk = MOBA_BLOCK // PAGE_SIZE
    nb = n_pages // pages_per_blk
    diag = (_iota((rows, B_W), 0) // nq) == (_iota((rows, B_W), 1) // HEAD_DIM)
    qt = jnp.concatenate([q] * HEADS_B, axis=0)
    qbd = jnp.where(diag, qt, jnp.zeros_like(qt))
    qbd32 = jnp.where(diag, jnp.concatenate([q32] * HEADS_B, axis=0), 0.0)

    blocks = []
    for b in range(nb):
        blocks.append(jnp.concatenate([pages[j][0:B_W, :] for j in range(b * pages_per_blk, (b + 1) * pages_per_blk)],
                                      axis=1))
    kmean_t = _block_means_t(blocks)
    lane = _iota((rows, LANES), 1)
    gate = _nn_precise(qbd32, kmean_t)
    gate = jnp.where(lane < nb, gate, NEG_INF)
    chosen = jnp.where((_rank_lower(gate, nb, lane) < MOBA_TOPK) & (lane < nb), 1.0, 0.0)

    kvn = _pad_page(kvn_rows).astype(BF16)
    s_new = jnp.where(_new_page_mask(rows, nq), _nt(qbd, kvn[:, 0:B_W]), NEG_INF)
    scores = []
    for j in range(n_pages):
        b = j // pages_per_blk
        s = _nn(qbd, pages[j][0:B_W, :].astype(BF16))
        scores.append(jnp.where(chosen[:, b:b + 1] > 0.5, s, NEG_INF))
    m = jnp.max(s_new, axis=1, keepdims=True)
    for s in scores:
        m = jnp.maximum(m, jnp.max(s, axis=1, keepdims=True))
    e = jnp.exp(s_new - m)
    d = jnp.sum(e, axis=1, keepdims=True)
    acc = _nn(e.astype(BF16), kvn[:, B_W:2 * B_W])
    for j in range(n_pages):
        e = jnp.exp(scores[j] - m)
        d = d + jnp.sum(e, axis=1, keepdims=True)
        acc = acc + _nt(e.astype(BF16), pages[j][B_W:2 * B_W, :].astype(BF16))
    on = acc / d
    head_of_lane = _iota((nq, B_W), 1) // HEAD_DIM
    o = jnp.zeros((nq, B_W), F32)
    for h in range(HEADS_B):
        o = o + jnp.where(head_of_lane == h, on[h * nq:(h + 1) * nq], 0.0)
    return o


def _dec_b(q16, q32, kvn32, cache_kvt, page_table, l, nq):
    n, n_pages = page_table.shape
    group = DEC_GROUP
    row = lambda w: pl.BlockSpec((group * nq, w), lambda b, pt: (b, 0))
    grid_spec = pltpu.PrefetchScalarGridSpec(
        num_scalar_prefetch=1, grid=(n // group,),
        in_specs=[row(B_W), row(B_W), row(2 * B_W)] + _page_specs(l, n_pages, 2 * B_W, group=group),
        out_specs=row(B_W))
    return pl.pallas_call(
        functools.partial(_dec_b_body, group=group),
        out_shape=jax.ShapeDtypeStruct((n * nq, B_W), BF16),
        grid_spec=grid_spec,
        compiler_params=_params(1),
        name="moba_attn_decode",
    )(page_table, q16, q32, kvn32, *([cache_kvt] * (n_pages * group)))


def _dec_c_body(pt_ref, qz_ref, gate_ref, ckvn_ref, winn_ref, kvc_ref, ovl_ref, st_ref, *refs, nsb, q0, group):
    pages, o_ref = refs[:-1], refs[-1]
    n_pages = len(pages) // group
    nq = gate_ref.shape[0] // group
    ovl = ovl_ref[...]
    for g in range(group):
        r = slice(g * nq, (g + 1) * nq)
        _dec_c_one(qz_ref[r, :], gate_ref[r, :], ckvn_ref[r, :], winn_ref[r, :],
                   kvc_ref[g * _CMP_ROWS:(g + 1) * _CMP_ROWS, :], ovl, st_ref[g],
                   pages[g * n_pages:(g + 1) * n_pages], o_ref, r, nsb, q0)


def _dec_c_one(qz, g, ckvn_rows, winn_rows, kvc, ovl, st32, pages, o_ref, out_rows, nsb, q0):
    n_pages = len(pages)
    nq = g.shape[0]
    nh = HEADS_C
    rows = nh * nq
    q4 = jnp.concatenate([qz[:, h * LANES:(h + 1) * LANES] for h in range(nh)], axis=0)
    lane = _iota((nq, LANES), 1)
    pos = q0 + _iota((nq, LANES), 0)
    lane4 = _iota((rows, LANES), 1)
    qrow4 = _iota((rows, LANES), 0) % nq
    new_ok = _new_page_mask(rows, nq)

    cmp_ok = (CMP_STRIDE * lane4 + (CMP_LEN - 1)) <= (q0 + qrow4)
    p = _masked_probs(_nt(q4, kvc), cmp_ok)
    o_cmp = _nn(p.astype(BF16), kvc)
    p_sum = p[0:nq]
    for h in range(1, nh):
        p_sum = p_sum + p[h * nq:(h + 1) * nq]

    own = pos // SEL_BLOCK
    flag = _nsa_flags(p_sum, ovl, own, lane, nsb)
    flag4 = jnp.concatenate([flag] * nh, axis=0)
    blk_per_page = PAGE_SIZE // SEL_BLOCK
    assert blk_per_page == 2

    ckvn = _pad_page(ckvn_rows).astype(BF16)
    kn = ckvn[:, LANES:2 * LANES]
    own_blk = n_pages * blk_per_page
    s_new = jnp.where(new_ok & (flag4[:, own_blk:own_blk + 1] > 0.5), _nt(q4, kn), NEG_INF)
    tiles = [pages[j][...].astype(BF16) for j in range(n_pages)]
    scores = []
    for j in range(n_pages):
        ok = jnp.where(lane4 < SEL_BLOCK, flag4[:, 2 * j:2 * j + 1], flag4[:, 2 * j + 1:2 * j + 2]) > 0.5
        scores.append(jnp.where(ok, _nn(q4, tiles[j]), NEG_INF))
    m = jnp.max(s_new, axis=1, keepdims=True)
    for s in scores:
        m = jnp.maximum(m, jnp.max(s, axis=1, keepdims=True))
    e = jnp.exp(s_new - m)
    d = jnp.sum(e, axis=1, keepdims=True)
    acc = _nn(e.astype(BF16), kn)
    for j in range(n_pages):
        e = jnp.exp(scores[j] - m)
        d = d + jnp.sum(e, axis=1, keepdims=True)
        acc = acc + _nt(e.astype(BF16), tiles[j])
    o_sel = acc / d

    wn = _pad_page(winn_rows).astype(BF16)
    s_new = jnp.where(new_ok, _nt(q4, wn), NEG_INF)
    st = st32.astype(BF16)
    key = _iota((rows, WINDOW), 1)
    ok = key >= (_iota((rows, WINDOW), 0) % nq)
    s_old = jnp.where(ok, _nn(q4, st), NEG_INF)
    m = jnp.maximum(jnp.max(s_new, axis=1, keepdims=True), jnp.max(s_old, axis=1, keepdims=True))
    e_new = jnp.exp(s_new - m)
    e_old = jnp.exp(s_old - m)
    d = jnp.sum(e_new, axis=1, keepdims=True) + jnp.sum(e_old, axis=1, keepdims=True)
    o_win = (_nn(e_new.astype(BF16), wn) + _nt(e_old.astype(BF16), st)) / d

    heads = []
    for h in range(nh):
        r = slice(h * nq, (h + 1) * nq)
        heads.append(g[:, 3 * h:3 * h + 1] * o_cmp[r] + g[:, 3 * h + 1:3 * h + 2] * o_sel[r]
                     + g[:, 3 * h + 2:3 * h + 3] * o_win[r])
    for pair in range(nh // 2):
        both = jnp.where(lane < HEAD_DIM, pltpu.roll(heads[2 * pair], HEAD_DIM, 1), heads[2 * pair + 1])
        o_ref[out_rows, pair * LANES:(pair + 1) * LANES] = both.astype(BF16)


def _dec_c(qz16, gates, ckvn32, winn32, kvcmp16, state_t, cache_ct, page_table, l, nq, q0):
    n, n_pages = page_table.shape
    assert state_t.shape[3] == WINDOW and q0 >= WINDOW
    ovl, nsb = _overlap_matrix(q0 + nq)
    group = 1
    row = lambda w: pl.BlockSpec((group * nq, w), lambda b, pt: (b, 0))
    grid_spec = pltpu.PrefetchScalarGridSpec(
        num_scalar_prefetch=1, grid=(n // group,),
        in_specs=[row(2 * C_W), row(LANES), row(4 * HEAD_DIM), row(2 * HEAD_DIM),
                  pl.BlockSpec((group * _CMP_ROWS, LANES), lambda b, pt: (b, 0)),
                  pl.BlockSpec((_CMP_ROWS, LANES), lambda b, pt: (0, 0)),
                  pl.BlockSpec((None, group, 2 * HEAD_DIM, WINDOW), lambda b, pt: (l, b, 0, 0))]
                 + _page_specs(l, n_pages, 2 * HEAD_DIM, row_block=1, group=group),
        out_specs=row(C_W))
    return pl.pallas_call(
        functools.partial(_dec_c_body, nsb=nsb, q0=q0, group=group),
        out_shape=jax.ShapeDtypeStruct((n * nq, C_W), BF16),
        grid_spec=grid_spec,
        compiler_params=_params(1),
        name="nsa_attn_decode",
    )(page_table, qz16, gates, ckvn32, winn32, kvcmp16, ovl, state_t, *([cache_ct] * (n_pages * group)))


def _rope_tables(n_pos):
    inv = ROPE_THETA ** (-jnp.arange(0, HEAD_DIM, 2, dtype=F32) / HEAD_DIM)
    ang = jnp.arange(n_pos, dtype=F32)[:, None] * inv[None, :]
    cos, sin = jnp.cos(ang), jnp.sin(ang)
    cos128 = jnp.concatenate([cos, cos, cos, cos], axis=-1)
    sin128 = jnp.concatenate([-sin, sin, -sin, sin], axis=-1)
    return cos128, sin128, cos.T, sin.T


def kernel(x_prompt, x_sample, cache_a_k, cache_a_v, cache_b_kv, cache_c_kv, state_c_win, page_table,
           w_in, w_out, g_mix, g_ffn, w_ffn_gate, w_ffn_up, w_ffn_down, diff_lambda, g_diff_head,
           w_cmp1, w_cmp2, cmp_pos, g_final):
    n_p, s_p, _ = x_prompt.shape
    n_s, s_s, _ = x_sample.shape
    n_pages = page_table.shape[1]
    past_len = n_pages * cache_a_k.shape[2]
    n_phys = cache_a_k.shape[1]
    assert cache_a_k.shape[2] == PAGE_SIZE and s_p % TM_PROJ == 0 and TM_PROJ % s_s == 0

    cos, sin, cos_t, sin_t = _rope_tables(past_len + s_s)
    tabs_p = (cos[:s_p], sin[:s_p], cos_t[:, :s_p], sin_t[:, :s_p])
    reps = TM_PROJ // s_s
    cos_s = jnp.tile(cos[past_len:past_len + s_s], (reps, 1))
    sin_s = jnp.tile(sin[past_len:past_len + s_s], (reps, 1))

    ckt = cache_a_k.transpose(0, 1, 3, 4, 5, 2).reshape(DEPTH, n_phys, A_W, PAGE_SIZE)
    cv4 = cache_a_v.reshape(DEPTH, n_phys, PAGE_SIZE * HEADS_A, 2 * HEAD_DIM)
    cbt = cache_b_kv.transpose(0, 1, 3, 4, 5, 2).reshape(DEPTH, n_phys, 2 * B_W, PAGE_SIZE)
    cct = cache_c_kv.transpose(0, 1, 3, 4, 2).reshape(DEPTH, n_phys, 4 * HEAD_DIM, PAGE_SIZE)
    stt = state_c_win.transpose(0, 1, 3, 4, 2).reshape(DEPTH, n_s, 2 * HEAD_DIM, WINDOW)

    hp = x_prompt.reshape(n_p * s_p, D_MODEL)
    hs = x_sample.reshape(n_s * s_s, D_MODEL)
    ent_p, ent_s = [], []
    for l in range(DEPTH):
        wg = w_ffn_gate[l].astype(BF16)
        wu = w_ffn_up[l].astype(BF16)
        wd = w_ffn_down[l].astype(BF16)
        w_main = w_in[l][:, :MAIN_W].astype(BF16)
        w_t = jnp.concatenate([w_in[l][:, _O_KA:_O_KA + A_W], w_in[l][:, _O_KB:_O_KB + 2 * B_W],
                               w_in[l][:, _O_KVC:_O_KVC + KVC_W]], axis=1).T.astype(BF16)
        w_gate = jnp.pad(w_in[l][:, MAIN_W:], ((0, 0), (0, LANES - GATE_W))).astype(BF16)
        wo = w_out[l].astype(BF16)
        cw = _compress_weights(w_cmp1[l], w_cmp2[l], cmp_pos[l])
        last = l == DEPTH - 1

        hp = _ffn(hp, g_ffn[l, 0], wg[0], wu[0], wd[0])
        (kat, va, bkvt, ckvt, wint, qa16, kat16, va16, qb16, qb32, bkvt16, qcz16, selt16, wint16, gates) = \
            _proj_prompt(hp, g_mix[l], w_main, w_t, w_gate, tabs_p, n_p, s_p)
        oa = _attn_a_prompt(qa16, kat16, va16, diff_lambda[l], g_diff_head[l], n_p, s_p, l)
        ob = _attn_b_prompt(qb16, qb32, bkvt, bkvt16, n_p, s_p)
        kvcmp = _compress_prompt(ckvt, cw, n_p, s_p)
        oc = _attn_c_prompt(qcz16, gates, selt16, wint16, kvcmp, n_p, s_p)
        hp = _outproj(hp, oa, ob, oc, wo)
        hp = _ffn(hp, g_ffn[l, 1], wg[1], wu[1], wd[1], g_final if last else None)
        win_keep = min(WINDOW, s_p)
        ent_p.append((kat.reshape(n_p, HEADS_A, 2, HEAD_DIM, s_p).transpose(0, 4, 1, 2, 3),
                      va.reshape(n_p, s_p, HEADS_A, 2 * HEAD_DIM),
                      bkvt.reshape(n_p, 2, HEADS_B, HEAD_DIM, s_p).transpose(0, 4, 1, 2, 3),
                      ckvt.reshape(n_p, 4, HEAD_DIM, s_p).transpose(0, 3, 1, 2),
                      wint[:, :, s_p - win_keep:].reshape(n_p, 2, HEAD_DIM, win_keep).transpose(0, 3, 1, 2)))

        hs = _ffn(hs, g_ffn[l, 0], wg[0], wu[0], wd[0])
        (ka, va, bkv, ckv, win, qa16, qb16, qb32, qcz16, gates) = _proj_sample(
            hs, g_mix[l], w_main, w_gate, cos_s, sin_s)
        oa = _dec_a(qa16, ka, va, diff_lambda[l], g_diff_head[l], ckt, cv4, page_table, l, s_s)
        ob = _dec_b(qb16, qb32, bkv, cbt, page_table, l, s_s)
        kvcmp = _compress_pages(cct, page_table, cw, l)
        oc = _dec_c(qcz16, gates, ckv, win, kvcmp, stt, cct, page_table, l, s_s, past_len)
        hs = _outproj(hs, oa, ob, oc, wo)
        hs = _ffn(hs, g_ffn[l, 1], wg[1], wu[1], wd[1], g_final if last else None)
        win_new = win.reshape(n_s, s_s, 2, HEAD_DIM)
        win_all = jnp.concatenate([state_c_win[l], win_new], axis=1)
        ent_s.append((ka.reshape(n_s, s_s, HEADS_A, 2, HEAD_DIM), va.reshape(n_s, s_s, HEADS_A, 2 * HEAD_DIM),
                      bkv.reshape(n_s, s_s, 2, HEADS_B, HEAD_DIM), ckv.reshape(n_s, s_s, 4, HEAD_DIM),
                      win_all[:, win_all.shape[1] - min(WINDOW, win_all.shape[1]):]))

    st_ = lambda ents, i: jnp.stack([e[i] for e in ents], axis=0)
    return (hp.reshape(n_p, s_p, D_MODEL), hs.reshape(n_s, s_s, D_MODEL),
            st_(ent_p, 0), st_(ent_s, 0), st_(ent_p, 1), st_(ent_s, 1),
            st_(ent_p, 2), st_(ent_s, 2), st_(ent_p, 3), st_(ent_s, 3),
            st_(ent_p, 4), st_(ent_s, 4))
```

```python
import functools
import math

import numpy as np
import jax
import jax.numpy as jnp
from jax import lax
from jax.experimental import pallas as pl
from jax.experimental.pallas import tpu as pltpu

F32 = jnp.float32
BF16 = jnp.bfloat16

D_MODEL = 1024
DEPTH = 2
HEAD_DIM = 64
HEADS_A = 4
HEADS_B = 4
HEADS_C = 4
D_FF = 2816
ROPE_THETA = 10000.0
MOBA_BLOCK = 256
MOBA_TOPK = 3
CMP_LEN = 32
CMP_STRIDE = 16
CMP_HIDDEN = 4 * HEAD_DIM
SEL_BLOCK = 64
SEL_TOPK = 16
WINDOW = 512
RMS_EPS = 1e-6
PAGE_SIZE = 128

A_W = HEADS_A * 2 * HEAD_DIM
B_W = HEADS_B * HEAD_DIM
C_W = HEADS_C * HEAD_DIM
KVC_W = 6 * HEAD_DIM
GATE_W = 3 * HEADS_C
MAIN_W = 3 * A_W + 3 * B_W + C_W + KVC_W
LANES = 128
QK_SCALE = HEAD_DIM ** -0.5
NEG_INF = float("-inf")
VMEM_LIMIT = 56 * 1024 * 1024

TM_FFN = 512
TF_FFN = 256
TM_PROJ = 512
TQ = 256
CMP_GROUP = 4
DEC_GROUP = 2
A_HEADS_PER_LOOP = 4

_O_QA, _O_KA, _O_VA = 0, A_W, 2 * A_W
_O_QB = 3 * A_W
_O_KB, _O_VB = _O_QB + B_W, _O_QB + 2 * B_W
_O_QC = _O_QB + 3 * B_W
_O_KVC = _O_QC + C_W


def _nn(a, b):
    return jnp.dot(a, b, preferred_element_type=F32)


def _nt(a, b):
    return lax.dot_general(a, b, (((1,), (1,)), ((), ())), preferred_element_type=F32)


def _split3(x):
    hi = x.astype(BF16)
    r1 = x - hi.astype(F32)
    mid = r1.astype(BF16)
    lo = (r1 - mid.astype(F32)).astype(BF16)
    return hi, mid, lo


def _nn_precise(a, b):
    a_hi, a_mid, _ = _split3(a)
    b_hi, b_mid, _ = _split3(b)
    return _nn(a_hi, b_hi) + (_nn(a_hi, b_mid) + _nn(a_mid, b_hi))


def _rms(x):
    return x * lax.rsqrt(jnp.mean(x * x, axis=-1, keepdims=True) + RMS_EPS)


def _iota(shape, dim):
    return lax.broadcasted_iota(jnp.int32, shape, dim)


def _params(n_axes):
    return pltpu.CompilerParams(dimension_semantics=("arbitrary",) * n_axes,
                                vmem_limit_bytes=VMEM_LIMIT)


def _lam_value(lam_ref, lam_init):
    lp = lam_ref[...]
    a = jnp.sum(lp[0:1] * lp[1:2], axis=1, keepdims=True)
    b = jnp.sum(lp[2:3] * lp[3:4], axis=1, keepdims=True)
    return jnp.exp(a) - jnp.exp(b) + lam_init


def _lam_init(lidx):
    return 0.8 - 0.6 * math.exp(-0.3 * lidx)


def _ffn_body(x_ref, g_ref, wg_ref, wu_ref, wd_ref, *rest, final):
    if final:
        gf_ref, o_ref = rest
    else:
        (o_ref,) = rest
    x = x_ref[...]
    xn = (_rms(x) * g_ref[...]).astype(BF16)
    acc = jnp.zeros_like(x)
    for f in range(D_FF // TF_FFN):
        sl = slice(f * TF_FFN, (f + 1) * TF_FFN)
        g = _nn(xn, wg_ref[:, sl])
        u = _nn(xn, wu_ref[:, sl])
        a = (g * jax.nn.sigmoid(g) * u).astype(BF16)
        acc = acc + _nn(a, wd_ref[sl, :])
    y = x + 0.5 * acc
    if final:
        y = _rms(y) * gf_ref[...]
    o_ref[...] = y


def _ffn(x, g, wg, wu, wd, l, k, g_final=None):
    m = x.shape[0]
    row = pl.BlockSpec((TM_FFN, D_MODEL), lambda i: (i, 0))
    vec = pl.BlockSpec((1, D_MODEL), lambda i: (0, 0))
    full = lambda r, c: pl.BlockSpec((None, None, r, c), lambda i: (l, k, 0, 0))
    in_specs = [row, vec, full(D_MODEL, D_FF), full(D_MODEL, D_FF), full(D_FF, D_MODEL)]
    args = [x, g.reshape(1, D_MODEL), wg, wu, wd]
    if g_final is not None:
        in_specs.append(vec)
        args.append(g_final.reshape(1, D_MODEL))
    return pl.pallas_call(
        functools.partial(_ffn_body, final=g_final is not None),
        out_shape=jax.ShapeDtypeStruct((m, D_MODEL), F32),
        grid=(m // TM_FFN,),
        in_specs=in_specs,
        out_specs=row,
        compiler_params=_params(1),
        name="ffn_half",
    )(*args)


def _rope_rows(lane_shape):
    lane = _iota(lane_shape, 1)
    return (lane % HEAD_DIM) < (HEAD_DIM // 2), lane < HEAD_DIM


def _rope_lanes(x, cos, sin, lo32):
    sh = jnp.where(lo32, pltpu.roll(x, LANES - HEAD_DIM // 2, 1), pltpu.roll(x, HEAD_DIM // 2, 1))
    return x * cos + sh * sin


def _proj_queries(u, w_ref, cos, sin, lo32, lo64, qa16_ref, qb16_ref, qb32_ref, qcz16_ref):
    p = _nn(u, w_ref[:, _O_QA:_O_QA + A_W])
    for k in range(A_W // LANES):
        qa16_ref[:, k * LANES:(k + 1) * LANES] = (
            _rope_lanes(p[:, k * LANES:(k + 1) * LANES], cos, sin, lo32) * QK_SCALE).astype(BF16)
    p = _nn(u, w_ref[:, _O_QB:_O_QB + B_W])
    for k in range(B_W // LANES):
        r = _rope_lanes(p[:, k * LANES:(k + 1) * LANES], cos, sin, lo32) * QK_SCALE
        qb32_ref[:, k * LANES:(k + 1) * LANES] = r
        qb16_ref[:, k * LANES:(k + 1) * LANES] = r.astype(BF16)
    p = _nn(u, w_ref[:, _O_QC:_O_QC + C_W])
    for k in range(C_W // LANES):
        r = _rope_lanes(p[:, k * LANES:(k + 1) * LANES], cos, sin, lo32) * QK_SCALE
        even = jnp.where(lo64, r, 0.0)
        odd = jnp.where(lo64, pltpu.roll(r, HEAD_DIM, 1), 0.0)
        qcz16_ref[:, (2 * k) * LANES:(2 * k + 1) * LANES] = even.astype(BF16)
        qcz16_ref[:, (2 * k + 1) * LANES:(2 * k + 2) * LANES] = odd.astype(BF16)


def _proj_sample_body(h_ref, g_ref, w_ref, wgate_ref, cos_ref, sin_ref,
                      ka_ref, va_ref, bkv_ref, ckv_ref, win_ref,
                      qa16_ref, qb16_ref, qb32_ref, qcz16_ref, gate_ref):
    tm = h_ref.shape[0]
    u = (_rms(h_ref[...]) * g_ref[...]).astype(BF16)
    cos = cos_ref[...]
    sin = sin_ref[...]
    lo32, lo64 = _rope_rows((tm, LANES))
    _proj_queries(u, w_ref, cos, sin, lo32, lo64, qa16_ref, qb16_ref, qb32_ref, qcz16_ref)
    p = _nn(u, w_ref[:, _O_KA:_O_KA + A_W])
    for k in range(A_W // LANES):
        ka_ref[:, k * LANES:(k + 1) * LANES] = _rope_lanes(p[:, k * LANES:(k + 1) * LANES], cos, sin, lo32)
    va_ref[...] = _nn(u, w_ref[:, _O_VA:_O_VA + A_W])
    p = _nn(u, w_ref[:, _O_KB:_O_KB + B_W])
    for k in range(B_W // LANES):
        bkv_ref[:, k * LANES:(k + 1) * LANES] = _rope_lanes(p[:, k * LANES:(k + 1) * LANES], cos, sin, lo32)
    bkv_ref[:, B_W:2 * B_W] = _nn(u, w_ref[:, _O_VB:_O_VB + B_W])
    p = _nn(u, w_ref[:, _O_KVC:_O_KVC + KVC_W])
    for k in range(KVC_W // LANES):
        x = p[:, k * LANES:(k + 1) * LANES]
        r = jnp.where(lo64, _rope_lanes(x, cos, sin, lo32), x)
        if k < 2:
            ckv_ref[:, k * LANES:(k + 1) * LANES] = r
        else:
            win_ref[...] = r
    gate_ref[...] = jax.nn.sigmoid(_nn(u, wgate_ref[...]))


def _proj_sample(h, g_mix, w_main, w_gate, cos, sin):
    m = h.shape[0]
    tm = TM_PROJ
    row = lambda w: pl.BlockSpec((tm, w), lambda i: (i, 0))
    full = lambda shape: pl.BlockSpec(shape, lambda i: (0, 0))
    outs = ((A_W, F32), (A_W, F32), (2 * B_W, F32), (4 * HEAD_DIM, F32), (2 * HEAD_DIM, F32),
            (A_W, BF16), (B_W, BF16), (B_W, F32), (2 * C_W, BF16), (LANES, F32))
    return pl.pallas_call(
        _proj_sample_body,
        out_shape=[jax.ShapeDtypeStruct((m, w), dt) for w, dt in outs],
        grid=(m // tm,),
        in_specs=[row(D_MODEL), full((1, D_MODEL)), full((D_MODEL, MAIN_W)), full((D_MODEL, LANES)),
                  full((tm, LANES)), full((tm, LANES))],
        out_specs=[row(w) for w, _ in outs],
        compiler_params=_params(1),
        name="in_proj_rope_sample",
    )(h, g_mix.reshape(1, D_MODEL), w_main, w_gate, cos, sin)


_KT_ROWS = A_W + 2 * B_W + KVC_W


def _proj_prompt_body(h_ref, g_ref, w_ref, wt_ref, wgate_ref, cos_ref, sin_ref, cos_t_ref, sin_t_ref,
                      kat_ref, va_ref, bkvt_ref, ckvt_ref, wint_ref,
                      qa16_ref, kat16_ref, va16_ref, qb16_ref, qb32_ref, bkvt16_ref,
                      qcz16_ref, selt16_ref, wint16_ref, gate_ref):
    tm = h_ref.shape[0]
    half = HEAD_DIM // 2
    u = (_rms(h_ref[...]) * g_ref[...]).astype(BF16)
    lo32, lo64 = _rope_rows((tm, LANES))
    _proj_queries(u, w_ref, cos_ref[...], sin_ref[...], lo32, lo64, qa16_ref, qb16_ref, qb32_ref, qcz16_ref)

    p = _nn(u, w_ref[:, _O_VA:_O_VA + A_W])
    va16_ref[...] = p.astype(BF16)
    for h in range(HEADS_A):
        va_ref[pl.ds(h, tm, stride=HEADS_A), :] = p[:, h * LANES:(h + 1) * LANES]

    cos_t = cos_t_ref[...]
    sin_t = sin_t_ref[...]

    def rope_t(x):
        x1, x2 = x[0:half], x[half:HEAD_DIM]
        return jnp.concatenate([x1 * cos_t - x2 * sin_t, x2 * cos_t + x1 * sin_t], axis=0)

    def store_t(f32_ref, b16_ref, row0, val):
        rows = val.shape[0]
        if f32_ref is not None:
            f32_ref[row0:row0 + rows, :] = val
        if b16_ref is not None:
            for t in range(tm // TQ):
                b16_ref[t, row0:row0 + rows, :] = val[:, t * TQ:(t + 1) * TQ].astype(BF16)

    pt = _nt(wt_ref[0:A_W, :], u)
    for g in range(A_W // HEAD_DIM):
        store_t(kat_ref, kat16_ref, g * HEAD_DIM, rope_t(pt[g * HEAD_DIM:(g + 1) * HEAD_DIM]))
    pt = _nt(wt_ref[A_W:A_W + 2 * B_W, :], u)
    for g in range(B_W // HEAD_DIM):
        store_t(bkvt_ref, bkvt16_ref, g * HEAD_DIM, rope_t(pt[g * HEAD_DIM:(g + 1) * HEAD_DIM]))
    store_t(bkvt_ref, bkvt16_ref, B_W, pt[B_W:2 * B_W])
    pt = _nt(wt_ref[A_W + 2 * B_W:_KT_ROWS, :], u)
    for g in range(KVC_W // HEAD_DIM):
        x = pt[g * HEAD_DIM:(g + 1) * HEAD_DIM]
        if g % 2 == 0:
            x = rope_t(x)
        if g < 2:
            store_t(ckvt_ref, None, g * HEAD_DIM, x)
        elif g < 4:
            store_t(ckvt_ref, None, g * HEAD_DIM, x)
            store_t(None, selt16_ref, (g - 2) * HEAD_DIM, x)
        else:
            store_t(wint_ref, wint16_ref, (g - 4) * HEAD_DIM, x)
    gate_ref[...] = jax.nn.sigmoid(_nn(u, wgate_ref[...]))


def _proj_prompt(h, g_mix, w_main, w_t, w_gate, tabs, n, s):
    cos, sin, cos_t, sin_t = tabs
    m = n * s
    tm = TM_PROJ
    per_seq = s // tm
    nt = tm // TQ
    row = lambda w: pl.BlockSpec((tm, w), lambda i: (i, 0))
    full = lambda shape: pl.BlockSpec(shape, lambda i: (0,) * len(shape))
    tab = pl.BlockSpec((tm, LANES), lambda i: (i % per_seq, 0))
    tab_t = pl.BlockSpec((HEAD_DIM // 2, tm), lambda i: (0, i % per_seq))
    feat = lambda w: pl.BlockSpec((None, w, tm), lambda i: (i // per_seq, 0, i % per_seq))
    tiles = lambda w: pl.BlockSpec((None, nt, w, TQ), lambda i: (i // per_seq, i % per_seq, 0, 0))
    sds = jax.ShapeDtypeStruct
    out_shape = [sds((n, A_W, s), F32), sds((m * HEADS_A, LANES), F32), sds((n, 2 * B_W, s), F32),
                 sds((n, 4 * HEAD_DIM, s), F32), sds((n, 2 * HEAD_DIM, s), F32),
                 sds((m, A_W), BF16), sds((n, s // TQ, A_W, TQ), BF16), sds((m, A_W), BF16),
                 sds((m, B_W), BF16), sds((m, B_W), F32), sds((n, s // TQ, 2 * B_W, TQ), BF16),
                 sds((m, 2 * C_W), BF16), sds((n, s // TQ, 2 * HEAD_DIM, TQ), BF16),
                 sds((n, s // TQ, 2 * HEAD_DIM, TQ), BF16), sds((m, LANES), F32)]
    out_specs = [feat(A_W), pl.BlockSpec((tm * HEADS_A, LANES), lambda i: (i, 0)), feat(2 * B_W),
                 feat(4 * HEAD_DIM), feat(2 * HEAD_DIM),
                 row(A_W), tiles(A_W), row(A_W), row(B_W), row(B_W), tiles(2 * B_W),
                 row(2 * C_W), tiles(2 * HEAD_DIM), tiles(2 * HEAD_DIM), row(LANES)]
    return pl.pallas_call(
        _proj_prompt_body,
        out_shape=out_shape,
        grid=(m // tm,),
        in_specs=[row(D_MODEL), full((1, D_MODEL)), full((D_MODEL, MAIN_W)), full((_KT_ROWS, D_MODEL)),
                  full((D_MODEL, LANES)), tab, tab, tab_t, tab_t],
        out_specs=out_specs,
        compiler_params=_params(1),
        name="in_proj_rope_prompt",
    )(h, g_mix.reshape(1, D_MODEL), w_main, w_t, w_gate, cos, sin, cos_t, sin_t)


def _outproj_body(h_ref, oa_ref, ob_ref, oc_ref, w_ref, o_ref):
    y = h_ref[...] + _nn(oa_ref[...], w_ref[0:A_W, :])
    y = y + _nn(ob_ref[...], w_ref[A_W:A_W + B_W, :])
    y = y + _nn(oc_ref[...], w_ref[A_W + B_W:, :])
    o_ref[...] = y


def _outproj(h, oa, ob, oc, w_out):
    m = h.shape[0]
    tm = TM_PROJ
    row = lambda w: pl.BlockSpec((tm, w), lambda i: (i, 0))
    return pl.pallas_call(
        _outproj_body,
        out_shape=jax.ShapeDtypeStruct((m, D_MODEL), F32),
        grid=(m // tm,),
        in_specs=[row(D_MODEL), row(A_W), row(B_W), row(C_W),
                  pl.BlockSpec((D_MODEL, D_MODEL), lambda i: (0, 0))],
        out_specs=row(D_MODEL),
        compiler_params=_params(1),
        name="out_proj",
    )(h, oa, ob, oc, w_out)


def _softmax_start(s):
    m = jnp.max(s, axis=-1, keepdims=True)
    p = jnp.exp(s - m)
    return m, jnp.sum(p, axis=-1, keepdims=True), p


def _softmax_step(s, m, l):
    m_new = jnp.maximum(m, jnp.max(s, axis=-1, keepdims=True))
    alpha = jnp.exp(m - m_new)
    p = jnp.exp(s - m_new)
    return m_new, alpha, alpha * l + jnp.sum(p, axis=-1, keepdims=True), p


def _rank_lower(x, n, width_iota):
    rank = jnp.zeros(x.shape, F32)
    for bp in range(n):
        col = x[:, bp:bp + 1]
        tie = jnp.where(bp < width_iota, 1.0, 0.0)
        rank = rank + jnp.where(col > x, 1.0, jnp.where(col == x, tie, 0.0))
    return rank


def _rank_lower_t(x, n, row_iota):
    rank = jnp.zeros(x.shape, F32)
    for bp in range(n):
        row = x[bp:bp + 1, :]
        tie = jnp.where(bp < row_iota, 1.0, 0.0)
        rank = rank + jnp.where(row > x, 1.0, jnp.where(row == x, tie, 0.0))
    return rank


def _head_rms_scale(o, gh, lam_init):
    return _rms(o) * gh * (1.0 - lam_init)


def _attn_a_prompt_body(q_ref, kt_ref, v_ref, lam_ref, gh_ref, o_ref, *, lam_init):
    i = pl.program_id(1)
    tq = q_ref.shape[0]
    lam = _lam_value(lam_ref, lam_init)
    gh = gh_ref[...]
    lane = _iota((tq, LANES), 1)
    causal = _iota((tq, tq), 1) <= _iota((tq, tq), 0)
    head_sl = [slice(h * LANES, (h + 1) * LANES) for h in range(HEADS_A)]
    chains = [(h, c) for h in range(HEADS_A) for c in range(2)]
    qms = []
    for h, c in chains:
        q2 = q_ref[:, head_sl[h]]
        in_c = (lane >= c * HEAD_DIM) & (lane < (c + 1) * HEAD_DIM)
        qms.append(jnp.where(in_c, q2, jnp.zeros_like(q2)))

    def rows(j):
        return pl.ds(pl.multiple_of(j * tq, tq), tq)

    ones = jnp.ones((tq, LANES), BF16)

    def v_ext(j, h):
        return jnp.concatenate([v_ref[rows(j), head_sl[h]], ones], axis=1)

    for h0 in range(0, HEADS_A, A_HEADS_PER_LOOP):
        group = [(2 * h + c, h) for h in range(h0, h0 + A_HEADS_PER_LOOP) for c in range(2)]
        state = []
        for idx, h in group:
            s = jnp.where(causal, _nn(qms[idx], kt_ref[i, head_sl[h], :]), NEG_INF)
            m = jnp.max(s, axis=-1, keepdims=True)
            p = jnp.exp((s - m).astype(BF16))
            state += [m, _nn(p, v_ext(i, h))]

        def body(j, carry, group=group):
            out = []
            for k, (idx, h) in enumerate(group):
                m, acc = carry[2 * k:2 * k + 2]
                s = _nn(qms[idx], kt_ref[j, head_sl[h], :])
                m_new = jnp.maximum(m, jnp.max(s, axis=-1, keepdims=True))
                p = jnp.exp((s - m_new).astype(BF16))
                out += [m_new, jnp.exp(m - m_new) * acc + _nn(p, v_ext(j, h))]
            return tuple(out)

        state = lax.fori_loop(0, i, body, tuple(state))
        for k in range(A_HEADS_PER_LOOP):
            a0, a1 = state[4 * k + 1], state[4 * k + 3]
            o0 = a0[:, :LANES] / a0[:, LANES:]
            o1 = a1[:, :LANES] / a1[:, LANES:]
            o_ref[:, head_sl[h0 + k]] = _head_rms_scale(o0 - lam * o1, gh, lam_init).astype(BF16)


def _attn_a_prompt(q16, kt16, v16, lam_p, g_head, n, s, lidx):
    nq = s // TQ
    qspec = pl.BlockSpec((TQ, A_W), lambda b, i: (b * nq + i, 0))
    return pl.pallas_call(
        functools.partial(_attn_a_prompt_body, lam_init=_lam_init(lidx)),
        out_shape=jax.ShapeDtypeStruct((n * s, A_W), BF16),
        grid=(n, nq),
        in_specs=[qspec,
                  pl.BlockSpec((None, nq, A_W, TQ), lambda b, i: (b, 0, 0, 0)),
                  pl.BlockSpec((s, A_W), lambda b, i: (b, 0)),
                  pl.BlockSpec((4, HEAD_DIM), lambda b, i: (0, 0)),
                  pl.BlockSpec((1, 2 * HEAD_DIM), lambda b, i: (0, 0))],
        out_specs=qspec,
        compiler_params=_params(2),
        name="diff_attn_prompt",
    )(q16, kt16, v16, lam_p, g_head.reshape(1, 2 * HEAD_DIM))


def _block_means_t(blocks):
    feats = blocks[0].shape[0]
    lane = _iota((feats, LANES), 1)
    out = jnp.zeros((feats, LANES), F32)
    for b, blk in enumerate(blocks):
        out = jnp.where(lane == b, jnp.sum(blk, axis=1, keepdims=True) * (1.0 / MOBA_BLOCK), out)
    return out


def _attn_b_prompt_body(q_ref, q32_ref, k32t_ref, kvt_ref, o_ref, kmean_ref):
    i = pl.program_id(1)
    tq = q_ref.shape[0]
    nb = k32t_ref.shape[1] // MOBA_BLOCK

    @pl.when(i == 0)
    def _():
        kmean_ref[...] = _block_means_t([k32t_ref[:, b * MOBA_BLOCK:(b + 1) * MOBA_BLOCK] for b in range(nb)])

    lane = _iota((tq, LANES), 1)
    causal = _iota((tq, tq), 1) <= _iota((tq, tq), 0)
    nb_rows = -(-nb // 8) * 8
    blk_t = _iota((nb_rows, tq), 0)
    past_t = blk_t < i
    ksl = [slice((h // 2) * LANES, (h // 2 + 1) * LANES) for h in range(HEADS_B)]
    vsl = [slice(B_W + (h // 2) * LANES, B_W + (h // 2 + 1) * LANES) for h in range(HEADS_B)]
    ones_t = jnp.ones((HEAD_DIM, tq), BF16)

    def v_ones(j, h):
        vt = kvt_ref[j, vsl[h], :]
        if h % 2 == 0:
            return jnp.concatenate([vt[0:HEAD_DIM], ones_t], axis=0)
        return jnp.concatenate([ones_t, vt[HEAD_DIM:2 * HEAD_DIM]], axis=0)

    qms, chosen, state = [], [], []
    for h in range(HEADS_B):
        q2 = q_ref[:, ksl[h]]
        in_h = (lane >= (h % 2) * HEAD_DIM) & (lane < (h % 2 + 1) * HEAD_DIM)
        qms.append(jnp.where(in_h, q2, jnp.zeros_like(q2)))
        gate = _nn_precise(jnp.where(in_h, q32_ref[:, ksl[h]], 0.0), kmean_ref[ksl[h], :])
        gate_t = jnp.where(past_t, gate.T[0:nb_rows], NEG_INF)
        pick_t = jnp.where((_rank_lower_t(gate_t, nb, blk_t) < MOBA_TOPK) & past_t, 1.0, 0.0)
        chosen.append(jnp.concatenate([pick_t, jnp.zeros((LANES - nb_rows, tq), F32)], axis=0).T)
        s = jnp.where(causal, _nn(qms[h], kvt_ref[i, ksl[h], :]), NEG_INF)
        m = jnp.max(s, axis=-1, keepdims=True)
        state += [m, _nt(jnp.exp((s - m).astype(BF16)), v_ones(i, h))]

    def body(j, carry):
        out = []
        for h in range(HEADS_B):
            m, acc = carry[2 * h:2 * h + 2]
            use = jnp.max(jnp.where(lane == j, chosen[h], 0.0), axis=1, keepdims=True)
            s = jnp.where(use > 0.5, _nn(qms[h], kvt_ref[j, ksl[h], :]), NEG_INF)
            m_new = jnp.maximum(m, jnp.max(s, axis=-1, keepdims=True))
            p = jnp.exp((s - m_new).astype(BF16))
            out += [m_new, jnp.exp(m - m_new) * acc + _nt(p, v_ones(j, h))]
        return tuple(out)

    state = lax.fori_loop(0, i, body, tuple(state))
    for pair in range(HEADS_B // 2):
        a0, a1 = state[4 * pair + 1], state[4 * pair + 3]
        o0 = a0 / pltpu.roll(a0, HEAD_DIM, 1)
        o1 = a1 / pltpu.roll(a1, HEAD_DIM, 1)
        o_ref[:, ksl[2 * pair]] = jnp.where(lane < HEAD_DIM, o0, o1).astype(BF16)


def _attn_b_prompt(q16, q32, bkvt32, bkvt16, n, s):
    nq = s // TQ
    assert TQ == MOBA_BLOCK and s // MOBA_BLOCK <= LANES
    qspec = pl.BlockSpec((TQ, B_W), lambda b, i: (b * nq + i, 0))
    return pl.pallas_call(
        _attn_b_prompt_body,
        out_shape=jax.ShapeDtypeStruct((n * s, B_W), BF16),
        grid=(n, nq),
        in_specs=[qspec, qspec,
                  pl.BlockSpec((None, B_W, s), lambda b, i: (b, 0, 0)),
                  pl.BlockSpec((None, nq, 2 * B_W, TQ), lambda b, i: (b, 0, 0, 0))],
        out_specs=qspec,
        scratch_shapes=[pltpu.VMEM((B_W, LANES), F32)],
        compiler_params=_params(2),
        name="moba_attn_prompt",
    )(q16, q32, bkvt32, bkvt16)


_CMP_ROWS = 128


def _compress_core(xs_ref, pe_ref, w1_ref, w2_ref, o_ref):
    half = CMP_LEN // 2
    n_chunk = xs_ref.shape[0] // CMP_STRIDE
    acc = [None, None]
    def rows(r):
        return (xs_ref[pl.ds(r % half, n_chunk, stride=CMP_STRIDE), :] + pe_ref[r:r + 1, :]).astype(BF16)

    for r in range(0, CMP_LEN, 2):
        t = _nn(jnp.concatenate([rows(r), rows(r + 1)], axis=1), w1_ref[r // 2])
        acc[r // half] = t if acc[r // half] is None else acc[r // half] + t
    pre = acc[0] + pltpu.roll(acc[1], n_chunk - 1, 0)
    hid = jax.nn.gelu(pre).astype(BF16)
    o_ref[...] = _nn(hid, w2_ref[...]).astype(BF16)


def _compress_prompt_body(xt_ref, pe_ref, w1_ref, w2_ref, o_ref, xs_ref):
    for j in range(xt_ref.shape[1] // LANES):
        xs_ref[j * LANES:(j + 1) * LANES, :] = xt_ref[:, j * LANES:(j + 1) * LANES].T
    _compress_core(xs_ref, pe_ref, w1_ref, w2_ref, o_ref)


def _compress_pages_body(pt_ref, pe_ref, w1_ref, w2_ref, *refs):
    pages, o_ref, xs_ref = refs[:-2], refs[-2], refs[-1]
    for j, pg in enumerate(pages):
        xs_ref[j * PAGE_SIZE:(j + 1) * PAGE_SIZE, :] = pg[...].T
    _compress_core(xs_ref, pe_ref, w1_ref, w2_ref, o_ref)


def _compress_weights(w_cmp1, w_cmp2, cmp_pos):
    w1 = w_cmp1.reshape(2, CMP_LEN, HEAD_DIM, CMP_HIDDEN)
    z1 = jnp.zeros((CMP_LEN, HEAD_DIM, CMP_HIDDEN), F32)
    top = jnp.concatenate([w1[0], z1], axis=2)
    bot = jnp.concatenate([z1, w1[1]], axis=2)
    w1c = jnp.concatenate([top, bot], axis=1).astype(BF16)
    w1c = w1c.reshape(CMP_LEN // 2, 2 * LANES, 2 * CMP_HIDDEN)
    z2 = jnp.zeros((CMP_HIDDEN, HEAD_DIM), F32)
    w2c = jnp.concatenate([jnp.concatenate([w_cmp2[0], z2], axis=1),
                           jnp.concatenate([z2, w_cmp2[1]], axis=1)], axis=0).astype(BF16)
    pe = jnp.concatenate([cmp_pos[0], cmp_pos[1]], axis=1)
    return pe, w1c, w2c


def _const_specs(index):
    return [pl.BlockSpec((CMP_LEN, LANES), index(2)),
            pl.BlockSpec((CMP_LEN // 2, 2 * LANES, 2 * CMP_HIDDEN), index(3)),
            pl.BlockSpec((2 * CMP_HIDDEN, LANES), index(2))]


def _compress_prompt(ckvt32, cw, n, s):
    assert s // CMP_STRIDE == _CMP_ROWS
    zero = lambda nd: (lambda b: (0,) * nd)
    return pl.pallas_call(
        _compress_prompt_body,
        out_shape=jax.ShapeDtypeStruct((n * _CMP_ROWS, LANES), BF16),
        grid=(n,),
        in_specs=[pl.BlockSpec((None, LANES, s), lambda b: (b, 0, 0))] + _const_specs(zero),
        out_specs=pl.BlockSpec((_CMP_ROWS, LANES), lambda b: (b, 0)),
        scratch_shapes=[pltpu.VMEM((s, LANES), F32)],
        compiler_params=_params(1),
        name="nsa_compress_prompt",
    )(ckvt32, *cw)


def _page_index(l, g, group, j, row_block, b, pt):
    return (l, pt[b * group + g, j], row_block, 0)


def _page_specs(l, n_pages, rows, row_block=0, group=1):
    return [pl.BlockSpec((None, None, rows, PAGE_SIZE), functools.partial(_page_index, l, g, group, j, row_block))
            for g in range(group) for j in range(n_pages)]


def _compress_pages(cache_ct, page_table, cw, l):
    n, n_pages = page_table.shape
    group = CMP_GROUP
    assert n_pages * PAGE_SIZE // CMP_STRIDE == _CMP_ROWS and n % group == 0
    zero = lambda nd: (lambda b, pt: (0,) * nd)
    grid_spec = pltpu.PrefetchScalarGridSpec(
        num_scalar_prefetch=1, grid=(n // group,),
        in_specs=_const_specs(zero) + _page_specs(l, n_pages, LANES, 0, group),
        out_specs=pl.BlockSpec((group * _CMP_ROWS, LANES), lambda b, pt: (b, 0)),
        scratch_shapes=[pltpu.VMEM((group * n_pages * PAGE_SIZE, LANES), F32)])
    return pl.pallas_call(
        _compress_pages_body,
        out_shape=jax.ShapeDtypeStruct((n * _CMP_ROWS, LANES), BF16),
        grid_spec=grid_spec,
        compiler_params=_params(1),
        name="nsa_compress_pages",
    )(page_table, *cw, *([cache_ct] * (n_pages * group)))


def _overlap_matrix(t_len):
    n_cmp = (t_len - CMP_LEN) // CMP_STRIDE + 1
    nsb = -(-t_len // SEL_BLOCK)
    starts = np.arange(n_cmp) * CMP_STRIDE
    sb = np.arange(nsb) * SEL_BLOCK
    ov = np.clip(np.minimum(starts[:, None] + CMP_LEN, sb[None, :] + SEL_BLOCK)
                 - np.maximum(starts[:, None], sb[None, :]), 0, None) / CMP_STRIDE
    out = np.zeros((_CMP_ROWS, LANES), np.float32)
    out[:n_cmp, :nsb] = ov
    return jnp.asarray(out, BF16), nsb


def _expand_matrix(s):
    nt = s // TQ
    e = np.zeros((nt, LANES, TQ), np.float32)
    for j in range(nt):
        for k in range(TQ):
            e[j, (j * TQ + k) // SEL_BLOCK, k] = 1.0
    return jnp.asarray(e, BF16)


def _nsa_flags(p_sum, ovl, own, lane, nsb):
    imp = None
    for part in _split3(p_sum):
        t = _nn(part, ovl)
        imp = t if imp is None else imp + t
    forced = (lane == 0) | (lane == own) | (lane == own - 1)
    imp = jnp.where(lane > own, NEG_INF, jnp.where(forced, jnp.inf, imp))
    rank = _rank_lower(imp, nsb, lane)
    return jnp.where((rank < SEL_TOPK) & (lane <= own), 1.0, 0.0)


def _nsa_flags_t(p_sum, ovl, pos0, nsb):
    tq = p_sum.shape[0]
    imp = None
    for part in _split3(p_sum):
        t = _nn(part, ovl)
        imp = t if imp is None else imp + t
    rows = -(-nsb // 8) * 8
    imp_t = imp.T[0:rows]
    blk = _iota((rows, tq), 0)
    own = (pos0 + _iota((rows, tq), 1)) // SEL_BLOCK
    forced = (blk == 0) | (blk == own) | (blk == own - 1)
    imp_t = jnp.where(blk > own, NEG_INF, jnp.where(forced, jnp.inf, imp_t))
    rank = _rank_lower_t(imp_t, nsb, blk)
    flag_t = jnp.where((rank < SEL_TOPK) & (blk <= own), 1.0, 0.0)
    flag_t = jnp.concatenate([flag_t, jnp.zeros((LANES - rows, tq), F32)], axis=0)
    return flag_t.T


def _masked_probs(s, mask):
    s = jnp.where(mask, s, NEG_INF)
    m = jnp.max(s, axis=-1, keepdims=True)
    m = jnp.where(m > NEG_INF, m, 0.0)
    e = jnp.exp(s - m)
    d = jnp.sum(e, axis=-1, keepdims=True)
    return e / jnp.where(d > 0, d, 1.0)


def _attn_c_prompt_body(qz_ref, gate_ref, selt_ref, wint_ref, kvc_ref, ovl_ref, exp_ref, o_ref, *, nsb):
    i = pl.program_id(1)
    tq = gate_ref.shape[0]
    nh = HEADS_C
    q4 = jnp.concatenate([qz_ref[:, h * LANES:(h + 1) * LANES] for h in range(nh)], axis=0)
    lane = _iota((tq, LANES), 1)
    pos = i * tq + _iota((tq, LANES), 0)
    rr = _iota((tq, tq), 0)
    cc = _iota((tq, tq), 1)
    causal = cc <= rr

    kvc = kvc_ref[...]
    cmp_ok = (CMP_STRIDE * lane + (CMP_LEN - 1)) <= pos
    s = _nt(q4, kvc).reshape(nh, tq, LANES)
    p = _masked_probs(s, cmp_ok[None])
    o_cmp = _nn(p.reshape(nh * tq, LANES).astype(BF16), kvc)
    p_sum = p[0] + p[1] + p[2] + p[3]

    flag = _nsa_flags_t(p_sum, ovl_ref[...], i * tq, nsb).astype(BF16)

    ones_t = jnp.ones((HEAD_DIM, tq), BF16)

    def pv_tile(kj):
        return jnp.concatenate([ones_t, kj[HEAD_DIM:2 * HEAD_DIM]], axis=0)

    kd = selt_ref[i]
    ok = (_nn(flag, exp_ref[i]) > 0.5) & causal
    s = jnp.where(ok[None], _nn(q4, kd).reshape(nh, tq, tq), NEG_INF)
    m = jnp.max(s, axis=-1, keepdims=True)
    p = jnp.exp((s - m).astype(BF16))
    acc = _nt(p.reshape(nh * tq, tq), pv_tile(kd)).reshape(nh, tq, LANES)

    def body(j, carry):
        m, acc = carry
        kj = selt_ref[j]
        ok = _nn(flag, exp_ref[j]) > 0.5
        s = jnp.where(ok[None], _nn(q4, kj).reshape(nh, tq, tq), NEG_INF)
        m_new = jnp.maximum(m, jnp.max(s, axis=-1, keepdims=True))
        p = jnp.exp((s - m_new).astype(BF16))
        pv = _nt(p.reshape(nh * tq, tq), pv_tile(kj)).reshape(nh, tq, LANES)
        return m_new, jnp.exp(m - m_new) * acc + pv

    m, acc = lax.fori_loop(0, i, body, (m, acc))
    o_sel = acc / pltpu.roll(acc, HEAD_DIM, 2)

    assert WINDOW == 2 * tq
    w2 = wint_ref[jnp.maximum(i - 2, 0)]
    w1 = wint_ref[jnp.maximum(i - 1, 0)]
    w0 = wint_ref[i]
    s2 = jnp.where(((cc >= rr) & (i >= 2))[None], _nn(q4, w2).reshape(nh, tq, tq), NEG_INF)
    s1 = jnp.where(i >= 1, _nn(q4, w1).reshape(nh, tq, tq), NEG_INF)
    s0 = jnp.where(causal[None], _nn(q4, w0).reshape(nh, tq, tq), NEG_INF)
    s_all = jnp.concatenate([s2, s1, s0], axis=-1)
    m = jnp.max(s_all, -1, keepdims=True)
    e = jnp.exp((s_all - m).astype(BF16)).reshape(nh * tq, 3 * tq)
    acc = (_nt(e[:, 0:tq], pv_tile(w2)) + _nt(e[:, tq:2 * tq], pv_tile(w1))
           + _nt(e[:, 2 * tq:3 * tq], pv_tile(w0)))
    o_win = (acc / pltpu.roll(acc, HEAD_DIM, 1)).reshape(nh, tq, LANES)

    o_cmp = o_cmp.reshape(nh, tq, LANES)
    g = gate_ref[...]
    heads = []
    for h in range(nh):
        heads.append(g[:, 3 * h:3 * h + 1] * o_cmp[h] + g[:, 3 * h + 1:3 * h + 2] * o_sel[h]
                     + g[:, 3 * h + 2:3 * h + 3] * o_win[h])
    for pair in range(nh // 2):
        both = jnp.where(lane < HEAD_DIM, pltpu.roll(heads[2 * pair], HEAD_DIM, 1), heads[2 * pair + 1])
        o_ref[:, pair * LANES:(pair + 1) * LANES] = both.astype(BF16)


def _attn_c_prompt(qz16, gates, selt16, wint16, kvcmp16, n, s):
    nq = s // TQ
    ovl, nsb = _overlap_matrix(s)
    expand = _expand_matrix(s)
    qrow = lambda w: pl.BlockSpec((TQ, w), lambda b, i: (b * nq + i, 0))
    tiles = pl.BlockSpec((None, nq, 2 * HEAD_DIM, TQ), lambda b, i: (b, 0, 0, 0))
    return pl.pallas_call(
        functools.partial(_attn_c_prompt_body, nsb=nsb),
        out_shape=jax.ShapeDtypeStruct((n * s, C_W), BF16),
        grid=(n, nq),
        in_specs=[qrow(2 * C_W), qrow(LANES), tiles, tiles,
                  pl.BlockSpec((_CMP_ROWS, LANES), lambda b, i: (b, 0)),
                  pl.BlockSpec((_CMP_ROWS, LANES), lambda b, i: (0, 0)),
                  pl.BlockSpec((nq, LANES, TQ), lambda b, i: (0, 0, 0))],
        out_specs=qrow(C_W),
        compiler_params=_params(2),
        name="nsa_attn_prompt",
    )(qz16, gates, selt16, wint16, kvcmp16, ovl, expand)


def _pad_page(x):
    rows, w = x.shape
    return jnp.concatenate([x, jnp.zeros((PAGE_SIZE - rows, w), x.dtype)], axis=0)


def _new_page_mask(n_rows, nq):
    r = _iota((n_rows, PAGE_SIZE), 0) % nq
    t = _iota((n_rows, PAGE_SIZE), 1)
    return t <= r


def _dec_a_body(pt_ref, q_ref, kn_ref, vn_ref, lam_ref, gh_ref, *refs, lam_init):
    n_pages = (len(refs) - 1) // 2
    kp, vp, o_ref = refs[:n_pages], refs[n_pages:2 * n_pages], refs[-1]
    nq = q_ref.shape[0]
    rows = 2 * HEADS_A * nq
    per_head = 2 * nq
    lam = _lam_value(lam_ref, lam_init)
    qt = jnp.concatenate([q_ref[...]] * (2 * HEADS_A), axis=0)
    diag = (_iota((rows, A_W), 0) // nq) == (_iota((rows, A_W), 1) // HEAD_DIM)
    qbd = jnp.where(diag, qt, jnp.zeros_like(qt))
    kn = _pad_page(kn_ref[...]).astype(BF16)
    vn = _pad_page(vn_ref[...]).astype(BF16)
    s_new = jnp.where(_new_page_mask(rows, nq), _nt(qbd, kn), NEG_INF)
    scores = [_nn(qbd, kp[j][...].astype(BF16)) for j in range(n_pages)]
    m = jnp.max(s_new, axis=1, keepdims=True)
    for s in scores:
        m = jnp.maximum(m, jnp.max(s, axis=1, keepdims=True))
    e = jnp.exp(s_new - m)
    d = jnp.sum(e, axis=1, keepdims=True)
    e = e.astype(BF16)
    accs = [_nn(e[h * per_head:(h + 1) * per_head], vn[:, h * LANES:(h + 1) * LANES]) for h in range(HEADS_A)]
    for j in range(n_pages):
        e = jnp.exp(scores[j] - m)
        d = d + jnp.sum(e, axis=1, keepdims=True)
        e = e.astype(BF16)
        for h in range(HEADS_A):
            vh = vp[j][pl.ds(h, PAGE_SIZE, stride=HEADS_A), :].astype(BF16)
            accs[h] = accs[h] + _nn(e[h * per_head:(h + 1) * per_head], vh)
    gh = gh_ref[...]
    for h in range(HEADS_A):
        on = accs[h] / d[h * per_head:(h + 1) * per_head]
        o = on[0:nq] - lam * on[nq:2 * nq]
        o_ref[:, h * LANES:(h + 1) * LANES] = _head_rms_scale(o, gh, lam_init).astype(BF16)


def _dec_a(q16, kn32, vn32, lam_p, g_head, cache_kt, cache_v4, page_table, l, nq):
    n, n_pages = page_table.shape
    row = lambda w: pl.BlockSpec((nq, w), lambda b, pt: (b, 0))
    grid_spec = pltpu.PrefetchScalarGridSpec(
        num_scalar_prefetch=1, grid=(n,),
        in_specs=[row(A_W), row(A_W), row(A_W),
                  pl.BlockSpec((4, HEAD_DIM), lambda b, pt: (0, 0)),
                  pl.BlockSpec((1, 2 * HEAD_DIM), lambda b, pt: (0, 0))]
                 + _page_specs(l, n_pages, A_W) + _page_specs(l, n_pages, A_W),
        out_specs=row(A_W))
    return pl.pallas_call(
        functools.partial(_dec_a_body, lam_init=_lam_init(l)),
        out_shape=jax.ShapeDtypeStruct((n * nq, A_W), BF16),
        grid_spec=grid_spec,
        compiler_params=_params(1),
        name="diff_attn_decode",
    )(page_table, q16, kn32, vn32, lam_p, g_head.reshape(1, 2 * HEAD_DIM),
      *([cache_kt] * n_pages), *([cache_v4] * n_pages))


def _dec_b_body(pt_ref, q_ref, q32_ref, kvn_ref, *refs, group):
    pages, o_ref = refs[:-1], refs[-1]
    n_pages = len(pages) // group
    nq = q_ref.shape[0] // group
    for g in range(group):
        r = slice(g * nq, (g + 1) * nq)
        o_ref[r, :] = _dec_b_one(q_ref[r, :], q32_ref[r, :], kvn_ref[r, :],
                                 pages[g * n_pages:(g + 1) * n_pages]).astype(BF16)


def _dec_b_one(q, q32, kvn_rows, pages):
    n_pages = len(pages)
    nq = q.shape[0]
    rows = HEADS_B * nq
    pages_per_blk = MOBA_BLOCK // PAGE_SIZE
    nb = n_pages // pages_per_blk
    diag = (_iota((rows, B_W), 0) // nq) == (_iota((rows, B_W), 1) // HEAD_DIM)
    qt = jnp.concatenate([q] * HEADS_B, axis=0)
    qbd = jnp.where(diag, qt, jnp.zeros_like(qt))
    qbd32 = jnp.where(diag, jnp.concatenate([q32] * HEADS_B, axis=0), 0.0)

    blocks = []
    for b in range(nb):
        blocks.append(jnp.concatenate([pages[j][0:B_W, :] for j in range(b * pages_per_blk, (b + 1) * pages_per_blk)],
                                      axis=1))
    kmean_t = _block_means_t(blocks)
    lane = _iota((rows, LANES), 1)
    gate = _nn_precise(qbd32, kmean_t)
    gate = jnp.where(lane < nb, gate, NEG_INF)
    chosen = jnp.where((_rank_lower(gate, nb, lane) < MOBA_TOPK) & (lane < nb), 1.0, 0.0)

    kvn = _pad_page(kvn_rows).astype(BF16)
    s_new = jnp.where(_new_page_mask(rows, nq), _nt(qbd, kvn[:, 0:B_W]), NEG_INF)
    scores = []
    for j in range(n_pages):
        b = j // pages_per_blk
        s = _nn(qbd, pages[j][0:B_W, :].astype(BF16))
        scores.append(jnp.where(chosen[:, b:b + 1] > 0.5, s, NEG_INF))
    m = jnp.max(s_new, axis=1, keepdims=True)
    for s in scores:
        m = jnp.maximum(m, jnp.max(s, axis=1, keepdims=True))
    e = jnp.exp(s_new - m)
    d = jnp.sum(e, axis=1, keepdims=True)
    acc = _nn(e.astype(BF16), kvn[:, B_W:2 * B_W])
    for j in range(n_pages):
        e = jnp.exp(scores[j] - m)
        d = d + jnp.sum(e, axis=1, keepdims=True)
        acc = acc + _nt(e.astype(BF16), pages[j][B_W:2 * B_W, :].astype(BF16))
    on = acc / d
    head_of_lane = _iota((nq, B_W), 1) // HEAD_DIM
    o = jnp.zeros((nq, B_W), F32)
    for h in range(HEADS_B):
        o = o + jnp.where(head_of_lane == h, on[h * nq:(h + 1) * nq], 0.0)
    return o


def _dec_b(q16, q32, kvn32, cache_kvt, page_table, l, nq):
    n, n_pages = page_table.shape
    group = DEC_GROUP
    row = lambda w: pl.BlockSpec((group * nq, w), lambda b, pt: (b, 0))
    grid_spec = pltpu.PrefetchScalarGridSpec(
        num_scalar_prefetch=1, grid=(n // group,),
        in_specs=[row(B_W), row(B_W), row(2 * B_W)] + _page_specs(l, n_pages, 2 * B_W, group=group),
        out_specs=row(B_W))
    return pl.pallas_call(
        functools.partial(_dec_b_body, group=group),
        out_shape=jax.ShapeDtypeStruct((n * nq, B_W), BF16),
        grid_spec=grid_spec,
        compiler_params=_params(1),
        name="moba_attn_decode",
    )(page_table, q16, q32, kvn32, *([cache_kvt] * (n_pages * group)))


def _dec_c_body(pt_ref, qz_ref, gate_ref, ckvn_ref, winn_ref, kvc_ref, ovl_ref, exp_ref, st_ref, *refs, nsb, q0, group):
    pages, o_ref = refs[:-1], refs[-1]
    n_pages = len(pages) // group
    nq = gate_ref.shape[0] // group
    ovl = ovl_ref[...]
    expand = exp_ref[...]
    for g in range(group):
        r = slice(g * nq, (g + 1) * nq)
        _dec_c_one(qz_ref[r, :], gate_ref[r, :], ckvn_ref[r, :], winn_ref[r, :],
                   kvc_ref[g * _CMP_ROWS:(g + 1) * _CMP_ROWS, :], ovl, expand, st_ref[g],
                   pages[g * n_pages:(g + 1) * n_pages], o_ref, r, nsb, q0)


def _dec_c_one(qz, g, ckvn_rows, winn_rows, kvc, ovl, expand, st32, pages, o_ref, out_rows, nsb, q0):
    n_pages = len(pages)
    nq = g.shape[0]
    nh = HEADS_C
    rows = nh * nq
    q4 = jnp.concatenate([qz[:, h * LANES:(h + 1) * LANES] for h in range(nh)], axis=0)
    lane = _iota((nq, LANES), 1)
    pos = q0 + _iota((nq, LANES), 0)
    lane4 = _iota((rows, LANES), 1)
    qrow4 = _iota((rows, LANES), 0) % nq
    new_ok = _new_page_mask(rows, nq)

    cmp_ok = (CMP_STRIDE * lane4 + (CMP_LEN - 1)) <= (q0 + qrow4)
    p = _masked_probs(_nt(q4, kvc), cmp_ok)
    o_cmp = _nn(p.astype(BF16), kvc)
    p_sum = p[0:nq]
    for h in range(1, nh):
        p_sum = p_sum + p[h * nq:(h + 1) * nq]

    own = pos // SEL_BLOCK
    flag = _nsa_flags(p_sum, ovl, own, lane, nsb)
    flag4 = jnp.concatenate([flag] * nh, axis=0)
    blk_per_page = PAGE_SIZE // SEL_BLOCK
    assert blk_per_page == 2

    ckvn = _pad_page(ckvn_rows).astype(BF16)
    kn = ckvn[:, LANES:2 * LANES]
    own_blk = n_pages * blk_per_page
    s_new = jnp.where(new_ok & (flag4[:, own_blk:own_blk + 1] > 0.5), _nt(q4, kn), NEG_INF)
    kt_all = jnp.concatenate([pages[j][...].astype(BF16) for j in range(n_pages)], axis=1)
    ok = _nn(flag4.astype(BF16), expand) > 0.5
    s_old = jnp.where(ok, _nn(q4, kt_all), NEG_INF)
    m = jnp.maximum(jnp.max(s_new, axis=1, keepdims=True), jnp.max(s_old, axis=1, keepdims=True))
    e_new = jnp.exp(s_new - m)
    e_old = jnp.exp(s_old - m)
    d = jnp.sum(e_new, axis=1, keepdims=True) + jnp.sum(e_old, axis=1, keepdims=True)
    acc = _nn(e_new.astype(BF16), kn) + _nt(e_old.astype(BF16), kt_all)
    o_sel = acc / d

    wn = _pad_page(winn_rows).astype(BF16)
    s_new = jnp.where(new_ok, _nt(q4, wn), NEG_INF)
    st = st32.astype(BF16)
    key = _iota((rows, WINDOW), 1)
    ok = key >= (_iota((rows, WINDOW), 0) % nq)
    s_old = jnp.where(ok, _nn(q4, st), NEG_INF)
    m = jnp.maximum(jnp.max(s_new, axis=1, keepdims=True), jnp.max(s_old, axis=1, keepdims=True))
    e_new = jnp.exp(s_new - m)
    e_old = jnp.exp(s_old - m)
    d = jnp.sum(e_new, axis=1, keepdims=True) + jnp.sum(e_old, axis=1, keepdims=True)
    o_win = (_nn(e_new.astype(BF16), wn) + _nt(e_old.astype(BF16), st)) / d

    heads = []
    for h in range(nh):
        r = slice(h * nq, (h + 1) * nq)
        heads.append(g[:, 3 * h:3 * h + 1] * o_cmp[r] + g[:, 3 * h + 1:3 * h + 2] * o_sel[r]
                     + g[:, 3 * h + 2:3 * h + 3] * o_win[r])
    for pair in range(nh // 2):
        both = jnp.where(lane < HEAD_DIM, pltpu.roll(heads[2 * pair], HEAD_DIM, 1), heads[2 * pair + 1])
        o_ref[out_rows, pair * LANES:(pair + 1) * LANES] = both.astype(BF16)


def _dec_c(qz16, gates, ckvn32, winn32, kvcmp16, state_t, cache_ct, page_table, l, nq, q0):
    n, n_pages = page_table.shape
    assert state_t.shape[3] == WINDOW and q0 >= WINDOW
    ovl, nsb = _overlap_matrix(q0 + nq)
    keys = np.arange(n_pages * PAGE_SIZE)
    expand = jnp.asarray(np.arange(LANES)[:, None] == keys[None, :] // SEL_BLOCK, BF16)
    group = 1
    row = lambda w: pl.BlockSpec((group * nq, w), lambda b, pt: (b, 0))
    grid_spec = pltpu.PrefetchScalarGridSpec(
        num_scalar_prefetch=1, grid=(n // group,),
        in_specs=[row(2 * C_W), row(LANES), row(4 * HEAD_DIM), row(2 * HEAD_DIM),
                  pl.BlockSpec((group * _CMP_ROWS, LANES), lambda b, pt: (b, 0)),
                  pl.BlockSpec((_CMP_ROWS, LANES), lambda b, pt: (0, 0)),
                  pl.BlockSpec((LANES, n_pages * PAGE_SIZE), lambda b, pt: (0, 0)),
                  pl.BlockSpec((None, group, 2 * HEAD_DIM, WINDOW), lambda b, pt: (l, b, 0, 0))]
                 + _page_specs(l, n_pages, 2 * HEAD_DIM, row_block=1, group=group),
        out_specs=row(C_W))
    return pl.pallas_call(
        functools.partial(_dec_c_body, nsb=nsb, q0=q0, group=group),
        out_shape=jax.ShapeDtypeStruct((n * nq, C_W), BF16),
        grid_spec=grid_spec,
        compiler_params=_params(1),
        name="nsa_attn_decode",
    )(page_table, qz16, gates, ckvn32, winn32, kvcmp16, ovl, expand, state_t, *([cache_ct] * (n_pages * group)))


def _rope_tables(n_pos):
    inv = ROPE_THETA ** (-jnp.arange(0, HEAD_DIM, 2, dtype=F32) / HEAD_DIM)
    ang = jnp.arange(n_pos, dtype=F32)[:, None] * inv[None, :]
    cos, sin = jnp.cos(ang), jnp.sin(ang)
    cos128 = jnp.concatenate([cos, cos, cos, cos], axis=-1)
    sin128 = jnp.concatenate([-sin, sin, -sin, sin], axis=-1)
    return cos128, sin128, cos.T, sin.T


def kernel(x_prompt, x_sample, cache_a_k, cache_a_v, cache_b_kv, cache_c_kv, state_c_win, page_table,
           w_in, w_out, g_mix, g_ffn, w_ffn_gate, w_ffn_up, w_ffn_down, diff_lambda, g_diff_head,
           w_cmp1, w_cmp2, cmp_pos, g_final):
    n_p, s_p, _ = x_prompt.shape
    n_s, s_s, _ = x_sample.shape
    n_pages = page_table.shape[1]
    past_len = n_pages * cache_a_k.shape[2]
    n_phys = cache_a_k.shape[1]
    assert cache_a_k.shape[2] == PAGE_SIZE and s_p % TM_PROJ == 0 and TM_PROJ % s_s == 0

    cos, sin, cos_t, sin_t = _rope_tables(past_len + s_s)
    tabs_p = (cos[:s_p], sin[:s_p], cos_t[:, :s_p], sin_t[:, :s_p])
    reps = TM_PROJ // s_s
    cos_s = jnp.tile(cos[past_len:past_len + s_s], (reps, 1))
    sin_s = jnp.tile(sin[past_len:past_len + s_s], (reps, 1))

    ckt = cache_a_k.transpose(0, 1, 3, 4, 5, 2).reshape(DEPTH, n_phys, A_W, PAGE_SIZE)
    cv4 = cache_a_v.reshape(DEPTH, n_phys, PAGE_SIZE * HEADS_A, 2 * HEAD_DIM)
    cbt = cache_b_kv.transpose(0, 1, 3, 4, 5, 2).reshape(DEPTH, n_phys, 2 * B_W, PAGE_SIZE)
    cct = cache_c_kv.transpose(0, 1, 3, 4, 2).reshape(DEPTH, n_phys, 4 * HEAD_DIM, PAGE_SIZE)
    stt = state_c_win.transpose(0, 1, 3, 4, 2).reshape(DEPTH, n_s, 2 * HEAD_DIM, WINDOW)

    hp = x_prompt.reshape(n_p * s_p, D_MODEL)
    hs = x_sample.reshape(n_s * s_s, D_MODEL)
    ent_p, ent_s = [], []
    wg = w_ffn_gate.astype(BF16)
    wu = w_ffn_up.astype(BF16)
    wd = w_ffn_down.astype(BF16)
    for l in range(DEPTH):
        w_main = w_in[l][:, :MAIN_W].astype(BF16)
        w_t = jnp.concatenate([w_in[l][:, _O_KA:_O_KA + A_W], w_in[l][:, _O_KB:_O_KB + 2 * B_W],
                               w_in[l][:, _O_KVC:_O_KVC + KVC_W]], axis=1).T.astype(BF16)
        w_gate = jnp.pad(w_in[l][:, MAIN_W:], ((0, 0), (0, LANES - GATE_W))).astype(BF16)
        wo = w_out[l].astype(BF16)
        cw = _compress_weights(w_cmp1[l], w_cmp2[l], cmp_pos[l])
        last = l == DEPTH - 1

        hp = _ffn(hp, g_ffn[l, 0], wg, wu, wd, l, 0)
        (kat, va, bkvt, ckvt, wint, qa16, kat16, va16, qb16, qb32, bkvt16, qcz16, selt16, wint16, gates) = \
            _proj_prompt(hp, g_mix[l], w_main, w_t, w_gate, tabs_p, n_p, s_p)
        oa = _attn_a_prompt(qa16, kat16, va16, diff_lambda[l], g_diff_head[l], n_p, s_p, l)
        ob = _attn_b_prompt(qb16, qb32, bkvt, bkvt16, n_p, s_p)
        kvcmp = _compress_prompt(ckvt, cw, n_p, s_p)
        oc = _attn_c_prompt(qcz16, gates, selt16, wint16, kvcmp, n_p, s_p)
        hp = _outproj(hp, oa, ob, oc, wo)
        hp = _ffn(hp, g_ffn[l, 1], wg, wu, wd, l, 1, g_final if last else None)
        win_keep = min(WINDOW, s_p)
        ent_p.append((kat.reshape(n_p, HEADS_A, 2, HEAD_DIM, s_p).transpose(0, 4, 1, 2, 3),
                      va.reshape(n_p, s_p, HEADS_A, 2 * HEAD_DIM),
                      bkvt.reshape(n_p, 2, HEADS_B, HEAD_DIM, s_p).transpose(0, 4, 1, 2, 3),
                      ckvt.reshape(n_p, 4, HEAD_DIM, s_p).transpose(0, 3, 1, 2),
                      wint[:, :, s_p - win_keep:].reshape(n_p, 2, HEAD_DIM, win_keep).transpose(0, 3, 1, 2)))

        hs = _ffn(hs, g_ffn[l, 0], wg, wu, wd, l, 0)
        (ka, va, bkv, ckv, win, qa16, qb16, qb32, qcz16, gates) = _proj_sample(
            hs, g_mix[l], w_main, w_gate, cos_s, sin_s)
        oa = _dec_a(qa16, ka, va, diff_lambda[l], g_diff_head[l], ckt, cv4, page_table, l, s_s)
        ob = _dec_b(qb16, qb32, bkv, cbt, page_table, l, s_s)
        kvcmp = _compress_pages(cct, page_table, cw, l)
        oc = _dec_c(qcz16, gates, ckv, win, kvcmp, stt, cct, page_table, l, s_s, past_len)
        hs = _outproj(hs, oa, ob, oc, wo)
        hs = _ffn(hs, g_ffn[l, 1], wg, wu, wd, l, 1, g_final if last else None)
        win_new = win.reshape(n_s, s_s, 2, HEAD_DIM)
        win_all = jnp.concatenate([state_c_win[l], win_new], axis=1)
        ent_s.append((ka.reshape(n_s, s_s, HEADS_A, 2, HEAD_DIM), va.reshape(n_s, s_s, HEADS_A, 2 * HEAD_DIM),
                      bkv.reshape(n_s, s_s, 2, HEADS_B, HEAD_DIM), ckv.reshape(n_s, s_s, 4, HEAD_DIM),
                      win_all[:, win_all.shape[1] - min(WINDOW, win_all.shape[1]):]))

    st_ = lambda ents, i: jnp.stack([e[i] for e in ents], axis=0)
    return (hp.reshape(n_p, s_p, D_MODEL), hs.reshape(n_s, s_s, D_MODEL),
            st_(ent_p, 0), st_(ent_s, 0), st_(ent_p, 1), st_(ent_s, 1),
            st_(ent_p, 2), st_(ent_s, 2), st_(ent_p, 3), st_(ent_s, 3),
            st_(ent_p, 4), st_(ent_s, 4))
```

```python
import functools
import math

import numpy as np
import jax
import jax.numpy as jnp
from jax import lax
from jax.experimental import pallas as pl
from jax.experimental.pallas import tpu as pltpu

F32 = jnp.float32
BF16 = jnp.bfloat16

D_MODEL = 1024
DEPTH = 2
HEAD_DIM = 64
HEADS_A = 4
HEADS_B = 4
HEADS_C = 4
D_FF = 2816
ROPE_THETA = 10000.0
MOBA_BLOCK = 256
MOBA_TOPK = 3
CMP_LEN = 32
CMP_STRIDE = 16
CMP_HIDDEN = 4 * HEAD_DIM
SEL_BLOCK = 64
SEL_TOPK = 16
WINDOW = 512
RMS_EPS = 1e-6
PAGE_SIZE = 128

A_W = HEADS_A * 2 * HEAD_DIM
B_W = HEADS_B * HEAD_DIM
C_W = HEADS_C * HEAD_DIM
KVC_W = 6 * HEAD_DIM
GATE_W = 3 * HEADS_C
MAIN_W = 3 * A_W + 3 * B_W + C_W + KVC_W
LANES = 128
QK_SCALE = HEAD_DIM ** -0.5
NEG_INF = float("-inf")
VMEM_LIMIT = 56 * 1024 * 1024

TM_FFN = 512
TF_FFN = 256
TM_PROJ = 512
TQ = 256
NSA_DEC_GROUP = 2
DEC_GROUP = 2
A_HEADS_PER_LOOP = 4

_O_QA, _O_KA, _O_VA = 0, A_W, 2 * A_W
_O_QB = 3 * A_W
_O_KB, _O_VB = _O_QB + B_W, _O_QB + 2 * B_W
_O_QC = _O_QB + 3 * B_W
_O_KVC = _O_QC + C_W


def _nn(a, b):
    return jnp.dot(a, b, preferred_element_type=F32)


def _nt(a, b):
    return lax.dot_general(a, b, (((1,), (1,)), ((), ())), preferred_element_type=F32)


def _split3(x):
    hi = x.astype(BF16)
    r1 = x - hi.astype(F32)
    mid = r1.astype(BF16)
    lo = (r1 - mid.astype(F32)).astype(BF16)
    return hi, mid, lo


def _nn_precise(a, b):
    a_hi, a_mid, _ = _split3(a)
    b_hi, b_mid, _ = _split3(b)
    return _nn(a_hi, b_hi) + (_nn(a_hi, b_mid) + _nn(a_mid, b_hi))


def _rms(x):
    return x * lax.rsqrt(jnp.mean(x * x, axis=-1, keepdims=True) + RMS_EPS)


def _iota(shape, dim):
    return lax.broadcasted_iota(jnp.int32, shape, dim)


def _params(n_axes):
    return pltpu.CompilerParams(dimension_semantics=("arbitrary",) * n_axes,
                                vmem_limit_bytes=VMEM_LIMIT)


def _lam_value(lam_ref, lam_init):
    lp = lam_ref[...]
    a = jnp.sum(lp[0:1] * lp[1:2], axis=1, keepdims=True)
    b = jnp.sum(lp[2:3] * lp[3:4], axis=1, keepdims=True)
    return jnp.exp(a) - jnp.exp(b) + lam_init


def _lam_init(lidx):
    return 0.8 - 0.6 * math.exp(-0.3 * lidx)


def _ffn_body(*refs, mixed, final):
    refs = list(refs)
    x_ref = refs.pop(0)
    x = x_ref[...]
    if mixed:
        oa_ref, ob_ref, oc_ref, wo_ref = refs[:4]
        refs = refs[4:]
        x = x + _nn(oa_ref[...], wo_ref[0:A_W, :])
        x = x + _nn(ob_ref[...], wo_ref[A_W:A_W + B_W, :])
        x = x + _nn(oc_ref[...], wo_ref[A_W + B_W:, :])
    g_ref, wg_ref, wu_ref, wd_ref = refs[:4]
    o_ref = refs[-1]
    xn = (_rms(x) * g_ref[...]).astype(BF16)
    acc = jnp.zeros_like(x)
    for f in range(D_FF // TF_FFN):
        sl = slice(f * TF_FFN, (f + 1) * TF_FFN)
        g = _nn(xn, wg_ref[:, sl])
        u = _nn(xn, wu_ref[:, sl])
        a = (g * jax.nn.sigmoid(g) * u).astype(BF16)
        acc = acc + _nn(a, wd_ref[sl, :])
    y = x + 0.5 * acc
    if final:
        y = _rms(y) * refs[4][...]
    o_ref[...] = y


def _ffn(x, g, wg, wu, wd, l, k, mix=None, g_final=None):
    m = x.shape[0]
    row = lambda w: pl.BlockSpec((TM_FFN, w), lambda i: (i, 0))
    vec = pl.BlockSpec((1, D_MODEL), lambda i: (0, 0))
    full = lambda r, c: pl.BlockSpec((None, None, r, c), lambda i: (l, k, 0, 0))
    in_specs, args = [row(D_MODEL)], [x]
    if mix is not None:
        oa, ob, oc, w_out = mix
        in_specs += [row(A_W), row(B_W), row(C_W), pl.BlockSpec((None, D_MODEL, D_MODEL), lambda i: (l, 0, 0))]
        args += [oa, ob, oc, w_out]
    in_specs += [vec, full(D_MODEL, D_FF), full(D_MODEL, D_FF), full(D_FF, D_MODEL)]
    args += [g.reshape(1, D_MODEL), wg, wu, wd]
    if g_final is not None:
        in_specs.append(vec)
        args.append(g_final.reshape(1, D_MODEL))
    return pl.pallas_call(
        functools.partial(_ffn_body, mixed=mix is not None, final=g_final is not None),
        out_shape=jax.ShapeDtypeStruct((m, D_MODEL), F32),
        grid=(m // TM_FFN,),
        in_specs=in_specs,
        out_specs=row(D_MODEL),
        compiler_params=_params(1),
        name="ffn_half",
    )(*args)


def _rope_rows(lane_shape):
    lane = _iota(lane_shape, 1)
    return (lane % HEAD_DIM) < (HEAD_DIM // 2), lane < HEAD_DIM


def _rope_lanes(x, cos, sin, lo32):
    sh = jnp.where(lo32, pltpu.roll(x, LANES - HEAD_DIM // 2, 1), pltpu.roll(x, HEAD_DIM // 2, 1))
    return x * cos + sh * sin


def _proj_queries(u, w_ref, cos, sin, lo32, lo64, qa16_ref, qb16_ref, qb32_ref, qcz16_ref):
    p = _nn(u, w_ref[:, _O_QA:_O_QA + A_W])
    for k in range(A_W // LANES):
        qa16_ref[:, k * LANES:(k + 1) * LANES] = (
            _rope_lanes(p[:, k * LANES:(k + 1) * LANES], cos, sin, lo32) * QK_SCALE).astype(BF16)
    p = _nn(u, w_ref[:, _O_QB:_O_QB + B_W])
    for k in range(B_W // LANES):
        r = _rope_lanes(p[:, k * LANES:(k + 1) * LANES], cos, sin, lo32) * QK_SCALE
        qb32_ref[:, k * LANES:(k + 1) * LANES] = r
        qb16_ref[:, k * LANES:(k + 1) * LANES] = r.astype(BF16)
    p = _nn(u, w_ref[:, _O_QC:_O_QC + C_W])
    for k in range(C_W // LANES):
        r = _rope_lanes(p[:, k * LANES:(k + 1) * LANES], cos, sin, lo32) * QK_SCALE
        even = jnp.where(lo64, r, 0.0)
        odd = jnp.where(lo64, pltpu.roll(r, HEAD_DIM, 1), 0.0)
        qcz16_ref[:, (2 * k) * LANES:(2 * k + 1) * LANES] = even.astype(BF16)
        qcz16_ref[:, (2 * k + 1) * LANES:(2 * k + 2) * LANES] = odd.astype(BF16)


def _proj_sample_body(h_ref, g_ref, w_ref, wgate_ref, cos_ref, sin_ref,
                      ka_ref, va_ref, bkv_ref, ckv_ref, win_ref,
                      qa16_ref, qb16_ref, qb32_ref, qcz16_ref, gate_ref):
    tm = h_ref.shape[0]
    u = (_rms(h_ref[...]) * g_ref[...]).astype(BF16)
    cos = cos_ref[...]
    sin = sin_ref[...]
    lo32, lo64 = _rope_rows((tm, LANES))
    _proj_queries(u, w_ref, cos, sin, lo32, lo64, qa16_ref, qb16_ref, qb32_ref, qcz16_ref)
    p = _nn(u, w_ref[:, _O_KA:_O_KA + A_W])
    for k in range(A_W // LANES):
        ka_ref[:, k * LANES:(k + 1) * LANES] = _rope_lanes(p[:, k * LANES:(k + 1) * LANES], cos, sin, lo32)
    va_ref[...] = _nn(u, w_ref[:, _O_VA:_O_VA + A_W])
    p = _nn(u, w_ref[:, _O_KB:_O_KB + B_W])
    for k in range(B_W // LANES):
        bkv_ref[:, k * LANES:(k + 1) * LANES] = _rope_lanes(p[:, k * LANES:(k + 1) * LANES], cos, sin, lo32)
    bkv_ref[:, B_W:2 * B_W] = _nn(u, w_ref[:, _O_VB:_O_VB + B_W])
    p = _nn(u, w_ref[:, _O_KVC:_O_KVC + KVC_W])
    for k in range(KVC_W // LANES):
        x = p[:, k * LANES:(k + 1) * LANES]
        r = jnp.where(lo64, _rope_lanes(x, cos, sin, lo32), x)
        if k < 2:
            ckv_ref[:, k * LANES:(k + 1) * LANES] = r
        else:
            win_ref[...] = r
    gate_ref[...] = jax.nn.sigmoid(_nn(u, wgate_ref[...]))


def _proj_sample(h, g_mix, w_main, w_gate, cos, sin):
    m = h.shape[0]
    tm = TM_PROJ
    row = lambda w: pl.BlockSpec((tm, w), lambda i: (i, 0))
    full = lambda shape: pl.BlockSpec(shape, lambda i: (0, 0))
    outs = ((A_W, F32), (A_W, F32), (2 * B_W, F32), (4 * HEAD_DIM, F32), (2 * HEAD_DIM, F32),
            (A_W, BF16), (B_W, BF16), (B_W, F32), (2 * C_W, BF16), (LANES, F32))
    return pl.pallas_call(
        _proj_sample_body,
        out_shape=[jax.ShapeDtypeStruct((m, w), dt) for w, dt in outs],
        grid=(m // tm,),
        in_specs=[row(D_MODEL), full((1, D_MODEL)), full((D_MODEL, MAIN_W)), full((D_MODEL, LANES)),
                  full((tm, LANES)), full((tm, LANES))],
        out_specs=[row(w) for w, _ in outs],
        compiler_params=_params(1),
        name="in_proj_rope_sample",
    )(h, g_mix.reshape(1, D_MODEL), w_main, w_gate, cos, sin)


_KT_ROWS = A_W + 2 * B_W + KVC_W


def _proj_prompt_body(h_ref, g_ref, w_ref, wt_ref, wgate_ref, cos_ref, sin_ref, cos_t_ref, sin_t_ref,
                      kat_ref, va_ref, bkvt_ref, ckvt_ref, wint_ref,
                      qa16_ref, kat16_ref, va16_ref, qb16_ref, qb32_ref, bkvt16_ref,
                      qcz16_ref, selt16_ref, wint16_ref, gate_ref):
    tm = h_ref.shape[0]
    half = HEAD_DIM // 2
    u = (_rms(h_ref[...]) * g_ref[...]).astype(BF16)
    lo32, lo64 = _rope_rows((tm, LANES))
    _proj_queries(u, w_ref, cos_ref[...], sin_ref[...], lo32, lo64, qa16_ref, qb16_ref, qb32_ref, qcz16_ref)

    p = _nn(u, w_ref[:, _O_VA:_O_VA + A_W])
    va16_ref[...] = p.astype(BF16)
    for h in range(HEADS_A):
        va_ref[pl.ds(h, tm, stride=HEADS_A), :] = p[:, h * LANES:(h + 1) * LANES]

    cos_t = cos_t_ref[...]
    sin_t = sin_t_ref[...]

    def rope_t(x):
        x1, x2 = x[0:half], x[half:HEAD_DIM]
        return jnp.concatenate([x1 * cos_t - x2 * sin_t, x2 * cos_t + x1 * sin_t], axis=0)

    def store_t(f32_ref, b16_ref, row0, val):
        rows = val.shape[0]
        if f32_ref is not None:
            f32_ref[row0:row0 + rows, :] = val
        if b16_ref is not None:
            for t in range(tm // TQ):
                b16_ref[t, row0:row0 + rows, :] = val[:, t * TQ:(t + 1) * TQ].astype(BF16)

    pt = _nt(wt_ref[0:A_W, :], u)
    for g in range(A_W // HEAD_DIM):
        store_t(kat_ref, kat16_ref, g * HEAD_DIM, rope_t(pt[g * HEAD_DIM:(g + 1) * HEAD_DIM]))
    pt = _nt(wt_ref[A_W:A_W + 2 * B_W, :], u)
    for g in range(B_W // HEAD_DIM):
        store_t(bkvt_ref, bkvt16_ref, g * HEAD_DIM, rope_t(pt[g * HEAD_DIM:(g + 1) * HEAD_DIM]))
    store_t(bkvt_ref, bkvt16_ref, B_W, pt[B_W:2 * B_W])
    pt = _nt(wt_ref[A_W + 2 * B_W:_KT_ROWS, :], u)
    for g in range(KVC_W // HEAD_DIM):
        x = pt[g * HEAD_DIM:(g + 1) * HEAD_DIM]
        if g % 2 == 0:
            x = rope_t(x)
        if g < 2:
            store_t(ckvt_ref, None, g * HEAD_DIM, x)
        elif g < 4:
            store_t(ckvt_ref, None, g * HEAD_DIM, x)
            store_t(None, selt16_ref, (g - 2) * HEAD_DIM, x)
        else:
            store_t(wint_ref, wint16_ref, (g - 4) * HEAD_DIM, x)
    gate_ref[...] = jax.nn.sigmoid(_nn(u, wgate_ref[...]))


def _proj_prompt(h, g_mix, w_main, w_t, w_gate, tabs, n, s):
    cos, sin, cos_t, sin_t = tabs
    m = n * s
    tm = TM_PROJ
    per_seq = s // tm
    nt = tm // TQ
    row = lambda w: pl.BlockSpec((tm, w), lambda i: (i, 0))
    full = lambda shape: pl.BlockSpec(shape, lambda i: (0,) * len(shape))
    tab = pl.BlockSpec((tm, LANES), lambda i: (i % per_seq, 0))
    tab_t = pl.BlockSpec((HEAD_DIM // 2, tm), lambda i: (0, i % per_seq))
    feat = lambda w: pl.BlockSpec((None, w, tm), lambda i: (i // per_seq, 0, i % per_seq))
    tiles = lambda w: pl.BlockSpec((None, nt, w, TQ), lambda i: (i // per_seq, i % per_seq, 0, 0))
    sds = jax.ShapeDtypeStruct
    out_shape = [sds((n, A_W, s), F32), sds((m * HEADS_A, LANES), F32), sds((n, 2 * B_W, s), F32),
                 sds((n, 4 * HEAD_DIM, s), F32), sds((n, 2 * HEAD_DIM, s), F32),
                 sds((m, A_W), BF16), sds((n, s // TQ, A_W, TQ), BF16), sds((m, A_W), BF16),
                 sds((m, B_W), BF16), sds((m, B_W), F32), sds((n, s // TQ, 2 * B_W, TQ), BF16),
                 sds((m, 2 * C_W), BF16), sds((n, s // TQ, 2 * HEAD_DIM, TQ), BF16),
                 sds((n, s // TQ, 2 * HEAD_DIM, TQ), BF16), sds((m, LANES), F32)]
    out_specs = [feat(A_W), pl.BlockSpec((tm * HEADS_A, LANES), lambda i: (i, 0)), feat(2 * B_W),
                 feat(4 * HEAD_DIM), feat(2 * HEAD_DIM),
                 row(A_W), tiles(A_W), row(A_W), row(B_W), row(B_W), tiles(2 * B_W),
                 row(2 * C_W), tiles(2 * HEAD_DIM), tiles(2 * HEAD_DIM), row(LANES)]
    return pl.pallas_call(
        _proj_prompt_body,
        out_shape=out_shape,
        grid=(m // tm,),
        in_specs=[row(D_MODEL), full((1, D_MODEL)), full((D_MODEL, MAIN_W)), full((_KT_ROWS, D_MODEL)),
                  full((D_MODEL, LANES)), tab, tab, tab_t, tab_t],
        out_specs=out_specs,
        compiler_params=_params(1),
        name="in_proj_rope_prompt",
    )(h, g_mix.reshape(1, D_MODEL), w_main, w_t, w_gate, cos, sin, cos_t, sin_t)


def _rank_lower(x, n, width_iota):
    rank = jnp.zeros(x.shape, F32)
    for bp in range(n):
        col = x[:, bp:bp + 1]
        tie = jnp.where(bp < width_iota, 1.0, 0.0)
        rank = rank + jnp.where(col > x, 1.0, jnp.where(col == x, tie, 0.0))
    return rank


def _rank_lower_t(x, n, row_iota):
    rank = jnp.zeros(x.shape, F32)
    for bp in range(n):
        row = x[bp:bp + 1, :]
        tie = jnp.where(bp < row_iota, 1.0, 0.0)
        rank = rank + jnp.where(row > x, 1.0, jnp.where(row == x, tie, 0.0))
    return rank


def _head_rms_scale(o, gh, lam_init):
    return _rms(o) * gh * (1.0 - lam_init)


def _attn_a_prompt_body(q_ref, kt_ref, v_ref, lam_ref, gh_ref, o_ref, *, lam_init):
    i = pl.program_id(1)
    tq = q_ref.shape[0]
    lam = _lam_value(lam_ref, lam_init)
    gh = gh_ref[...]
    lane = _iota((tq, LANES), 1)
    causal = _iota((tq, tq), 1) <= _iota((tq, tq), 0)
    head_sl = [slice(h * LANES, (h + 1) * LANES) for h in range(HEADS_A)]
    chains = [(h, c) for h in range(HEADS_A) for c in range(2)]
    qms = []
    for h, c in chains:
        q2 = q_ref[:, head_sl[h]]
        in_c = (lane >= c * HEAD_DIM) & (lane < (c + 1) * HEAD_DIM)
        qms.append(jnp.where(in_c, q2, jnp.zeros_like(q2)))

    def rows(j):
        return pl.ds(pl.multiple_of(j * tq, tq), tq)

    ones = jnp.ones((tq, LANES), BF16)

    def v_ext(j, h):
        return jnp.concatenate([v_ref[rows(j), head_sl[h]], ones], axis=1)

    for h0 in range(0, HEADS_A, A_HEADS_PER_LOOP):
        group = [(2 * h + c, h) for h in range(h0, h0 + A_HEADS_PER_LOOP) for c in range(2)]
        state = []
        for idx, h in group:
            s = jnp.where(causal, _nn(qms[idx], kt_ref[i, head_sl[h], :]), NEG_INF)
            m = jnp.max(s, axis=-1, keepdims=True)
            p = jnp.exp((s - m).astype(BF16))
            state += [m, _nn(p, v_ext(i, h))]

        def body(j, carry, group=group):
            out = []
            for k, (idx, h) in enumerate(group):
                m, acc = carry[2 * k:2 * k + 2]
                s = _nn(qms[idx], kt_ref[j, head_sl[h], :])
                m_new = jnp.maximum(m, jnp.max(s, axis=-1, keepdims=True))
                p = jnp.exp((s - m_new).astype(BF16))
                out += [m_new, jnp.exp(m - m_new) * acc + _nn(p, v_ext(j, h))]
            return tuple(out)

        state = lax.fori_loop(0, i, body, tuple(state))
        for k in range(A_HEADS_PER_LOOP):
            a0, a1 = state[4 * k + 1], state[4 * k + 3]
            o0 = a0[:, :LANES] / a0[:, LANES:]
            o1 = a1[:, :LANES] / a1[:, LANES:]
            o_ref[:, head_sl[h0 + k]] = _head_rms_scale(o0 - lam * o1, gh, lam_init).astype(BF16)


def _attn_a_prompt(q16, kt16, v16, lam_p, g_head, n, s, lidx):
    nq = s // TQ
    qspec = pl.BlockSpec((TQ, A_W), lambda b, i: (b * nq + i, 0))
    return pl.pallas_call(
        functools.partial(_attn_a_prompt_body, lam_init=_lam_init(lidx)),
        out_shape=jax.ShapeDtypeStruct((n * s, A_W), BF16),
        grid=(n, nq),
        in_specs=[qspec,
                  pl.BlockSpec((None, nq, A_W, TQ), lambda b, i: (b, 0, 0, 0)),
                  pl.BlockSpec((s, A_W), lambda b, i: (b, 0)),
                  pl.BlockSpec((4, HEAD_DIM), lambda b, i: (0, 0)),
                  pl.BlockSpec((1, 2 * HEAD_DIM), lambda b, i: (0, 0))],
        out_specs=qspec,
        compiler_params=_params(2),
        name="diff_attn_prompt",
    )(q16, kt16, v16, lam_p, g_head.reshape(1, 2 * HEAD_DIM))


def _block_means_t(blocks):
    feats = blocks[0].shape[0]
    lane = _iota((feats, LANES), 1)
    out = jnp.zeros((feats, LANES), F32)
    for b, blk in enumerate(blocks):
        out = jnp.where(lane == b, jnp.sum(blk, axis=1, keepdims=True) * (1.0 / MOBA_BLOCK), out)
    return out


def _attn_b_prompt_body(q_ref, q32_ref, k32t_ref, kvt_ref, o_ref, kmean_ref):
    i = pl.program_id(1)
    tq = q_ref.shape[0]
    nb = k32t_ref.shape[1] // MOBA_BLOCK

    @pl.when(i == 0)
    def _():
        kmean_ref[...] = _block_means_t([k32t_ref[:, b * MOBA_BLOCK:(b + 1) * MOBA_BLOCK] for b in range(nb)])

    lane = _iota((tq, LANES), 1)
    causal = _iota((tq, tq), 1) <= _iota((tq, tq), 0)
    nb_rows = -(-nb // 8) * 8
    blk_t = _iota((nb_rows, tq), 0)
    past_t = blk_t < i
    ksl = [slice((h // 2) * LANES, (h // 2 + 1) * LANES) for h in range(HEADS_B)]
    vsl = [slice(B_W + (h // 2) * LANES, B_W + (h // 2 + 1) * LANES) for h in range(HEADS_B)]
    ones_t = jnp.ones((HEAD_DIM, tq), BF16)

    def v_ones(j, h):
        vt = kvt_ref[j, vsl[h], :]
        if h % 2 == 0:
            return jnp.concatenate([vt[0:HEAD_DIM], ones_t], axis=0)
        return jnp.concatenate([ones_t, vt[HEAD_DIM:2 * HEAD_DIM]], axis=0)

    qms, chosen, state = [], [], []
    for h in range(HEADS_B):
        q2 = q_ref[:, ksl[h]]
        in_h = (lane >= (h % 2) * HEAD_DIM) & (lane < (h % 2 + 1) * HEAD_DIM)
        qms.append(jnp.where(in_h, q2, jnp.zeros_like(q2)))
        gate = _nn_precise(jnp.where(in_h, q32_ref[:, ksl[h]], 0.0), kmean_ref[ksl[h], :])
        gate_t = jnp.where(past_t, gate.T[0:nb_rows], NEG_INF)
        pick_t = jnp.where((_rank_lower_t(gate_t, nb, blk_t) < MOBA_TOPK) & past_t, 1.0, 0.0)
        chosen.append(jnp.concatenate([pick_t, jnp.zeros((LANES - nb_rows, tq), F32)], axis=0).T)
        s = jnp.where(causal, _nn(qms[h], kvt_ref[i, ksl[h], :]), NEG_INF)
        m = jnp.max(s, axis=-1, keepdims=True)
        state += [m, _nt(jnp.exp((s - m).astype(BF16)), v_ones(i, h))]

    def body(j, carry):
        out = []
        for h in range(HEADS_B):
            m, acc = carry[2 * h:2 * h + 2]
            use = jnp.max(jnp.where(lane == j, chosen[h], 0.0), axis=1, keepdims=True)
            s = jnp.where(use > 0.5, _nn(qms[h], kvt_ref[j, ksl[h], :]), NEG_INF)
            m_new = jnp.maximum(m, jnp.max(s, axis=-1, keepdims=True))
            p = jnp.exp((s - m_new).astype(BF16))
            out += [m_new, jnp.exp(m - m_new) * acc + _nt(p, v_ones(j, h))]
        return tuple(out)

    state = lax.fori_loop(0, i, body, tuple(state))
    for pair in range(HEADS_B // 2):
        a0, a1 = state[4 * pair + 1], state[4 * pair + 3]
        o0 = a0 / pltpu.roll(a0, HEAD_DIM, 1)
        o1 = a1 / pltpu.roll(a1, HEAD_DIM, 1)
        o_ref[:, ksl[2 * pair]] = jnp.where(lane < HEAD_DIM, o0, o1).astype(BF16)


def _attn_b_prompt(q16, q32, bkvt32, bkvt16, n, s):
    nq = s // TQ
    assert TQ == MOBA_BLOCK and s // MOBA_BLOCK <= LANES
    qspec = pl.BlockSpec((TQ, B_W), lambda b, i: (b * nq + i, 0))
    return pl.pallas_call(
        _attn_b_prompt_body,
        out_shape=jax.ShapeDtypeStruct((n * s, B_W), BF16),
        grid=(n, nq),
        in_specs=[qspec, qspec,
                  pl.BlockSpec((None, B_W, s), lambda b, i: (b, 0, 0)),
                  pl.BlockSpec((None, nq, 2 * B_W, TQ), lambda b, i: (b, 0, 0, 0))],
        out_specs=qspec,
        scratch_shapes=[pltpu.VMEM((B_W, LANES), F32)],
        compiler_params=_params(2),
        name="moba_attn_prompt",
    )(q16, q32, bkvt32, bkvt16)


_CMP_ROWS = 128


def _compress_core(xs_ref, pe_ref, w1_ref, w2_ref):
    half = CMP_LEN // 2
    n_chunk = xs_ref.shape[0] // CMP_STRIDE
    acc = [None, None]
    def rows(r):
        return (xs_ref[pl.ds(r % half, n_chunk, stride=CMP_STRIDE), :] + pe_ref[r:r + 1, :]).astype(BF16)

    for r in range(0, CMP_LEN, 2):
        t = _nn(jnp.concatenate([rows(r), rows(r + 1)], axis=1), w1_ref[r // 2])
        acc[r // half] = t if acc[r // half] is None else acc[r // half] + t
    pre = acc[0] + pltpu.roll(acc[1], n_chunk - 1, 0)
    hid = jax.nn.gelu(pre).astype(BF16)
    return _nn(hid, w2_ref[...]).astype(BF16)


def _compress_prompt_body(xt_ref, pe_ref, w1_ref, w2_ref, o_ref, xs_ref):
    for j in range(xt_ref.shape[1] // LANES):
        xs_ref[j * LANES:(j + 1) * LANES, :] = xt_ref[:, j * LANES:(j + 1) * LANES].T
    o_ref[...] = _compress_core(xs_ref, pe_ref, w1_ref, w2_ref)


def _compress_weights(w_cmp1, w_cmp2, cmp_pos):
    w1 = w_cmp1.reshape(2, CMP_LEN, HEAD_DIM, CMP_HIDDEN)
    z1 = jnp.zeros((CMP_LEN, HEAD_DIM, CMP_HIDDEN), F32)
    top = jnp.concatenate([w1[0], z1], axis=2)
    bot = jnp.concatenate([z1, w1[1]], axis=2)
    w1c = jnp.concatenate([top, bot], axis=1).astype(BF16)
    w1c = w1c.reshape(CMP_LEN // 2, 2 * LANES, 2 * CMP_HIDDEN)
    z2 = jnp.zeros((CMP_HIDDEN, HEAD_DIM), F32)
    w2c = jnp.concatenate([jnp.concatenate([w_cmp2[0], z2], axis=1),
                           jnp.concatenate([z2, w_cmp2[1]], axis=1)], axis=0).astype(BF16)
    pe = jnp.concatenate([cmp_pos[0], cmp_pos[1]], axis=1)
    return pe, w1c, w2c


def _const_specs(index):
    return [pl.BlockSpec((CMP_LEN, LANES), index(2)),
            pl.BlockSpec((CMP_LEN // 2, 2 * LANES, 2 * CMP_HIDDEN), index(3)),
            pl.BlockSpec((2 * CMP_HIDDEN, LANES), index(2))]


def _compress_prompt(ckvt32, cw, n, s):
    assert s // CMP_STRIDE == _CMP_ROWS
    zero = lambda nd: (lambda b: (0,) * nd)
    return pl.pallas_call(
        _compress_prompt_body,
        out_shape=jax.ShapeDtypeStruct((n * _CMP_ROWS, LANES), BF16),
        grid=(n,),
        in_specs=[pl.BlockSpec((None, LANES, s), lambda b: (b, 0, 0))] + _const_specs(zero),
        out_specs=pl.BlockSpec((_CMP_ROWS, LANES), lambda b: (b, 0)),
        scratch_shapes=[pltpu.VMEM((s, LANES), F32)],
        compiler_params=_params(1),
        name="nsa_compress_prompt",
    )(ckvt32, *cw)


def _page_index(l, g, group, j, row_block, b, pt):
    return (l, pt[b * group + g, j], row_block, 0)


def _page_specs(l, n_pages, rows, row_block=0, group=1):
    return [pl.BlockSpec((None, None, rows, PAGE_SIZE), functools.partial(_page_index, l, g, group, j, row_block))
            for g in range(group) for j in range(n_pages)]


def _overlap_matrix(t_len):
    n_cmp = (t_len - CMP_LEN) // CMP_STRIDE + 1
    nsb = -(-t_len // SEL_BLOCK)
    starts = np.arange(n_cmp) * CMP_STRIDE
    sb = np.arange(nsb) * SEL_BLOCK
    ov = np.clip(np.minimum(starts[:, None] + CMP_LEN, sb[None, :] + SEL_BLOCK)
                 - np.maximum(starts[:, None], sb[None, :]), 0, None) / CMP_STRIDE
    out = np.zeros((_CMP_ROWS, LANES), np.float32)
    out[:n_cmp, :nsb] = ov
    return jnp.asarray(out, BF16), nsb


def _expand_matrix(s):
    nt = s // TQ
    e = np.zeros((nt, LANES, TQ), np.float32)
    for j in range(nt):
        for k in range(TQ):
            e[j, (j * TQ + k) // SEL_BLOCK, k] = 1.0
    return jnp.asarray(e, BF16)


def _nsa_flags(p_sum, ovl, own, lane, nsb):
    imp = None
    for part in _split3(p_sum):
        t = _nn(part, ovl)
        imp = t if imp is None else imp + t
    forced = (lane == 0) | (lane == own) | (lane == own - 1)
    imp = jnp.where(lane > own, NEG_INF, jnp.where(forced, jnp.inf, imp))
    rank = _rank_lower(imp, nsb, lane)
    return jnp.where((rank < SEL_TOPK) & (lane <= own), 1.0, 0.0)


def _nsa_flags_t(p_sum, ovl, pos0, nsb):
    tq = p_sum.shape[0]
    imp = None
    for part in _split3(p_sum):
        t = _nn(part, ovl)
        imp = t if imp is None else imp + t
    rows = -(-nsb // 8) * 8
    imp_t = imp.T[0:rows]
    blk = _iota((rows, tq), 0)
    own = (pos0 + _iota((rows, tq), 1)) // SEL_BLOCK
    forced = (blk == 0) | (blk == own) | (blk == own - 1)
    imp_t = jnp.where(blk > own, NEG_INF, jnp.where(forced, jnp.inf, imp_t))
    rank = _rank_lower_t(imp_t, nsb, blk)
    flag_t = jnp.where((rank < SEL_TOPK) & (blk <= own), 1.0, 0.0)
    flag_t = jnp.concatenate([flag_t, jnp.zeros((LANES - rows, tq), F32)], axis=0)
    return flag_t.T


def _masked_probs(s, mask):
    s = jnp.where(mask, s, NEG_INF)
    m = jnp.max(s, axis=-1, keepdims=True)
    m = jnp.where(m > NEG_INF, m, 0.0)
    e = jnp.exp(s - m)
    d = jnp.sum(e, axis=-1, keepdims=True)
    return e / jnp.where(d > 0, d, 1.0)


def _attn_c_prompt_body(qz_ref, gate_ref, selt_ref, wint_ref, kvc_ref, ovl_ref, exp_ref, o_ref, *, nsb):
    i = pl.program_id(1)
    tq = gate_ref.shape[0]
    nh = HEADS_C
    q4 = jnp.concatenate([qz_ref[:, h * LANES:(h + 1) * LANES] for h in range(nh)], axis=0)
    lane = _iota((tq, LANES), 1)
    pos = i * tq + _iota((tq, LANES), 0)
    rr = _iota((tq, tq), 0)
    cc = _iota((tq, tq), 1)
    causal = cc <= rr

    kvc = kvc_ref[...]
    cmp_ok = (CMP_STRIDE * lane + (CMP_LEN - 1)) <= pos
    s = _nt(q4, kvc).reshape(nh, tq, LANES)
    p = _masked_probs(s, cmp_ok[None])
    o_cmp = _nn(p.reshape(nh * tq, LANES).astype(BF16), kvc)
    p_sum = p[0] + p[1] + p[2] + p[3]

    flag = _nsa_flags_t(p_sum, ovl_ref[...], i * tq, nsb).astype(BF16)

    ones_t = jnp.ones((HEAD_DIM, tq), BF16)

    def pv_tile(kj):
        return jnp.concatenate([ones_t, kj[HEAD_DIM:2 * HEAD_DIM]], axis=0)

    kd = selt_ref[i]
    ok = (_nn(flag, exp_ref[i]) > 0.5) & causal
    s = jnp.where(ok[None], _nn(q4, kd).reshape(nh, tq, tq), NEG_INF)
    m = jnp.max(s, axis=-1, keepdims=True)
    p = jnp.exp((s - m).astype(BF16))
    acc = _nt(p.reshape(nh * tq, tq), pv_tile(kd)).reshape(nh, tq, LANES)

    def body(j, carry):
        m, acc = carry
        kj = selt_ref[j]
        ok = _nn(flag, exp_ref[j]) > 0.5
        s = jnp.where(ok[None], _nn(q4, kj).reshape(nh, tq, tq), NEG_INF)
        m_new = jnp.maximum(m, jnp.max(s, axis=-1, keepdims=True))
        p = jnp.exp((s - m_new).astype(BF16))
        pv = _nt(p.reshape(nh * tq, tq), pv_tile(kj)).reshape(nh, tq, LANES)
        return m_new, jnp.exp(m - m_new) * acc + pv

    m, acc = lax.fori_loop(0, i, body, (m, acc))
    o_sel = acc / pltpu.roll(acc, HEAD_DIM, 2)

    assert WINDOW == 2 * tq
    w2 = wint_ref[jnp.maximum(i - 2, 0)]
    w1 = wint_ref[jnp.maximum(i - 1, 0)]
    w0 = wint_ref[i]
    s2 = jnp.where(((cc >= rr) & (i >= 2))[None], _nn(q4, w2).reshape(nh, tq, tq), NEG_INF)
    s1 = jnp.where(i >= 1, _nn(q4, w1).reshape(nh, tq, tq), NEG_INF)
    s0 = jnp.where(causal[None], _nn(q4, w0).reshape(nh, tq, tq), NEG_INF)
    s_all = jnp.concatenate([s2, s1, s0], axis=-1)
    m = jnp.max(s_all, -1, keepdims=True)
    e = jnp.exp((s_all - m).astype(BF16)).reshape(nh * tq, 3 * tq)
    acc = (_nt(e[:, 0:tq], pv_tile(w2)) + _nt(e[:, tq:2 * tq], pv_tile(w1))
           + _nt(e[:, 2 * tq:3 * tq], pv_tile(w0)))
    o_win = (acc / pltpu.roll(acc, HEAD_DIM, 1)).reshape(nh, tq, LANES)

    o_cmp = o_cmp.reshape(nh, tq, LANES)
    g = gate_ref[...]
    heads = []
    for h in range(nh):
        heads.append(g[:, 3 * h:3 * h + 1] * o_cmp[h] + g[:, 3 * h + 1:3 * h + 2] * o_sel[h]
                     + g[:, 3 * h + 2:3 * h + 3] * o_win[h])
    for pair in range(nh // 2):
        both = jnp.where(lane < HEAD_DIM, pltpu.roll(heads[2 * pair], HEAD_DIM, 1), heads[2 * pair + 1])
        o_ref[:, pair * LANES:(pair + 1) * LANES] = both.astype(BF16)


def _attn_c_prompt(qz16, gates, selt16, wint16, kvcmp16, n, s):
    nq = s // TQ
    ovl, nsb = _overlap_matrix(s)
    expand = _expand_matrix(s)
    qrow = lambda w: pl.BlockSpec((TQ, w), lambda b, i: (b * nq + i, 0))
    tiles = pl.BlockSpec((None, nq, 2 * HEAD_DIM, TQ), lambda b, i: (b, 0, 0, 0))
    return pl.pallas_call(
        functools.partial(_attn_c_prompt_body, nsb=nsb),
        out_shape=jax.ShapeDtypeStruct((n * s, C_W), BF16),
        grid=(n, nq),
        in_specs=[qrow(2 * C_W), qrow(LANES), tiles, tiles,
                  pl.BlockSpec((_CMP_ROWS, LANES), lambda b, i: (b, 0)),
                  pl.BlockSpec((_CMP_ROWS, LANES), lambda b, i: (0, 0)),
                  pl.BlockSpec((nq, LANES, TQ), lambda b, i: (0, 0, 0))],
        out_specs=qrow(C_W),
        compiler_params=_params(2),
        name="nsa_attn_prompt",
    )(qz16, gates, selt16, wint16, kvcmp16, ovl, expand)


def _pad_page(x):
    rows, w = x.shape
    return jnp.concatenate([x, jnp.zeros((PAGE_SIZE - rows, w), x.dtype)], axis=0)


def _new_page_mask(n_rows, nq):
    r = _iota((n_rows, PAGE_SIZE), 0) % nq
    t = _iota((n_rows, PAGE_SIZE), 1)
    return t <= r


def _dec_a_body(pt_ref, q_ref, kn_ref, vn_ref, lam_ref, gh_ref, *refs, lam_init):
    n_pages = (len(refs) - 1) // 2
    kp, vp, o_ref = refs[:n_pages], refs[n_pages:2 * n_pages], refs[-1]
    nq = q_ref.shape[0]
    rows = 2 * HEADS_A * nq
    per_head = 2 * nq
    lam = _lam_value(lam_ref, lam_init)
    qt = jnp.concatenate([q_ref[...]] * (2 * HEADS_A), axis=0)
    diag = (_iota((rows, A_W), 0) // nq) == (_iota((rows, A_W), 1) // HEAD_DIM)
    qbd = jnp.where(diag, qt, jnp.zeros_like(qt))
    kn = _pad_page(kn_ref[...]).astype(BF16)
    vn = _pad_page(vn_ref[...]).astype(BF16)
    s_new = jnp.where(_new_page_mask(rows, nq), _nt(qbd, kn), NEG_INF)
    scores = [_nn(qbd, kp[j][...].astype(BF16)) for j in range(n_pages)]
    m = jnp.max(s_new, axis=1, keepdims=True)
    for s in scores:
        m = jnp.maximum(m, jnp.max(s, axis=1, keepdims=True))
    e = jnp.exp(s_new - m)
    d = jnp.sum(e, axis=1, keepdims=True)
    e = e.astype(BF16)
    accs = [_nn(e[h * per_head:(h + 1) * per_head], vn[:, h * LANES:(h + 1) * LANES]) for h in range(HEADS_A)]
    for j in range(n_pages):
        e = jnp.exp(scores[j] - m)
        d = d + jnp.sum(e, axis=1, keepdims=True)
        e = e.astype(BF16)
        for h in range(HEADS_A):
            vh = vp[j][pl.ds(h, PAGE_SIZE, stride=HEADS_A), :].astype(BF16)
            accs[h] = accs[h] + _nn(e[h * per_head:(h + 1) * per_head], vh)
    gh = gh_ref[...]
    for h in range(HEADS_A):
        on = accs[h] / d[h * per_head:(h + 1) * per_head]
        o = on[0:nq] - lam * on[nq:2 * nq]
        o_ref[:, h * LANES:(h + 1) * LANES] = _head_rms_scale(o, gh, lam_init).astype(BF16)


def _dec_a(q16, kn32, vn32, lam_p, g_head, cache_kt, cache_v4, page_table, l, nq):
    n, n_pages = page_table.shape
    row = lambda w: pl.BlockSpec((nq, w), lambda b, pt: (b, 0))
    grid_spec = pltpu.PrefetchScalarGridSpec(
        num_scalar_prefetch=1, grid=(n,),
        in_specs=[row(A_W), row(A_W), row(A_W),
                  pl.BlockSpec((4, HEAD_DIM), lambda b, pt: (0, 0)),
                  pl.BlockSpec((1, 2 * HEAD_DIM), lambda b, pt: (0, 0))]
                 + _page_specs(l, n_pages, A_W) + _page_specs(l, n_pages, A_W),
        out_specs=row(A_W))
    return pl.pallas_call(
        functools.partial(_dec_a_body, lam_init=_lam_init(l)),
        out_shape=jax.ShapeDtypeStruct((n * nq, A_W), BF16),
        grid_spec=grid_spec,
        compiler_params=_params(1),
        name="diff_attn_decode",
    )(page_table, q16, kn32, vn32, lam_p, g_head.reshape(1, 2 * HEAD_DIM),
      *([cache_kt] * n_pages), *([cache_v4] * n_pages))


def _dec_b_body(pt_ref, q_ref, q32_ref, kvn_ref, *refs, group):
    pages, o_ref = refs[:-1], refs[-1]
    n_pages = len(pages) // group
    nq = q_ref.shape[0] // group
    for g in range(group):
        r = slice(g * nq, (g + 1) * nq)
        o_ref[r, :] = _dec_b_one(q_ref[r, :], q32_ref[r, :], kvn_ref[r, :],
                                 pages[g * n_pages:(g + 1) * n_pages]).astype(BF16)


def _dec_b_one(q, q32, kvn_rows, pages):
    n_pages = len(pages)
    nq = q.shape[0]
    rows = HEADS_B * nq
    pages_per_blk = MOBA_BLOCK // PAGE_SIZE
    nb = n_pages // pages_per_blk
    diag = (_iota((rows, B_W), 0) // nq) == (_iota((rows, B_W), 1) // HEAD_DIM)
    qt = jnp.concatenate([q] * HEADS_B, axis=0)
    qbd = jnp.where(diag, qt, jnp.zeros_like(qt))
    qbd32 = jnp.where(diag, jnp.concatenate([q32] * HEADS_B, axis=0), 0.0)

    blocks = []
    for b in range(nb):
        blocks.append(jnp.concatenate([pages[j][0:B_W, :] for j in range(b * pages_per_blk, (b + 1) * pages_per_blk)],
                                      axis=1))
    kmean_t = _block_means_t(blocks)
    lane = _iota((rows, LANES), 1)
    gate = _nn_precise(qbd32, kmean_t)
    gate = jnp.where(lane < nb, gate, NEG_INF)
    chosen = jnp.where((_rank_lower(gate, nb, lane) < MOBA_TOPK) & (lane < nb), 1.0, 0.0)

    kvn = _pad_page(kvn_rows).astype(BF16)
    s_new = jnp.where(_new_page_mask(rows, nq), _nt(qbd, kvn[:, 0:B_W]), NEG_INF)
    scores = []
    for j in range(n_pages):
        b = j // pages_per_blk
        s = _nn(qbd, pages[j][0:B_W, :].astype(BF16))
        scores.append(jnp.where(chosen[:, b:b + 1] > 0.5, s, NEG_INF))
    m = jnp.max(s_new, axis=1, keepdims=True)
    for s in scores:
        m = jnp.maximum(m, jnp.max(s, axis=1, keepdims=True))
    e = jnp.exp(s_new - m)
    d = jnp.sum(e, axis=1, keepdims=True)
    acc = _nn(e.astype(BF16), kvn[:, B_W:2 * B_W])
    for j in range(n_pages):
        e = jnp.exp(scores[j] - m)
        d = d + jnp.sum(e, axis=1, keepdims=True)
        acc = acc + _nt(e.astype(BF16), pages[j][B_W:2 * B_W, :].astype(BF16))
    on = acc / d
    head_of_lane = _iota((nq, B_W), 1) // HEAD_DIM
    o = jnp.zeros((nq, B_W), F32)
    for h in range(HEADS_B):
        o = o + jnp.where(head_of_lane == h, on[h * nq:(h + 1) * nq], 0.0)
    return o


def _dec_b(q16, q32, kvn32, cache_kvt, page_table, l, nq):
    n, n_pages = page_table.shape
    group = DEC_GROUP
    row = lambda w: pl.BlockSpec((group * nq, w), lambda b, pt: (b, 0))
    grid_spec = pltpu.PrefetchScalarGridSpec(
        num_scalar_prefetch=1, grid=(n // group,),
        in_specs=[row(B_W), row(B_W), row(2 * B_W)] + _page_specs(l, n_pages, 2 * B_W, group=group),
        out_specs=row(B_W))
    return pl.pallas_call(
        functools.partial(_dec_b_body, group=group),
        out_shape=jax.ShapeDtypeStruct((n * nq, B_W), BF16),
        grid_spec=grid_spec,
        compiler_params=_params(1),
        name="moba_attn_decode",
    )(page_table, q16, q32, kvn32, *([cache_kvt] * (n_pages * group)))


def _dec_c_body(pt_ref, qz_ref, gate_ref, ckvn_ref, winn_ref, pe_ref, w1_ref, w2_ref, ovl_ref, exp_ref, st_ref,
                *refs, nsb, q0, group):
    pages, o_ref, xs_ref = refs[:-2], refs[-2], refs[-1]
    n_pages = len(pages) // group
    nq = gate_ref.shape[0] // group
    for j, pg in enumerate(pages):
        xs_ref[j * PAGE_SIZE:(j + 1) * PAGE_SIZE, :] = pg[0:LANES, :].T
    kvc_all = _compress_core(xs_ref, pe_ref, w1_ref, w2_ref)
    ovl = ovl_ref[...]
    expand = exp_ref[...]
    for g in range(group):
        r = slice(g * nq, (g + 1) * nq)
        tiles = [pg[LANES:2 * LANES, :] for pg in pages[g * n_pages:(g + 1) * n_pages]]
        _dec_c_one(qz_ref[r, :], gate_ref[r, :], ckvn_ref[r, :], winn_ref[r, :],
                   kvc_all[g * _CMP_ROWS:(g + 1) * _CMP_ROWS], ovl, expand, st_ref[g], tiles, o_ref, r, nsb, q0)


def _dec_c_one(qz, g, ckvn_rows, winn_rows, kvc, ovl, expand, st32, tiles, o_ref, out_rows, nsb, q0):
    n_pages = len(tiles)
    nq = g.shape[0]
    nh = HEADS_C
    rows = nh * nq
    q4 = jnp.concatenate([qz[:, h * LANES:(h + 1) * LANES] for h in range(nh)], axis=0)
    lane = _iota((nq, LANES), 1)
    pos = q0 + _iota((nq, LANES), 0)
    lane4 = _iota((rows, LANES), 1)
    qrow4 = _iota((rows, LANES), 0) % nq
    new_ok = _new_page_mask(rows, nq)

    cmp_ok = (CMP_STRIDE * lane4 + (CMP_LEN - 1)) <= (q0 + qrow4)
    p = _masked_probs(_nt(q4, kvc), cmp_ok)
    o_cmp = _nn(p.astype(BF16), kvc)
    p_sum = p[0:nq]
    for h in range(1, nh):
        p_sum = p_sum + p[h * nq:(h + 1) * nq]

    own = pos // SEL_BLOCK
    flag = _nsa_flags(p_sum, ovl, own, lane, nsb)
    flag4 = jnp.concatenate([flag] * nh, axis=0)
    blk_per_page = PAGE_SIZE // SEL_BLOCK
    assert blk_per_page == 2

    ckvn = _pad_page(ckvn_rows).astype(BF16)
    kn = ckvn[:, LANES:2 * LANES]
    own_blk = n_pages * blk_per_page
    s_new = jnp.where(new_ok & (flag4[:, own_blk:own_blk + 1] > 0.5), _nt(q4, kn), NEG_INF)
    kt_all = jnp.concatenate([t.astype(BF16) for t in tiles], axis=1)
    ok = _nn(flag4.astype(BF16), expand) > 0.5
    s_old = jnp.where(ok, _nn(q4, kt_all), NEG_INF)
    m = jnp.maximum(jnp.max(s_new, axis=1, keepdims=True), jnp.max(s_old, axis=1, keepdims=True))
    e_new = jnp.exp(s_new - m)
    e_old = jnp.exp(s_old - m)
    d = jnp.sum(e_new, axis=1, keepdims=True) + jnp.sum(e_old, axis=1, keepdims=True)
    acc = _nn(e_new.astype(BF16), kn) + _nt(e_old.astype(BF16), kt_all)
    o_sel = acc / d

    wn = _pad_page(winn_rows).astype(BF16)
    s_new = jnp.where(new_ok, _nt(q4, wn), NEG_INF)
    st = st32.astype(BF16)
    key = _iota((rows, WINDOW), 1)
    ok = key >= (_iota((rows, WINDOW), 0) % nq)
    s_old = jnp.where(ok, _nn(q4, st), NEG_INF)
    m = jnp.maximum(jnp.max(s_new, axis=1, keepdims=True), jnp.max(s_old, axis=1, keepdims=True))
    e_new = jnp.exp(s_new - m)
    e_old = jnp.exp(s_old - m)
    d = jnp.sum(e_new, axis=1, keepdims=True) + jnp.sum(e_old, axis=1, keepdims=True)
    o_win = (_nn(e_new.astype(BF16), wn) + _nt(e_old.astype(BF16), st)) / d

    heads = []
    for h in range(nh):
        r = slice(h * nq, (h + 1) * nq)
        heads.append(g[:, 3 * h:3 * h + 1] * o_cmp[r] + g[:, 3 * h + 1:3 * h + 2] * o_sel[r]
                     + g[:, 3 * h + 2:3 * h + 3] * o_win[r])
    for pair in range(nh // 2):
        both = jnp.where(lane < HEAD_DIM, pltpu.roll(heads[2 * pair], HEAD_DIM, 1), heads[2 * pair + 1])
        o_ref[out_rows, pair * LANES:(pair + 1) * LANES] = both.astype(BF16)


def _dec_c(qz16, gates, ckvn32, winn32, cw, state_t, cache_ct, page_table, l, nq, q0):
    n, n_pages = page_table.shape
    assert state_t.shape[3] == WINDOW and q0 >= WINDOW and n_pages * PAGE_SIZE // CMP_STRIDE == _CMP_ROWS
    ovl, nsb = _overlap_matrix(q0 + nq)
    keys = np.arange(n_pages * PAGE_SIZE)
    expand = jnp.asarray(np.arange(LANES)[:, None] == keys[None, :] // SEL_BLOCK, BF16)
    group = NSA_DEC_GROUP
    row = lambda w: pl.BlockSpec((group * nq, w), lambda b, pt: (b, 0))
    zero = lambda nd: (lambda b, pt: (0,) * nd)
    grid_spec = pltpu.PrefetchScalarGridSpec(
        num_scalar_prefetch=1, grid=(n // group,),
        in_specs=[row(2 * C_W), row(LANES), row(4 * HEAD_DIM), row(2 * HEAD_DIM)] + _const_specs(zero)
                 + [pl.BlockSpec((_CMP_ROWS, LANES), lambda b, pt: (0, 0)),
                    pl.BlockSpec((LANES, n_pages * PAGE_SIZE), lambda b, pt: (0, 0)),
                    pl.BlockSpec((None, group, 2 * HEAD_DIM, WINDOW), lambda b, pt: (l, b, 0, 0))]
                 + _page_specs(l, n_pages, 4 * HEAD_DIM, group=group),
        out_specs=row(C_W),
        scratch_shapes=[pltpu.VMEM((group * n_pages * PAGE_SIZE, LANES), F32)])
    return pl.pallas_call(
        functools.partial(_dec_c_body, nsb=nsb, q0=q0, group=group),
        out_shape=jax.ShapeDtypeStruct((n * nq, C_W), BF16),
        grid_spec=grid_spec,
        compiler_params=_params(1),
        name="nsa_attn_decode",
    )(page_table, qz16, gates, ckvn32, winn32, *cw, ovl, expand, state_t, *([cache_ct] * (n_pages * group)))


def _rope_tables(n_pos):
    inv = ROPE_THETA ** (-jnp.arange(0, HEAD_DIM, 2, dtype=F32) / HEAD_DIM)
    ang = jnp.arange(n_pos, dtype=F32)[:, None] * inv[None, :]
    cos, sin = jnp.cos(ang), jnp.sin(ang)
    cos128 = jnp.concatenate([cos, cos, cos, cos], axis=-1)
    sin128 = jnp.concatenate([-sin, sin, -sin, sin], axis=-1)
    return cos128, sin128, cos.T, sin.T


def kernel(x_prompt, x_sample, cache_a_k, cache_a_v, cache_b_kv, cache_c_kv, state_c_win, page_table,
           w_in, w_out, g_mix, g_ffn, w_ffn_gate, w_ffn_up, w_ffn_down, diff_lambda, g_diff_head,
           w_cmp1, w_cmp2, cmp_pos, g_final):
    n_p, s_p, _ = x_prompt.shape
    n_s, s_s, _ = x_sample.shape
    n_pages = page_table.shape[1]
    past_len = n_pages * cache_a_k.shape[2]
    n_phys = cache_a_k.shape[1]
    assert cache_a_k.shape[2] == PAGE_SIZE and s_p % TM_PROJ == 0 and TM_PROJ % s_s == 0

    cos, sin, cos_t, sin_t = _rope_tables(past_len + s_s)
    tabs_p = (cos[:s_p], sin[:s_p], cos_t[:, :s_p], sin_t[:, :s_p])
    reps = TM_PROJ // s_s
    cos_s = jnp.tile(cos[past_len:past_len + s_s], (reps, 1))
    sin_s = jnp.tile(sin[past_len:past_len + s_s], (reps, 1))

    ckt = cache_a_k.transpose(0, 1, 3, 4, 5, 2).reshape(DEPTH, n_phys, A_W, PAGE_SIZE)
    cv4 = cache_a_v.reshape(DEPTH, n_phys, PAGE_SIZE * HEADS_A, 2 * HEAD_DIM)
    cbt = cache_b_kv.transpose(0, 1, 3, 4, 5, 2).reshape(DEPTH, n_phys, 2 * B_W, PAGE_SIZE)
    cct = cache_c_kv.transpose(0, 1, 3, 4, 2).reshape(DEPTH, n_phys, 4 * HEAD_DIM, PAGE_SIZE)
    stt = state_c_win.transpose(0, 1, 3, 4, 2).reshape(DEPTH, n_s, 2 * HEAD_DIM, WINDOW)

    hp = x_prompt.reshape(n_p * s_p, D_MODEL)
    hs = x_sample.reshape(n_s * s_s, D_MODEL)
    ent_p, ent_s = [], []
    wg = w_ffn_gate.astype(BF16)
    wu = w_ffn_up.astype(BF16)
    wd = w_ffn_down.astype(BF16)
    wo = w_out.astype(BF16)
    for l in range(DEPTH):
        w_main = w_in[l][:, :MAIN_W].astype(BF16)
        w_t = jnp.concatenate([w_in[l][:, _O_KA:_O_KA + A_W], w_in[l][:, _O_KB:_O_KB + 2 * B_W],
                               w_in[l][:, _O_KVC:_O_KVC + KVC_W]], axis=1).T.astype(BF16)
        w_gate = jnp.pad(w_in[l][:, MAIN_W:], ((0, 0), (0, LANES - GATE_W))).astype(BF16)
        cw = _compress_weights(w_cmp1[l], w_cmp2[l], cmp_pos[l])
        last = l == DEPTH - 1

        hp = _ffn(hp, g_ffn[l, 0], wg, wu, wd, l, 0)
        (kat, va, bkvt, ckvt, wint, qa16, kat16, va16, qb16, qb32, bkvt16, qcz16, selt16, wint16, gates) = \
            _proj_prompt(hp, g_mix[l], w_main, w_t, w_gate, tabs_p, n_p, s_p)
        oa = _attn_a_prompt(qa16, kat16, va16, diff_lambda[l], g_diff_head[l], n_p, s_p, l)
        ob = _attn_b_prompt(qb16, qb32, bkvt, bkvt16, n_p, s_p)
        kvcmp = _compress_prompt(ckvt, cw, n_p, s_p)
        oc = _attn_c_prompt(qcz16, gates, selt16, wint16, kvcmp, n_p, s_p)
        hp = _ffn(hp, g_ffn[l, 1], wg, wu, wd, l, 1, (oa, ob, oc, wo), g_final if last else None)
        win_keep = min(WINDOW, s_p)
        ent_p.append((kat.reshape(n_p, HEADS_A, 2, HEAD_DIM, s_p).transpose(0, 4, 1, 2, 3),
                      va.reshape(n_p, s_p, HEADS_A, 2 * HEAD_DIM),
                      bkvt.reshape(n_p, 2, HEADS_B, HEAD_DIM, s_p).transpose(0, 4, 1, 2, 3),
                      ckvt.reshape(n_p, 4, HEAD_DIM, s_p).transpose(0, 3, 1, 2),
                      wint[:, :, s_p - win_keep:].reshape(n_p, 2, HEAD_DIM, win_keep).transpose(0, 3, 1, 2)))

        hs = _ffn(hs, g_ffn[l, 0], wg, wu, wd, l, 0)
        (ka, va, bkv, ckv, win, qa16, qb16, qb32, qcz16, gates) = _proj_sample(
            hs, g_mix[l], w_main, w_gate, cos_s, sin_s)
        oa = _dec_a(qa16, ka, va, diff_lambda[l], g_diff_head[l], ckt, cv4, page_table, l, s_s)
        ob = _dec_b(qb16, qb32, bkv, cbt, page_table, l, s_s)
        oc = _dec_c(qcz16, gates, ckv, win, cw, stt, cct, page_table, l, s_s, past_len)
        hs = _ffn(hs, g_ffn[l, 1], wg, wu, wd, l, 1, (oa, ob, oc, wo), g_final if last else None)
        win_new = win.reshape(n_s, s_s, 2, HEAD_DIM)
        win_all = jnp.concatenate([state_c_win[l], win_new], axis=1)
        ent_s.append((ka.reshape(n_s, s_s, HEADS_A, 2, HEAD_DIM), va.reshape(n_s, s_s, HEADS_A, 2 * HEAD_DIM),
                      bkv.reshape(n_s, s_s, 2, HEADS_B, HEAD_DIM), ckv.reshape(n_s, s_s, 4, HEAD_DIM),
                      win_all[:, win_all.shape[1] - min(WINDOW, win_all.shape[1]):]))

    st_ = lambda ents, i: jnp.stack([e[i] for e in ents], axis=0)
    return (hp.reshape(n_p, s_p, D_MODEL), hs.reshape(n_s, s_s, D_MODEL),
            st_(ent_p, 0), st_(ent_s, 0), st_(ent_p, 1), st_(ent_s, 1),
            st_(ent_p, 2), st_(ent_s, 2), st_(ent_p, 3), st_(ent_s, 3),
            st_(ent_p, 4), st_(ent_s, 4))
```

```python
import functools
import math

import numpy as np
import jax
import jax.numpy as jnp
from jax import lax
from jax.experimental import pallas as pl
from jax.experimental.pallas import tpu as pltpu

F32 = jnp.float32
BF16 = jnp.bfloat16

D_MODEL = 1024
DEPTH = 2
HEAD_DIM = 64
HEADS_A = 4
HEADS_B = 4
HEADS_C = 4
D_FF = 2816
ROPE_THETA = 10000.0
MOBA_BLOCK = 256
MOBA_TOPK = 3
CMP_LEN = 32
CMP_STRIDE = 16
CMP_HIDDEN = 4 * HEAD_DIM
SEL_BLOCK = 64
SEL_TOPK = 16
WINDOW = 512
RMS_EPS = 1e-6
PAGE_SIZE = 128

A_W = HEADS_A * 2 * HEAD_DIM
B_W = HEADS_B * HEAD_DIM
C_W = HEADS_C * HEAD_DIM
KVC_W = 6 * HEAD_DIM
GATE_W = 3 * HEADS_C
MAIN_W = 3 * A_W + 3 * B_W + C_W + KVC_W
LANES = 128
QK_SCALE = HEAD_DIM ** -0.5
NEG_INF = float("-inf")
VMEM_LIMIT = 56 * 1024 * 1024

TM_FFN = 512
TF_FFN = 256
TM_PROJ = 512
TQ = 256
NSA_DEC_GROUP = 2
DEC_GROUP = 2
A_HEADS_PER_LOOP = 4

_O_QA, _O_KA, _O_VA = 0, A_W, 2 * A_W
_O_QB = 3 * A_W
_O_KB, _O_VB = _O_QB + B_W, _O_QB + 2 * B_W
_O_QC = _O_QB + 3 * B_W
_O_KVC = _O_QC + C_W


def _nn(a, b):
    return jnp.dot(a, b, preferred_element_type=F32)


def _nt(a, b):
    return lax.dot_general(a, b, (((1,), (1,)), ((), ())), preferred_element_type=F32)


def _split3(x):
    hi = x.astype(BF16)
    r1 = x - hi.astype(F32)
    mid = r1.astype(BF16)
    lo = (r1 - mid.astype(F32)).astype(BF16)
    return hi, mid, lo


def _nn_precise(a, b):
    a_hi, a_mid, _ = _split3(a)
    b_hi, b_mid, _ = _split3(b)
    return _nn(a_hi, b_hi) + (_nn(a_hi, b_mid) + _nn(a_mid, b_hi))


def _rms(x):
    return x * lax.rsqrt(jnp.mean(x * x, axis=-1, keepdims=True) + RMS_EPS)


def _iota(shape, dim):
    return lax.broadcasted_iota(jnp.int32, shape, dim)


def _params(n_axes):
    return pltpu.CompilerParams(dimension_semantics=("arbitrary",) * n_axes,
                                vmem_limit_bytes=VMEM_LIMIT)


def _lam_value(lam_ref, lam_init):
    lp = lam_ref[...]
    a = jnp.sum(lp[0:1] * lp[1:2], axis=1, keepdims=True)
    b = jnp.sum(lp[2:3] * lp[3:4], axis=1, keepdims=True)
    return jnp.exp(a) - jnp.exp(b) + lam_init


def _lam_init(lidx):
    return 0.8 - 0.6 * math.exp(-0.3 * lidx)


def _ffn_body(*refs, mixed, final):
    refs = list(refs)
    x_ref = refs.pop(0)
    x = x_ref[...]
    if mixed:
        oa_ref, ob_ref, oc_ref, wo_ref = refs[:4]
        refs = refs[4:]
        x = x + _nn(oa_ref[...], wo_ref[0:A_W, :])
        x = x + _nn(ob_ref[...], wo_ref[A_W:A_W + B_W, :])
        x = x + _nn(oc_ref[...], wo_ref[A_W + B_W:, :])
    g_ref, wg_ref, wu_ref, wd_ref = refs[:4]
    o_ref = refs[-1]
    xn = (_rms(x) * g_ref[...]).astype(BF16)
    acc = jnp.zeros_like(x)
    for f in range(D_FF // TF_FFN):
        sl = slice(f * TF_FFN, (f + 1) * TF_FFN)
        g = _nn(xn, wg_ref[:, sl])
        u = _nn(xn, wu_ref[:, sl])
        a = (g * jax.nn.sigmoid(g) * u).astype(BF16)
        acc = acc + _nn(a, wd_ref[sl, :])
    y = x + 0.5 * acc
    if final:
        y = _rms(y) * refs[4][...]
    o_ref[...] = y


def _ffn(x, g, wg, wu, wd, l, k, mix=None, g_final=None):
    m = x.shape[0]
    row = lambda w: pl.BlockSpec((TM_FFN, w), lambda i: (i, 0))
    vec = pl.BlockSpec((1, D_MODEL), lambda i: (0, 0))
    full = lambda r, c: pl.BlockSpec((None, None, r, c), lambda i: (l, k, 0, 0))
    in_specs, args = [row(D_MODEL)], [x]
    if mix is not None:
        oa, ob, oc, w_out = mix
        in_specs += [row(A_W), row(B_W), row(C_W), pl.BlockSpec((None, D_MODEL, D_MODEL), lambda i: (l, 0, 0))]
        args += [oa, ob, oc, w_out]
    in_specs += [vec, full(D_MODEL, D_FF), full(D_MODEL, D_FF), full(D_FF, D_MODEL)]
    args += [g.reshape(1, D_MODEL), wg, wu, wd]
    if g_final is not None:
        in_specs.append(vec)
        args.append(g_final.reshape(1, D_MODEL))
    return pl.pallas_call(
        functools.partial(_ffn_body, mixed=mix is not None, final=g_final is not None),
        out_shape=jax.ShapeDtypeStruct((m, D_MODEL), F32),
        grid=(m // TM_FFN,),
        in_specs=in_specs,
        out_specs=row(D_MODEL),
        compiler_params=_params(1),
        name="ffn_half",
    )(*args)


def _rope_rows(lane_shape):
    lane = _iota(lane_shape, 1)
    return (lane % HEAD_DIM) < (HEAD_DIM // 2), lane < HEAD_DIM


def _rope_lanes(x, cos, sin, lo32):
    sh = jnp.where(lo32, pltpu.roll(x, LANES - HEAD_DIM // 2, 1), pltpu.roll(x, HEAD_DIM // 2, 1))
    return x * cos + sh * sin


def _proj_queries(u, w_ref, cos, sin, lo32, lo64, qa16_ref, qb16_ref, qb32_ref, qcz16_ref):
    p = _nn(u, w_ref[:, _O_QA:_O_QA + A_W])
    for k in range(A_W // LANES):
        qa16_ref[:, k * LANES:(k + 1) * LANES] = (
            _rope_lanes(p[:, k * LANES:(k + 1) * LANES], cos, sin, lo32) * QK_SCALE).astype(BF16)
    p = _nn(u, w_ref[:, _O_QB:_O_QB + B_W])
    for k in range(B_W // LANES):
        r = _rope_lanes(p[:, k * LANES:(k + 1) * LANES], cos, sin, lo32) * QK_SCALE
        qb32_ref[:, k * LANES:(k + 1) * LANES] = r
        qb16_ref[:, k * LANES:(k + 1) * LANES] = r.astype(BF16)
    p = _nn(u, w_ref[:, _O_QC:_O_QC + C_W])
    for k in range(C_W // LANES):
        r = _rope_lanes(p[:, k * LANES:(k + 1) * LANES], cos, sin, lo32) * QK_SCALE
        even = jnp.where(lo64, r, 0.0)
        odd = jnp.where(lo64, pltpu.roll(r, HEAD_DIM, 1), 0.0)
        qcz16_ref[:, (2 * k) * LANES:(2 * k + 1) * LANES] = even.astype(BF16)
        qcz16_ref[:, (2 * k + 1) * LANES:(2 * k + 2) * LANES] = odd.astype(BF16)


def _proj_sample_body(h_ref, g_ref, w_ref, wgate_ref, cos_ref, sin_ref,
                      ka_ref, va_ref, bkv_ref, ckv_ref, win_ref,
                      qa16_ref, qb16_ref, qb32_ref, qcz16_ref, gate_ref):
    tm = h_ref.shape[0]
    u = (_rms(h_ref[...]) * g_ref[...]).astype(BF16)
    cos = cos_ref[...]
    sin = sin_ref[...]
    lo32, lo64 = _rope_rows((tm, LANES))
    _proj_queries(u, w_ref, cos, sin, lo32, lo64, qa16_ref, qb16_ref, qb32_ref, qcz16_ref)
    p = _nn(u, w_ref[:, _O_KA:_O_KA + A_W])
    for k in range(A_W // LANES):
        ka_ref[:, k * LANES:(k + 1) * LANES] = _rope_lanes(p[:, k * LANES:(k + 1) * LANES], cos, sin, lo32)
    va_ref[...] = _nn(u, w_ref[:, _O_VA:_O_VA + A_W])
    p = _nn(u, w_ref[:, _O_KB:_O_KB + B_W])
    for k in range(B_W // LANES):
        bkv_ref[:, k * LANES:(k + 1) * LANES] = _rope_lanes(p[:, k * LANES:(k + 1) * LANES], cos, sin, lo32)
    bkv_ref[:, B_W:2 * B_W] = _nn(u, w_ref[:, _O_VB:_O_VB + B_W])
    p = _nn(u, w_ref[:, _O_KVC:_O_KVC + KVC_W])
    for k in range(KVC_W // LANES):
        x = p[:, k * LANES:(k + 1) * LANES]
        r = jnp.where(lo64, _rope_lanes(x, cos, sin, lo32), x)
        if k < 2:
            ckv_ref[:, k * LANES:(k + 1) * LANES] = r
        else:
            win_ref[...] = r
    gate_ref[...] = jax.nn.sigmoid(_nn(u, wgate_ref[...]))


def _proj_sample(h, g_mix, w_main, w_gate, cos, sin):
    m = h.shape[0]
    tm = TM_PROJ
    row = lambda w: pl.BlockSpec((tm, w), lambda i: (i, 0))
    full = lambda shape: pl.BlockSpec(shape, lambda i: (0, 0))
    outs = ((A_W, F32), (A_W, F32), (2 * B_W, F32), (4 * HEAD_DIM, F32), (2 * HEAD_DIM, F32),
            (A_W, BF16), (B_W, BF16), (B_W, F32), (2 * C_W, BF16), (LANES, F32))
    return pl.pallas_call(
        _proj_sample_body,
        out_shape=[jax.ShapeDtypeStruct((m, w), dt) for w, dt in outs],
        grid=(m // tm,),
        in_specs=[row(D_MODEL), full((1, D_MODEL)), full((D_MODEL, MAIN_W)), full((D_MODEL, LANES)),
                  full((tm, LANES)), full((tm, LANES))],
        out_specs=[row(w) for w, _ in outs],
        compiler_params=_params(1),
        name="in_proj_rope_sample",
    )(h, g_mix.reshape(1, D_MODEL), w_main, w_gate, cos, sin)


_KT_ROWS = A_W + 2 * B_W + KVC_W


def _proj_prompt_body(h_ref, g_ref, w_ref, wt_ref, wgate_ref, cos_ref, sin_ref, cos_t_ref, sin_t_ref,
                      _kat_in, _va_in, _bkvt_in, _ckvt_in, _wint_in,
                      kat_ref, va_ref, bkvt_ref, ckvt_ref, wint_ref,
                      qa16_ref, kat16_ref, va16_ref, qb16_ref, qb32_ref, bkvt16_ref,
                      qcz16_ref, selt16_ref, wint16_ref, gate_ref):
    tm = h_ref.shape[0]
    half = HEAD_DIM // 2
    u = (_rms(h_ref[...]) * g_ref[...]).astype(BF16)
    lo32, lo64 = _rope_rows((tm, LANES))
    _proj_queries(u, w_ref, cos_ref[...], sin_ref[...], lo32, lo64, qa16_ref, qb16_ref, qb32_ref, qcz16_ref)

    p = _nn(u, w_ref[:, _O_VA:_O_VA + A_W])
    va16_ref[...] = p.astype(BF16)
    for h in range(HEADS_A):
        va_ref[pl.ds(h, tm, stride=HEADS_A), :] = p[:, h * LANES:(h + 1) * LANES]

    cos_t = cos_t_ref[...]
    sin_t = sin_t_ref[...]

    def rope_t(x):
        x1, x2 = x[0:half], x[half:HEAD_DIM]
        return jnp.concatenate([x1 * cos_t - x2 * sin_t, x2 * cos_t + x1 * sin_t], axis=0)

    def store_t(f32_ref, b16_ref, row0, val):
        rows = val.shape[0]
        if f32_ref is not None:
            f32_ref[row0:row0 + rows, :] = val
        if b16_ref is not None:
            for t in range(tm // TQ):
                b16_ref[t, row0:row0 + rows, :] = val[:, t * TQ:(t + 1) * TQ].astype(BF16)

    pt = _nt(wt_ref[0:A_W, :], u)
    for g in range(A_W // HEAD_DIM):
        store_t(kat_ref, kat16_ref, g * HEAD_DIM, rope_t(pt[g * HEAD_DIM:(g + 1) * HEAD_DIM]))
    pt = _nt(wt_ref[A_W:A_W + 2 * B_W, :], u)
    for g in range(B_W // HEAD_DIM):
        store_t(bkvt_ref, bkvt16_ref, g * HEAD_DIM, rope_t(pt[g * HEAD_DIM:(g + 1) * HEAD_DIM]))
    store_t(bkvt_ref, bkvt16_ref, B_W, pt[B_W:2 * B_W])
    pt = _nt(wt_ref[A_W + 2 * B_W:_KT_ROWS, :], u)
    for g in range(KVC_W // HEAD_DIM):
        x = pt[g * HEAD_DIM:(g + 1) * HEAD_DIM]
        if g % 2 == 0:
            x = rope_t(x)
        if g < 2:
            store_t(ckvt_ref, None, g * HEAD_DIM, x)
        elif g < 4:
            store_t(ckvt_ref, None, g * HEAD_DIM, x)
            store_t(None, selt16_ref, (g - 2) * HEAD_DIM, x)
        else:
            store_t(wint_ref, wint16_ref, (g - 4) * HEAD_DIM, x)
    gate_ref[...] = jax.nn.sigmoid(_nn(u, wgate_ref[...]))


def _proj_prompt(h, g_mix, w_main, w_t, w_gate, tabs, n, s, l, stacked):
    cos, sin, cos_t, sin_t = tabs
    m = n * s
    tm = TM_PROJ
    per_seq = s // tm
    nt = tm // TQ
    row = lambda w: pl.BlockSpec((tm, w), lambda i: (i, 0))
    full = lambda shape: pl.BlockSpec(shape, lambda i: (0,) * len(shape))
    tab = pl.BlockSpec((tm, LANES), lambda i: (i % per_seq, 0))
    tab_t = pl.BlockSpec((HEAD_DIM // 2, tm), lambda i: (0, i % per_seq))
    feat = lambda w: pl.BlockSpec((None, None, w, tm), lambda i: (l, i // per_seq, 0, i % per_seq))
    tiles = lambda w: pl.BlockSpec((None, nt, w, TQ), lambda i: (i // per_seq, i % per_seq, 0, 0))
    sds = jax.ShapeDtypeStruct
    out_shape = [sds(a.shape, a.dtype) for a in stacked]
    out_shape += [sds((m, A_W), BF16), sds((n, s // TQ, A_W, TQ), BF16), sds((m, A_W), BF16),
                  sds((m, B_W), BF16), sds((m, B_W), F32), sds((n, s // TQ, 2 * B_W, TQ), BF16),
                  sds((m, 2 * C_W), BF16), sds((n, s // TQ, 2 * HEAD_DIM, TQ), BF16),
                  sds((n, s // TQ, 2 * HEAD_DIM, TQ), BF16), sds((m, LANES), F32)]
    out_specs = [feat(A_W), pl.BlockSpec((None, tm * HEADS_A, LANES), lambda i: (l, i, 0)), feat(2 * B_W),
                 feat(4 * HEAD_DIM), feat(2 * HEAD_DIM),
                 row(A_W), tiles(A_W), row(A_W), row(B_W), row(B_W), tiles(2 * B_W),
                 row(2 * C_W), tiles(2 * HEAD_DIM), tiles(2 * HEAD_DIM), row(LANES)]
    in_specs = [row(D_MODEL), full((1, D_MODEL)), full((D_MODEL, MAIN_W)), full((_KT_ROWS, D_MODEL)),
                full((D_MODEL, LANES)), tab, tab, tab_t, tab_t]
    n_in = len(in_specs)
    in_specs += [pl.BlockSpec(memory_space=pl.ANY)] * len(stacked)
    return pl.pallas_call(
        _proj_prompt_body,
        out_shape=out_shape,
        grid=(m // tm,),
        in_specs=in_specs,
        out_specs=out_specs,
        input_output_aliases={n_in + k: k for k in range(len(stacked))},
        compiler_params=_params(1),
        name="in_proj_rope_prompt",
    )(h, g_mix.reshape(1, D_MODEL), w_main, w_t, w_gate, cos, sin, cos_t, sin_t, *stacked)


def _rank_lower(x, n, width_iota):
    rank = jnp.zeros(x.shape, F32)
    for bp in range(n):
        col = x[:, bp:bp + 1]
        tie = jnp.where(bp < width_iota, 1.0, 0.0)
        rank = rank + jnp.where(col > x, 1.0, jnp.where(col == x, tie, 0.0))
    return rank


def _rank_lower_t(x, n, row_iota):
    rank = jnp.zeros(x.shape, F32)
    for bp in range(n):
        row = x[bp:bp + 1, :]
        tie = jnp.where(bp < row_iota, 1.0, 0.0)
        rank = rank + jnp.where(row > x, 1.0, jnp.where(row == x, tie, 0.0))
    return rank


def _head_rms_scale(o, gh, lam_init):
    return _rms(o) * gh * (1.0 - lam_init)


def _attn_a_prompt_body(q_ref, kt_ref, v_ref, lam_ref, gh_ref, o_ref, *, lam_init):
    i = pl.program_id(1)
    tq = q_ref.shape[0]
    lam = _lam_value(lam_ref, lam_init)
    gh = gh_ref[...]
    lane = _iota((tq, LANES), 1)
    causal = _iota((tq, tq), 1) <= _iota((tq, tq), 0)
    head_sl = [slice(h * LANES, (h + 1) * LANES) for h in range(HEADS_A)]
    chains = [(h, c) for h in range(HEADS_A) for c in range(2)]
    qms = []
    for h, c in chains:
        q2 = q_ref[:, head_sl[h]]
        in_c = (lane >= c * HEAD_DIM) & (lane < (c + 1) * HEAD_DIM)
        qms.append(jnp.where(in_c, q2, jnp.zeros_like(q2)))

    def rows(j):
        return pl.ds(pl.multiple_of(j * tq, tq), tq)

    ones = jnp.ones((tq, LANES), BF16)

    def v_ext(j, h):
        return jnp.concatenate([v_ref[rows(j), head_sl[h]], ones], axis=1)

    for h0 in range(0, HEADS_A, A_HEADS_PER_LOOP):
        group = [(2 * h + c, h) for h in range(h0, h0 + A_HEADS_PER_LOOP) for c in range(2)]
        state = []
        for idx, h in group:
            s = jnp.where(causal, _nn(qms[idx], kt_ref[i, head_sl[h], :]), NEG_INF)
            m = jnp.max(s, axis=-1, keepdims=True)
            p = jnp.exp((s - m).astype(BF16))
            state += [m, _nn(p, v_ext(i, h))]

        def body(j, carry, group=group):
            out = []
            for k, (idx, h) in enumerate(group):
                m, acc = carry[2 * k:2 * k + 2]
                s = _nn(qms[idx], kt_ref[j, head_sl[h], :])
                m_new = jnp.maximum(m, jnp.max(s, axis=-1, keepdims=True))
                p = jnp.exp((s - m_new).astype(BF16))
                out += [m_new, jnp.exp(m - m_new) * acc + _nn(p, v_ext(j, h))]
            return tuple(out)

        state = lax.fori_loop(0, i, body, tuple(state))
        for k in range(A_HEADS_PER_LOOP):
            a0, a1 = state[4 * k + 1], state[4 * k + 3]
            o0 = a0[:, :LANES] / a0[:, LANES:]
            o1 = a1[:, :LANES] / a1[:, LANES:]
            o_ref[:, head_sl[h0 + k]] = _head_rms_scale(o0 - lam * o1, gh, lam_init).astype(BF16)


def _attn_a_prompt(q16, kt16, v16, lam_p, g_head, n, s, lidx):
    nq = s // TQ
    qspec = pl.BlockSpec((TQ, A_W), lambda b, i: (b * nq + i, 0))
    return pl.pallas_call(
        functools.partial(_attn_a_prompt_body, lam_init=_lam_init(lidx)),
        out_shape=jax.ShapeDtypeStruct((n * s, A_W), BF16),
        grid=(n, nq),
        in_specs=[qspec,
                  pl.BlockSpec((None, nq, A_W, TQ), lambda b, i: (b, 0, 0, 0)),
                  pl.BlockSpec((s, A_W), lambda b, i: (b, 0)),
                  pl.BlockSpec((4, HEAD_DIM), lambda b, i: (0, 0)),
                  pl.BlockSpec((1, 2 * HEAD_DIM), lambda b, i: (0, 0))],
        out_specs=qspec,
        compiler_params=_params(2),
        name="diff_attn_prompt",
    )(q16, kt16, v16, lam_p, g_head.reshape(1, 2 * HEAD_DIM))


def _block_means_t(blocks):
    feats = blocks[0].shape[0]
    lane = _iota((feats, LANES), 1)
    out = jnp.zeros((feats, LANES), F32)
    for b, blk in enumerate(blocks):
        out = jnp.where(lane == b, jnp.sum(blk, axis=1, keepdims=True) * (1.0 / MOBA_BLOCK), out)
    return out


def _attn_b_prompt_body(q_ref, q32_ref, k32t_ref, kvt_ref, o_ref, kmean_ref):
    i = pl.program_id(1)
    tq = q_ref.shape[0]
    nb = k32t_ref.shape[1] // MOBA_BLOCK

    @pl.when(i == 0)
    def _():
        kmean_ref[...] = _block_means_t([k32t_ref[:, b * MOBA_BLOCK:(b + 1) * MOBA_BLOCK] for b in range(nb)])

    lane = _iota((tq, LANES), 1)
    causal = _iota((tq, tq), 1) <= _iota((tq, tq), 0)
    nb_rows = -(-nb // 8) * 8
    blk_t = _iota((nb_rows, tq), 0)
    past_t = blk_t < i
    ksl = [slice((h // 2) * LANES, (h // 2 + 1) * LANES) for h in range(HEADS_B)]
    vsl = [slice(B_W + (h // 2) * LANES, B_W + (h // 2 + 1) * LANES) for h in range(HEADS_B)]
    ones_t = jnp.ones((HEAD_DIM, tq), BF16)

    def v_ones(j, h):
        vt = kvt_ref[j, vsl[h], :]
        if h % 2 == 0:
            return jnp.concatenate([vt[0:HEAD_DIM], ones_t], axis=0)
        return jnp.concatenate([ones_t, vt[HEAD_DIM:2 * HEAD_DIM]], axis=0)

    qms, chosen, state = [], [], []
    for h in range(HEADS_B):
        q2 = q_ref[:, ksl[h]]
        in_h = (lane >= (h % 2) * HEAD_DIM) & (lane < (h % 2 + 1) * HEAD_DIM)
        qms.append(jnp.where(in_h, q2, jnp.zeros_like(q2)))
        gate = _nn_precise(jnp.where(in_h, q32_ref[:, ksl[h]], 0.0), kmean_ref[ksl[h], :])
        gate_t = jnp.where(past_t, gate.T[0:nb_rows], NEG_INF)
        pick_t = jnp.where((_rank_lower_t(gate_t, nb, blk_t) < MOBA_TOPK) & past_t, 1.0, 0.0)
        chosen.append(jnp.concatenate([pick_t, jnp.zeros((LANES - nb_rows, tq), F32)], axis=0).T)
        s = jnp.where(causal, _nn(qms[h], kvt_ref[i, ksl[h], :]), NEG_INF)
        m = jnp.max(s, axis=-1, keepdims=True)
        state += [m, _nt(jnp.exp((s - m).astype(BF16)), v_ones(i, h))]

    def body(j, carry):
        out = []
        for h in range(HEADS_B):
            m, acc = carry[2 * h:2 * h + 2]
            use = jnp.max(jnp.where(lane == j, chosen[h], 0.0), axis=1, keepdims=True)
            s = jnp.where(use > 0.5, _nn(qms[h], kvt_ref[j, ksl[h], :]), NEG_INF)
            m_new = jnp.maximum(m, jnp.max(s, axis=-1, keepdims=True))
            p = jnp.exp((s - m_new).astype(BF16))
            out += [m_new, jnp.exp(m - m_new) * acc + _nt(p, v_ones(j, h))]
        return tuple(out)

    state = lax.fori_loop(0, i, body, tuple(state))
    for pair in range(HEADS_B // 2):
        a0, a1 = state[4 * pair + 1], state[4 * pair + 3]
        o0 = a0 / pltpu.roll(a0, HEAD_DIM, 1)
        o1 = a1 / pltpu.roll(a1, HEAD_DIM, 1)
        o_ref[:, ksl[2 * pair]] = jnp.where(lane < HEAD_DIM, o0, o1).astype(BF16)


def _attn_b_prompt(q16, q32, bkvt32, bkvt16, n, s, l):
    nq = s // TQ
    assert TQ == MOBA_BLOCK and s // MOBA_BLOCK <= LANES
    qspec = pl.BlockSpec((TQ, B_W), lambda b, i: (b * nq + i, 0))
    return pl.pallas_call(
        _attn_b_prompt_body,
        out_shape=jax.ShapeDtypeStruct((n * s, B_W), BF16),
        grid=(n, nq),
        in_specs=[qspec, qspec,
                  pl.BlockSpec((None, None, B_W, s), lambda b, i: (l, b, 0, 0)),
                  pl.BlockSpec((None, nq, 2 * B_W, TQ), lambda b, i: (b, 0, 0, 0))],
        out_specs=qspec,
        scratch_shapes=[pltpu.VMEM((B_W, LANES), F32)],
        compiler_params=_params(2),
        name="moba_attn_prompt",
    )(q16, q32, bkvt32, bkvt16)


_CMP_ROWS = 128


def _compress_core(xs_ref, pe_ref, w1_ref, w2_ref):
    half = CMP_LEN // 2
    n_chunk = xs_ref.shape[0] // CMP_STRIDE
    acc = [None, None]
    def rows(r):
        return (xs_ref[pl.ds(r % half, n_chunk, stride=CMP_STRIDE), :] + pe_ref[r:r + 1, :]).astype(BF16)

    for r in range(0, CMP_LEN, 2):
        t = _nn(jnp.concatenate([rows(r), rows(r + 1)], axis=1), w1_ref[r // 2])
        acc[r // half] = t if acc[r // half] is None else acc[r // half] + t
    pre = acc[0] + pltpu.roll(acc[1], n_chunk - 1, 0)
    hid = jax.nn.gelu(pre).astype(BF16)
    return _nn(hid, w2_ref[...]).astype(BF16)


def _compress_prompt_body(xt_ref, pe_ref, w1_ref, w2_ref, o_ref, xs_ref):
    for j in range(xt_ref.shape[1] // LANES):
        xs_ref[j * LANES:(j + 1) * LANES, :] = xt_ref[:, j * LANES:(j + 1) * LANES].T
    o_ref[...] = _compress_core(xs_ref, pe_ref, w1_ref, w2_ref)


def _compress_weights(w_cmp1, w_cmp2, cmp_pos):
    w1 = w_cmp1.reshape(2, CMP_LEN, HEAD_DIM, CMP_HIDDEN)
    z1 = jnp.zeros((CMP_LEN, HEAD_DIM, CMP_HIDDEN), F32)
    top = jnp.concatenate([w1[0], z1], axis=2)
    bot = jnp.concatenate([z1, w1[1]], axis=2)
    w1c = jnp.concatenate([top, bot], axis=1).astype(BF16)
    w1c = w1c.reshape(CMP_LEN // 2, 2 * LANES, 2 * CMP_HIDDEN)
    z2 = jnp.zeros((CMP_HIDDEN, HEAD_DIM), F32)
    w2c = jnp.concatenate([jnp.concatenate([w_cmp2[0], z2], axis=1),
                           jnp.concatenate([z2, w_cmp2[1]], axis=1)], axis=0).astype(BF16)
    pe = jnp.concatenate([cmp_pos[0], cmp_pos[1]], axis=1)
    return pe, w1c, w2c


def _const_specs(index):
    return [pl.BlockSpec((CMP_LEN, LANES), index(2)),
            pl.BlockSpec((CMP_LEN // 2, 2 * LANES, 2 * CMP_HIDDEN), index(3)),
            pl.BlockSpec((2 * CMP_HIDDEN, LANES), index(2))]


def _compress_prompt(ckvt32, cw, n, s, l):
    assert s // CMP_STRIDE == _CMP_ROWS
    zero = lambda nd: (lambda b: (0,) * nd)
    return pl.pallas_call(
        _compress_prompt_body,
        out_shape=jax.ShapeDtypeStruct((n * _CMP_ROWS, LANES), BF16),
        grid=(n,),
        in_specs=[pl.BlockSpec((None, None, LANES, s), lambda b: (l, b, 0, 0))] + _const_specs(zero),
        out_specs=pl.BlockSpec((_CMP_ROWS, LANES), lambda b: (b, 0)),
        scratch_shapes=[pltpu.VMEM((s, LANES), F32)],
        compiler_params=_params(1),
        name="nsa_compress_prompt",
    )(ckvt32, *cw)


def _page_index(l, g, group, j, row_block, b, pt):
    return (l, pt[b * group + g, j], row_block, 0)


def _page_specs(l, n_pages, rows, row_block=0, group=1):
    return [pl.BlockSpec((None, None, rows, PAGE_SIZE), functools.partial(_page_index, l, g, group, j, row_block))
            for g in range(group) for j in range(n_pages)]


def _overlap_matrix(t_len):
    n_cmp = (t_len - CMP_LEN) // CMP_STRIDE + 1
    nsb = -(-t_len // SEL_BLOCK)
    starts = np.arange(n_cmp) * CMP_STRIDE
    sb = np.arange(nsb) * SEL_BLOCK
    ov = np.clip(np.minimum(starts[:, None] + CMP_LEN, sb[None, :] + SEL_BLOCK)
                 - np.maximum(starts[:, None], sb[None, :]), 0, None) / CMP_STRIDE
    out = np.zeros((_CMP_ROWS, LANES), np.float32)
    out[:n_cmp, :nsb] = ov
    return jnp.asarray(out, BF16), nsb


def _expand_matrix(s):
    nt = s // TQ
    e = np.zeros((nt, LANES, TQ), np.float32)
    for j in range(nt):
        for k in range(TQ):
            e[j, (j * TQ + k) // SEL_BLOCK, k] = 1.0
    return jnp.asarray(e, BF16)


def _nsa_flags(p_sum, ovl, own, lane, nsb):
    imp = None
    for part in _split3(p_sum):
        t = _nn(part, ovl)
        imp = t if imp is None else imp + t
    forced = (lane == 0) | (lane == own) | (lane == own - 1)
    imp = jnp.where(lane > own, NEG_INF, jnp.where(forced, jnp.inf, imp))
    rank = _rank_lower(imp, nsb, lane)
    return jnp.where((rank < SEL_TOPK) & (lane <= own), 1.0, 0.0)


def _nsa_flags_t(p_sum, ovl, pos0, nsb):
    tq = p_sum.shape[0]
    imp = None
    for part in _split3(p_sum):
        t = _nn(part, ovl)
        imp = t if imp is None else imp + t
    rows = -(-nsb // 8) * 8
    imp_t = imp.T[0:rows]
    blk = _iota((rows, tq), 0)
    own = (pos0 + _iota((rows, tq), 1)) // SEL_BLOCK
    forced = (blk == 0) | (blk == own) | (blk == own - 1)
    imp_t = jnp.where(blk > own, NEG_INF, jnp.where(forced, jnp.inf, imp_t))
    rank = _rank_lower_t(imp_t, nsb, blk)
    flag_t = jnp.where((rank < SEL_TOPK) & (blk <= own), 1.0, 0.0)
    flag_t = jnp.concatenate([flag_t, jnp.zeros((LANES - rows, tq), F32)], axis=0)
    return flag_t.T


def _masked_probs(s, mask):
    s = jnp.where(mask, s, NEG_INF)
    m = jnp.max(s, axis=-1, keepdims=True)
    m = jnp.where(m > NEG_INF, m, 0.0)
    e = jnp.exp(s - m)
    d = jnp.sum(e, axis=-1, keepdims=True)
    return e / jnp.where(d > 0, d, 1.0)


def _attn_c_prompt_body(qz_ref, gate_ref, selt_ref, wint_ref, kvc_ref, ovl_ref, exp_ref, o_ref, *, nsb):
    i = pl.program_id(1)
    tq = gate_ref.shape[0]
    nh = HEADS_C
    q4 = jnp.concatenate([qz_ref[:, h * LANES:(h + 1) * LANES] for h in range(nh)], axis=0)
    lane = _iota((tq, LANES), 1)
    pos = i * tq + _iota((tq, LANES), 0)
    rr = _iota((tq, tq), 0)
    cc = _iota((tq, tq), 1)
    causal = cc <= rr

    kvc = kvc_ref[...]
    cmp_ok = (CMP_STRIDE * lane + (CMP_LEN - 1)) <= pos
    s = _nt(q4, kvc).reshape(nh, tq, LANES)
    p = _masked_probs(s, cmp_ok[None])
    o_cmp = _nn(p.reshape(nh * tq, LANES).astype(BF16), kvc)
    p_sum = p[0] + p[1] + p[2] + p[3]

    flag = _nsa_flags_t(p_sum, ovl_ref[...], i * tq, nsb).astype(BF16)

    ones_t = jnp.ones((HEAD_DIM, tq), BF16)

    def pv_tile(kj):
        return jnp.concatenate([ones_t, kj[HEAD_DIM:2 * HEAD_DIM]], axis=0)

    kd = selt_ref[i]
    ok = (_nn(flag, exp_ref[i]) > 0.5) & causal
    s = jnp.where(ok[None], _nn(q4, kd).reshape(nh, tq, tq), NEG_INF)
    m = jnp.max(s, axis=-1, keepdims=True)
    p = jnp.exp((s - m).astype(BF16))
    acc = _nt(p.reshape(nh * tq, tq), pv_tile(kd)).reshape(nh, tq, LANES)

    def body(j, carry):
        m, acc = carry
        kj = selt_ref[j]
        ok = _nn(flag, exp_ref[j]) > 0.5
        s = jnp.where(ok[None], _nn(q4, kj).reshape(nh, tq, tq), NEG_INF)
        m_new = jnp.maximum(m, jnp.max(s, axis=-1, keepdims=True))
        p = jnp.exp((s - m_new).astype(BF16))
        pv = _nt(p.reshape(nh * tq, tq), pv_tile(kj)).reshape(nh, tq, LANES)
        return m_new, jnp.exp(m - m_new) * acc + pv

    m, acc = lax.fori_loop(0, i, body, (m, acc))
    o_sel = acc / pltpu.roll(acc, HEAD_DIM, 2)

    assert WINDOW == 2 * tq
    w2 = wint_ref[jnp.maximum(i - 2, 0)]
    w1 = wint_ref[jnp.maximum(i - 1, 0)]
    w0 = wint_ref[i]
    s2 = jnp.where(((cc >= rr) & (i >= 2))[None], _nn(q4, w2).reshape(nh, tq, tq), NEG_INF)
    s1 = jnp.where(i >= 1, _nn(q4, w1).reshape(nh, tq, tq), NEG_INF)
    s0 = jnp.where(causal[None], _nn(q4, w0).reshape(nh, tq, tq), NEG_INF)
    s_all = jnp.concatenate([s2, s1, s0], axis=-1)
    m = jnp.max(s_all, -1, keepdims=True)
    e = jnp.exp((s_all - m).astype(BF16)).reshape(nh * tq, 3 * tq)
    acc = (_nt(e[:, 0:tq], pv_tile(w2)) + _nt(e[:, tq:2 * tq], pv_tile(w1))
           + _nt(e[:, 2 * tq:3 * tq], pv_tile(w0)))
    o_win = (acc / pltpu.roll(acc, HEAD_DIM, 1)).reshape(nh, tq, LANES)

    o_cmp = o_cmp.reshape(nh, tq, LANES)
    g = gate_ref[...]
    heads = []
    for h in range(nh):
        heads.append(g[:, 3 * h:3 * h + 1] * o_cmp[h] + g[:, 3 * h + 1:3 * h + 2] * o_sel[h]
                     + g[:, 3 * h + 2:3 * h + 3] * o_win[h])
    for pair in range(nh // 2):
        both = jnp.where(lane < HEAD_DIM, pltpu.roll(heads[2 * pair], HEAD_DIM, 1), heads[2 * pair + 1])
        o_ref[:, pair * LANES:(pair + 1) * LANES] = both.astype(BF16)


def _attn_c_prompt(qz16, gates, selt16, wint16, kvcmp16, n, s):
    nq = s // TQ
    ovl, nsb = _overlap_matrix(s)
    expand = _expand_matrix(s)
    qrow = lambda w: pl.BlockSpec((TQ, w), lambda b, i: (b * nq + i, 0))
    tiles = pl.BlockSpec((None, nq, 2 * HEAD_DIM, TQ), lambda b, i: (b, 0, 0, 0))
    return pl.pallas_call(
        functools.partial(_attn_c_prompt_body, nsb=nsb),
        out_shape=jax.ShapeDtypeStruct((n * s, C_W), BF16),
        grid=(n, nq),
        in_specs=[qrow(2 * C_W), qrow(LANES), tiles, tiles,
                  pl.BlockSpec((_CMP_ROWS, LANES), lambda b, i: (b, 0)),
                  pl.BlockSpec((_CMP_ROWS, LANES), lambda b, i: (0, 0)),
                  pl.BlockSpec((nq, LANES, TQ), lambda b, i: (0, 0, 0))],
        out_specs=qrow(C_W),
        compiler_params=_params(2),
        name="nsa_attn_prompt",
    )(qz16, gates, selt16, wint16, kvcmp16, ovl, expand)


def _pad_page(x):
    rows, w = x.shape
    return jnp.concatenate([x, jnp.zeros((PAGE_SIZE - rows, w), x.dtype)], axis=0)


def _new_page_mask(n_rows, nq):
    r = _iota((n_rows, PAGE_SIZE), 0) % nq
    t = _iota((n_rows, PAGE_SIZE), 1)
    return t <= r


def _dec_a_body(pt_ref, q_ref, kn_ref, vn_ref, lam_ref, gh_ref, *refs, lam_init):
    n_pages = (len(refs) - 1) // 2
    kp, vp, o_ref = refs[:n_pages], refs[n_pages:2 * n_pages], refs[-1]
    nq = q_ref.shape[0]
    rows = 2 * HEADS_A * nq
    per_head = 2 * nq
    lam = _lam_value(lam_ref, lam_init)
    qt = jnp.concatenate([q_ref[...]] * (2 * HEADS_A), axis=0)
    diag = (_iota((rows, A_W), 0) // nq) == (_iota((rows, A_W), 1) // HEAD_DIM)
    qbd = jnp.where(diag, qt, jnp.zeros_like(qt))
    kn = _pad_page(kn_ref[...]).astype(BF16)
    vn = _pad_page(vn_ref[...]).astype(BF16)
    s_new = jnp.where(_new_page_mask(rows, nq), _nt(qbd, kn), NEG_INF)
    scores = [_nn(qbd, kp[j][...].astype(BF16)) for j in range(n_pages)]
    m = jnp.max(s_new, axis=1, keepdims=True)
    for s in scores:
        m = jnp.maximum(m, jnp.max(s, axis=1, keepdims=True))
    e = jnp.exp(s_new - m)
    d = jnp.sum(e, axis=1, keepdims=True)
    e = e.astype(BF16)
    accs = [_nn(e[h * per_head:(h + 1) * per_head], vn[:, h * LANES:(h + 1) * LANES]) for h in range(HEADS_A)]
    for j in range(n_pages):
        e = jnp.exp(scores[j] - m)
        d = d + jnp.sum(e, axis=1, keepdims=True)
        e = e.astype(BF16)
        for h in range(HEADS_A):
            vh = vp[j][pl.ds(h, PAGE_SIZE, stride=HEADS_A), :].astype(BF16)
            accs[h] = accs[h] + _nn(e[h * per_head:(h + 1) * per_head], vh)
    gh = gh_ref[...]
    for h in range(HEADS_A):
        on = accs[h] / d[h * per_head:(h + 1) * per_head]
        o = on[0:nq] - lam * on[nq:2 * nq]
        o_ref[:, h * LANES:(h + 1) * LANES] = _head_rms_scale(o, gh, lam_init).astype(BF16)


def _dec_a(q16, kn32, vn32, lam_p, g_head, cache_kt, cache_v4, page_table, l, nq):
    n, n_pages = page_table.shape
    row = lambda w: pl.BlockSpec((nq, w), lambda b, pt: (b, 0))
    grid_spec = pltpu.PrefetchScalarGridSpec(
        num_scalar_prefetch=1, grid=(n,),
        in_specs=[row(A_W), row(A_W), row(A_W),
                  pl.BlockSpec((4, HEAD_DIM), lambda b, pt: (0, 0)),
                  pl.BlockSpec((1, 2 * HEAD_DIM), lambda b, pt: (0, 0))]
                 + _page_specs(l, n_pages, A_W) + _page_specs(l, n_pages, A_W),
        out_specs=row(A_W))
    return pl.pallas_call(
        functools.partial(_dec_a_body, lam_init=_lam_init(l)),
        out_shape=jax.ShapeDtypeStruct((n * nq, A_W), BF16),
        grid_spec=grid_spec,
        compiler_params=_params(1),
        name="diff_attn_decode",
    )(page_table, q16, kn32, vn32, lam_p, g_head.reshape(1, 2 * HEAD_DIM),
      *([cache_kt] * n_pages), *([cache_v4] * n_pages))


def _dec_b_body(pt_ref, q_ref, q32_ref, kvn_ref, *refs, group):
    pages, o_ref = refs[:-1], refs[-1]
    n_pages = len(pages) // group
    nq = q_ref.shape[0] // group
    for g in range(group):
        r = slice(g * nq, (g + 1) * nq)
        o_ref[r, :] = _dec_b_one(q_ref[r, :], q32_ref[r, :], kvn_ref[r, :],
                                 pages[g * n_pages:(g + 1) * n_pages]).astype(BF16)


def _dec_b_one(q, q32, kvn_rows, pages):
    n_pages = len(pages)
    nq = q.shape[0]
    rows = HEADS_B * nq
    pages_per_blk = MOBA_BLOCK // PAGE_SIZE
    nb = n_pages // pages_per_blk
    diag = (_iota((rows, B_W), 0) // nq) == (_iota((rows, B_W), 1) // HEAD_DIM)
    qt = jnp.concatenate([q] * HEADS_B, axis=0)
    qbd = jnp.where(diag, qt, jnp.zeros_like(qt))
    qbd32 = jnp.where(diag, jnp.concatenate([q32] * HEADS_B, axis=0), 0.0)

    blocks = []
    for b in range(nb):
        blocks.append(jnp.concatenate([pages[j][0:B_W, :] for j in range(b * pages_per_blk, (b + 1) * pages_per_blk)],
                                      axis=1))
    kmean_t = _block_means_t(blocks)
    lane = _iota((rows, LANES), 1)
    gate = _nn_precise(qbd32, kmean_t)
    gate = jnp.where(lane < nb, gate, NEG_INF)
    chosen = jnp.where((_rank_lower(gate, nb, lane) < MOBA_TOPK) & (lane < nb), 1.0, 0.0)

    kvn = _pad_page(kvn_rows).astype(BF16)
    s_new = jnp.where(_new_page_mask(rows, nq), _nt(qbd, kvn[:, 0:B_W]), NEG_INF)
    scores = []
    for j in range(n_pages):
        b = j // pages_per_blk
        s = _nn(qbd, pages[j][0:B_W, :].astype(BF16))
        scores.append(jnp.where(chosen[:, b:b + 1] > 0.5, s, NEG_INF))
    m = jnp.max(s_new, axis=1, keepdims=True)
    for s in scores:
        m = jnp.maximum(m, jnp.max(s, axis=1, keepdims=True))
    e = jnp.exp(s_new - m)
    d = jnp.sum(e, axis=1, keepdims=True)
    acc = _nn(e.astype(BF16), kvn[:, B_W:2 * B_W])
    for j in range(n_pages):
        e = jnp.exp(scores[j] - m)
        d = d + jnp.sum(e, axis=1, keepdims=True)
        acc = acc + _nt(e.astype(BF16), pages[j][B_W:2 * B_W, :].astype(BF16))
    on = acc / d
    head_of_lane = _iota((nq, B_W), 1) // HEAD_DIM
    o = jnp.zeros((nq, B_W), F32)
    for h in range(HEADS_B):
        o = o + jnp.where(head_of_lane == h, on[h * nq:(h + 1) * nq], 0.0)
    return o


def _dec_b(q16, q32, kvn32, cache_kvt, page_table, l, nq):
    n, n_pages = page_table.shape
    group = DEC_GROUP
    row = lambda w: pl.BlockSpec((group * nq, w), lambda b, pt: (b, 0))
    grid_spec = pltpu.PrefetchScalarGridSpec(
        num_scalar_prefetch=1, grid=(n // group,),
        in_specs=[row(B_W), row(B_W), row(2 * B_W)] + _page_specs(l, n_pages, 2 * B_W, group=group),
        out_specs=row(B_W))
    return pl.pallas_call(
        functools.partial(_dec_b_body, group=group),
        out_shape=jax.ShapeDtypeStruct((n * nq, B_W), BF16),
        grid_spec=grid_spec,
        compiler_params=_params(1),
        name="moba_attn_decode",
    )(page_table, q16, q32, kvn32, *([cache_kvt] * (n_pages * group)))


def _dec_c_body(pt_ref, qz_ref, gate_ref, ckvn_ref, winn_ref, pe_ref, w1_ref, w2_ref, ovl_ref, exp_ref, st_ref,
                *refs, nsb, q0, group):
    pages, o_ref, xs_ref = refs[:-2], refs[-2], refs[-1]
    n_pages = len(pages) // group
    nq = gate_ref.shape[0] // group
    for j, pg in enumerate(pages):
        xs_ref[j * PAGE_SIZE:(j + 1) * PAGE_SIZE, :] = pg[0:LANES, :].T
    kvc_all = _compress_core(xs_ref, pe_ref, w1_ref, w2_ref)
    ovl = ovl_ref[...]
    expand = exp_ref[...]
    for g in range(group):
        r = slice(g * nq, (g + 1) * nq)
        tiles = [pg[LANES:2 * LANES, :] for pg in pages[g * n_pages:(g + 1) * n_pages]]
        _dec_c_one(qz_ref[r, :], gate_ref[r, :], ckvn_ref[r, :], winn_ref[r, :],
                   kvc_all[g * _CMP_ROWS:(g + 1) * _CMP_ROWS], ovl, expand, st_ref[g], tiles, o_ref, r, nsb, q0)


def _dec_c_one(qz, g, ckvn_rows, winn_rows, kvc, ovl, expand, st32, tiles, o_ref, out_rows, nsb, q0):
    n_pages = len(tiles)
    nq = g.shape[0]
    nh = HEADS_C
    rows = nh * nq
    q4 = jnp.concatenate([qz[:, h * LANES:(h + 1) * LANES] for h in range(nh)], axis=0)
    lane = _iota((nq, LANES), 1)
    pos = q0 + _iota((nq, LANES), 0)
    lane4 = _iota((rows, LANES), 1)
    qrow4 = _iota((rows, LANES), 0) % nq
    new_ok = _new_page_mask(rows, nq)

    cmp_ok = (CMP_STRIDE * lane4 + (CMP_LEN - 1)) <= (q0 + qrow4)
    p = _masked_probs(_nt(q4, kvc), cmp_ok)
    o_cmp = _nn(p.astype(BF16), kvc)
    p_sum = p[0:nq]
    for h in range(1, nh):
        p_sum = p_sum + p[h * nq:(h + 1) * nq]

    own = pos // SEL_BLOCK
    flag = _nsa_flags(p_sum, ovl, own, lane, nsb)
    flag4 = jnp.concatenate([flag] * nh, axis=0)
    blk_per_page = PAGE_SIZE // SEL_BLOCK
    assert blk_per_page == 2

    ckvn = _pad_page(ckvn_rows).astype(BF16)
    kn = ckvn[:, LANES:2 * LANES]
    own_blk = n_pages * blk_per_page
    s_new = jnp.where(new_ok & (flag4[:, own_blk:own_blk + 1] > 0.5), _nt(q4, kn), NEG_INF)
    kt_all = jnp.concatenate([t.astype(BF16) for t in tiles], axis=1)
    ok = _nn(flag4.astype(BF16), expand) > 0.5
    s_old = jnp.where(ok, _nn(q4, kt_all), NEG_INF)
    m = jnp.maximum(jnp.max(s_new, axis=1, keepdims=True), jnp.max(s_old, axis=1, keepdims=True))
    e_new = jnp.exp(s_new - m)
    e_old = jnp.exp(s_old - m)
    d = jnp.sum(e_new, axis=1, keepdims=True) + jnp.sum(e_old, axis=1, keepdims=True)
    acc = _nn(e_new.astype(BF16), kn) + _nt(e_old.astype(BF16), kt_all)
    o_sel = acc / d

    wn = _pad_page(winn_rows).astype(BF16)
    s_new = jnp.where(new_ok, _nt(q4, wn), NEG_INF)
    st = st32.astype(BF16)
    key = _iota((rows, WINDOW), 1)
    ok = key >= (_iota((rows, WINDOW), 0) % nq)
    s_old = jnp.where(ok, _nn(q4, st), NEG_INF)
    m = jnp.maximum(jnp.max(s_new, axis=1, keepdims=True), jnp.max(s_old, axis=1, keepdims=True))
    e_new = jnp.exp(s_new - m)
    e_old = jnp.exp(s_old - m)
    d = jnp.sum(e_new, axis=1, keepdims=True) + jnp.sum(e_old, axis=1, keepdims=True)
    o_win = (_nn(e_new.astype(BF16), wn) + _nt(e_old.astype(BF16), st)) / d

    heads = []
    for h in range(nh):
        r = slice(h * nq, (h + 1) * nq)
        heads.append(g[:, 3 * h:3 * h + 1] * o_cmp[r] + g[:, 3 * h + 1:3 * h + 2] * o_sel[r]
                     + g[:, 3 * h + 2:3 * h + 3] * o_win[r])
    for pair in range(nh // 2):
        both = jnp.where(lane < HEAD_DIM, pltpu.roll(heads[2 * pair], HEAD_DIM, 1), heads[2 * pair + 1])
        o_ref[out_rows, pair * LANES:(pair + 1) * LANES] = both.astype(BF16)


def _dec_c(qz16, gates, ckvn32, winn32, cw, state_t, cache_ct, page_table, l, nq, q0):
    n, n_pages = page_table.shape
    assert state_t.shape[3] == WINDOW and q0 >= WINDOW and n_pages * PAGE_SIZE // CMP_STRIDE == _CMP_ROWS
    ovl, nsb = _overlap_matrix(q0 + nq)
    keys = np.arange(n_pages * PAGE_SIZE)
    expand = jnp.asarray(np.arange(LANES)[:, None] == keys[None, :] // SEL_BLOCK, BF16)
    group = NSA_DEC_GROUP
    row = lambda w: pl.BlockSpec((group * nq, w), lambda b, pt: (b, 0))
    zero = lambda nd: (lambda b, pt: (0,) * nd)
    grid_spec = pltpu.PrefetchScalarGridSpec(
        num_scalar_prefetch=1, grid=(n // group,),
        in_specs=[row(2 * C_W), row(LANES), row(4 * HEAD_DIM), row(2 * HEAD_DIM)] + _const_specs(zero)
                 + [pl.BlockSpec((_CMP_ROWS, LANES), lambda b, pt: (0, 0)),
                    pl.BlockSpec((LANES, n_pages * PAGE_SIZE), lambda b, pt: (0, 0)),
                    pl.BlockSpec((None, group, 2 * HEAD_DIM, WINDOW), lambda b, pt: (l, b, 0, 0))]
                 + _page_specs(l, n_pages, 4 * HEAD_DIM, group=group),
        out_specs=row(C_W),
        scratch_shapes=[pltpu.VMEM((group * n_pages * PAGE_SIZE, LANES), F32)])
    return pl.pallas_call(
        functools.partial(_dec_c_body, nsb=nsb, q0=q0, group=group),
        out_shape=jax.ShapeDtypeStruct((n * nq, C_W), BF16),
        grid_spec=grid_spec,
        compiler_params=_params(1),
        name="nsa_attn_decode",
    )(page_table, qz16, gates, ckvn32, winn32, *cw, ovl, expand, state_t, *([cache_ct] * (n_pages * group)))


def _rope_tables(n_pos):
    inv = ROPE_THETA ** (-jnp.arange(0, HEAD_DIM, 2, dtype=F32) / HEAD_DIM)
    ang = jnp.arange(n_pos, dtype=F32)[:, None] * inv[None, :]
    cos, sin = jnp.cos(ang), jnp.sin(ang)
    cos128 = jnp.concatenate([cos, cos, cos, cos], axis=-1)
    sin128 = jnp.concatenate([-sin, sin, -sin, sin], axis=-1)
    return cos128, sin128, cos.T, sin.T


def kernel(x_prompt, x_sample, cache_a_k, cache_a_v, cache_b_kv, cache_c_kv, state_c_win, page_table,
           w_in, w_out, g_mix, g_ffn, w_ffn_gate, w_ffn_up, w_ffn_down, diff_lambda, g_diff_head,
           w_cmp1, w_cmp2, cmp_pos, g_final):
    n_p, s_p, _ = x_prompt.shape
    n_s, s_s, _ = x_sample.shape
    n_pages = page_table.shape[1]
    past_len = n_pages * cache_a_k.shape[2]
    n_phys = cache_a_k.shape[1]
    assert cache_a_k.shape[2] == PAGE_SIZE and s_p % TM_PROJ == 0 and TM_PROJ % s_s == 0

    cos, sin, cos_t, sin_t = _rope_tables(past_len + s_s)
    tabs_p = (cos[:s_p], sin[:s_p], cos_t[:, :s_p], sin_t[:, :s_p])
    reps = TM_PROJ // s_s
    cos_s = jnp.tile(cos[past_len:past_len + s_s], (reps, 1))
    sin_s = jnp.tile(sin[past_len:past_len + s_s], (reps, 1))

    ckt = cache_a_k.transpose(0, 1, 3, 4, 5, 2).reshape(DEPTH, n_phys, A_W, PAGE_SIZE)
    cv4 = cache_a_v.reshape(DEPTH, n_phys, PAGE_SIZE * HEADS_A, 2 * HEAD_DIM)
    cbt = cache_b_kv.transpose(0, 1, 3, 4, 5, 2).reshape(DEPTH, n_phys, 2 * B_W, PAGE_SIZE)
    cct = cache_c_kv.transpose(0, 1, 3, 4, 2).reshape(DEPTH, n_phys, 4 * HEAD_DIM, PAGE_SIZE)
    stt = state_c_win.transpose(0, 1, 3, 4, 2).reshape(DEPTH, n_s, 2 * HEAD_DIM, WINDOW)

    hp = x_prompt.reshape(n_p * s_p, D_MODEL)
    hs = x_sample.reshape(n_s * s_s, D_MODEL)
    ent_s = []
    stacked = tuple(jnp.zeros(shape, F32) for shape in (
        (DEPTH, n_p, A_W, s_p), (DEPTH, n_p * s_p * HEADS_A, LANES), (DEPTH, n_p, 2 * B_W, s_p),
        (DEPTH, n_p, 4 * HEAD_DIM, s_p), (DEPTH, n_p, 2 * HEAD_DIM, s_p)))
    wg = w_ffn_gate.astype(BF16)
    wu = w_ffn_up.astype(BF16)
    wd = w_ffn_down.astype(BF16)
    wo = w_out.astype(BF16)
    for l in range(DEPTH):
        w_main = w_in[l][:, :MAIN_W].astype(BF16)
        w_t = jnp.concatenate([w_in[l][:, _O_KA:_O_KA + A_W], w_in[l][:, _O_KB:_O_KB + 2 * B_W],
                               w_in[l][:, _O_KVC:_O_KVC + KVC_W]], axis=1).T.astype(BF16)
        w_gate = jnp.pad(w_in[l][:, MAIN_W:], ((0, 0), (0, LANES - GATE_W))).astype(BF16)
        cw = _compress_weights(w_cmp1[l], w_cmp2[l], cmp_pos[l])
        last = l == DEPTH - 1

        hp = _ffn(hp, g_ffn[l, 0], wg, wu, wd, l, 0)
        outs = _proj_prompt(hp, g_mix[l], w_main, w_t, w_gate, tabs_p, n_p, s_p, l, stacked)
        stacked = tuple(outs[:5])
        qa16, kat16, va16, qb16, qb32, bkvt16, qcz16, selt16, wint16, gates = outs[5:]
        oa = _attn_a_prompt(qa16, kat16, va16, diff_lambda[l], g_diff_head[l], n_p, s_p, l)
        ob = _attn_b_prompt(qb16, qb32, stacked[2], bkvt16, n_p, s_p, l)
        kvcmp = _compress_prompt(stacked[3], cw, n_p, s_p, l)
        oc = _attn_c_prompt(qcz16, gates, selt16, wint16, kvcmp, n_p, s_p)
        hp = _ffn(hp, g_ffn[l, 1], wg, wu, wd, l, 1, (oa, ob, oc, wo), g_final if last else None)

        hs = _ffn(hs, g_ffn[l, 0], wg, wu, wd, l, 0)
        (ka, va, bkv, ckv, win, qa16, qb16, qb32, qcz16, gates) = _proj_sample(
            hs, g_mix[l], w_main, w_gate, cos_s, sin_s)
        oa = _dec_a(qa16, ka, va, diff_lambda[l], g_diff_head[l], ckt, cv4, page_table, l, s_s)
        ob = _dec_b(qb16, qb32, bkv, cbt, page_table, l, s_s)
        oc = _dec_c(qcz16, gates, ckv, win, cw, stt, cct, page_table, l, s_s, past_len)
        hs = _ffn(hs, g_ffn[l, 1], wg, wu, wd, l, 1, (oa, ob, oc, wo), g_final if last else None)
        win_new = win.reshape(n_s, s_s, 2, HEAD_DIM)
        win_all = jnp.concatenate([state_c_win[l], win_new], axis=1)
        ent_s.append((ka.reshape(n_s, s_s, HEADS_A, 2, HEAD_DIM), va.reshape(n_s, s_s, HEADS_A, 2 * HEAD_DIM),
                      bkv.reshape(n_s, s_s, 2, HEADS_B, HEAD_DIM), ckv.reshape(n_s, s_s, 4, HEAD_DIM),
                      win_all[:, win_all.shape[1] - min(WINDOW, win_all.shape[1]):]))

    st_ = lambda i: jnp.stack([e[i] for e in ent_s], axis=0)
    kat, va, bkvt, ckvt, wint = stacked
    win_keep = min(WINDOW, s_p)
    return (hp.reshape(n_p, s_p, D_MODEL), hs.reshape(n_s, s_s, D_MODEL),
            kat.reshape(DEPTH, n_p, HEADS_A, 2, HEAD_DIM, s_p).transpose(0, 1, 5, 2, 3, 4), st_(0),
            va.reshape(DEPTH, n_p, s_p, HEADS_A, 2 * HEAD_DIM), st_(1),
            bkvt.reshape(DEPTH, n_p, 2, HEADS_B, HEAD_DIM, s_p).transpose(0, 1, 5, 2, 3, 4), st_(2),
            ckvt.reshape(DEPTH, n_p, 4, HEAD_DIM, s_p).transpose(0, 1, 4, 2, 3), st_(3),
            wint[:, :, :, s_p - win_keep:].reshape(DEPTH, n_p, 2, HEAD_DIM, win_keep).transpose(0, 1, 4, 2, 3), st_(4))
```

```python
import functools
import math

import numpy as np
import jax
import jax.numpy as jnp
from jax import lax
from jax.experimental import pallas as pl
from jax.experimental.pallas import tpu as pltpu

F32 = jnp.float32
BF16 = jnp.bfloat16

D_MODEL = 1024
DEPTH = 2
HEAD_DIM = 64
HEADS_A = 4
HEADS_B = 4
HEADS_C = 4
D_FF = 2816
ROPE_THETA = 10000.0
MOBA_BLOCK = 256
MOBA_TOPK = 3
CMP_LEN = 32
CMP_STRIDE = 16
CMP_HIDDEN = 4 * HEAD_DIM
SEL_BLOCK = 64
SEL_TOPK = 16
WINDOW = 512
RMS_EPS = 1e-6
PAGE_SIZE = 128

A_W = HEADS_A * 2 * HEAD_DIM
B_W = HEADS_B * HEAD_DIM
C_W = HEADS_C * HEAD_DIM
KVC_W = 6 * HEAD_DIM
GATE_W = 3 * HEADS_C
MAIN_W = 3 * A_W + 3 * B_W + C_W + KVC_W
LANES = 128
QK_SCALE = HEAD_DIM ** -0.5
NEG_INF = float("-inf")
VMEM_LIMIT = 56 * 1024 * 1024

TM_FFN = 512
TF_FFN = 256
TM_PROJ = 512
TQ = 256
NSA_DEC_GROUP = 2
DEC_GROUP = 4
A_HEADS_PER_LOOP = 4

_O_QA, _O_KA, _O_VA = 0, A_W, 2 * A_W
_O_QB = 3 * A_W
_O_KB, _O_VB = _O_QB + B_W, _O_QB + 2 * B_W
_O_QC = _O_QB + 3 * B_W
_O_KVC = _O_QC + C_W


def _nn(a, b):
    return jnp.dot(a, b, preferred_element_type=F32)


def _nt(a, b):
    return lax.dot_general(a, b, (((1,), (1,)), ((), ())), preferred_element_type=F32)


def _split3(x):
    hi = x.astype(BF16)
    r1 = x - hi.astype(F32)
    mid = r1.astype(BF16)
    lo = (r1 - mid.astype(F32)).astype(BF16)
    return hi, mid, lo


def _nn_precise(a, b):
    a_hi, a_mid, _ = _split3(a)
    b_hi, b_mid, _ = _split3(b)
    return _nn(a_hi, b_hi) + (_nn(a_hi, b_mid) + _nn(a_mid, b_hi))


def _rms(x):
    return x * lax.rsqrt(jnp.mean(x * x, axis=-1, keepdims=True) + RMS_EPS)


def _iota(shape, dim):
    return lax.broadcasted_iota(jnp.int32, shape, dim)


def _params(n_axes):
    return pltpu.CompilerParams(dimension_semantics=("arbitrary",) * n_axes,
                                vmem_limit_bytes=VMEM_LIMIT)


def _lam_value(lam_ref, lam_init):
    lp = lam_ref[...]
    a = jnp.sum(lp[0:1] * lp[1:2], axis=1, keepdims=True)
    b = jnp.sum(lp[2:3] * lp[3:4], axis=1, keepdims=True)
    return jnp.exp(a) - jnp.exp(b) + lam_init


def _lam_init(lidx):
    return 0.8 - 0.6 * math.exp(-0.3 * lidx)


def _ffn_body(*refs, mixed, final):
    refs = list(refs)
    x_ref = refs.pop(0)
    x = x_ref[...]
    if mixed:
        oa_ref, ob_ref, oc_ref, wo_ref = refs[:4]
        refs = refs[4:]
        x = x + _nn(oa_ref[...], wo_ref[0:A_W, :])
        x = x + _nn(ob_ref[...], wo_ref[A_W:A_W + B_W, :])
        x = x + _nn(oc_ref[...], wo_ref[A_W + B_W:, :])
    g_ref, wg_ref, wu_ref, wd_ref = refs[:4]
    o_ref = refs[-1]
    xn = (_rms(x) * g_ref[...]).astype(BF16)
    acc = jnp.zeros_like(x)
    for f in range(D_FF // TF_FFN):
        sl = slice(f * TF_FFN, (f + 1) * TF_FFN)
        g = _nn(xn, wg_ref[:, sl])
        u = _nn(xn, wu_ref[:, sl])
        a = (g * jax.nn.sigmoid(g) * u).astype(BF16)
        acc = acc + _nn(a, wd_ref[sl, :])
    y = x + 0.5 * acc
    if final:
        y = _rms(y) * refs[4][...]
    o_ref[...] = y


def _ffn(x, g, wg, wu, wd, l, k, mix=None, g_final=None):
    m = x.shape[0]
    row = lambda w: pl.BlockSpec((TM_FFN, w), lambda i: (i, 0))
    vec = pl.BlockSpec((1, D_MODEL), lambda i: (0, 0))
    full = lambda r, c: pl.BlockSpec((None, None, r, c), lambda i: (l, k, 0, 0))
    in_specs, args = [row(D_MODEL)], [x]
    if mix is not None:
        oa, ob, oc, w_out = mix
        in_specs += [row(A_W), row(B_W), row(C_W), pl.BlockSpec((None, D_MODEL, D_MODEL), lambda i: (l, 0, 0))]
        args += [oa, ob, oc, w_out]
    in_specs += [vec, full(D_MODEL, D_FF), full(D_MODEL, D_FF), full(D_FF, D_MODEL)]
    args += [g.reshape(1, D_MODEL), wg, wu, wd]
    if g_final is not None:
        in_specs.append(vec)
        args.append(g_final.reshape(1, D_MODEL))
    return pl.pallas_call(
        functools.partial(_ffn_body, mixed=mix is not None, final=g_final is not None),
        out_shape=jax.ShapeDtypeStruct((m, D_MODEL), F32),
        grid=(m // TM_FFN,),
        in_specs=in_specs,
        out_specs=row(D_MODEL),
        compiler_params=_params(1),
        name="ffn_half",
    )(*args)


def _rope_rows(lane_shape):
    lane = _iota(lane_shape, 1)
    return (lane % HEAD_DIM) < (HEAD_DIM // 2), lane < HEAD_DIM


def _rope_lanes(x, cos, sin, lo32):
    sh = jnp.where(lo32, pltpu.roll(x, LANES - HEAD_DIM // 2, 1), pltpu.roll(x, HEAD_DIM // 2, 1))
    return x * cos + sh * sin


def _proj_queries(u, w_ref, cos, sin, lo32, lo64, qa16_ref, qb16_ref, qb32_ref, qcz16_ref):
    p = _nn(u, w_ref[:, _O_QA:_O_QA + A_W])
    for k in range(A_W // LANES):
        qa16_ref[:, k * LANES:(k + 1) * LANES] = (
            _rope_lanes(p[:, k * LANES:(k + 1) * LANES], cos, sin, lo32) * QK_SCALE).astype(BF16)
    p = _nn(u, w_ref[:, _O_QB:_O_QB + B_W])
    for k in range(B_W // LANES):
        r = _rope_lanes(p[:, k * LANES:(k + 1) * LANES], cos, sin, lo32) * QK_SCALE
        qb32_ref[:, k * LANES:(k + 1) * LANES] = r
        qb16_ref[:, k * LANES:(k + 1) * LANES] = r.astype(BF16)
    p = _nn(u, w_ref[:, _O_QC:_O_QC + C_W])
    for k in range(C_W // LANES):
        r = _rope_lanes(p[:, k * LANES:(k + 1) * LANES], cos, sin, lo32) * QK_SCALE
        even = jnp.where(lo64, r, 0.0)
        odd = jnp.where(lo64, pltpu.roll(r, HEAD_DIM, 1), 0.0)
        qcz16_ref[:, (2 * k) * LANES:(2 * k + 1) * LANES] = even.astype(BF16)
        qcz16_ref[:, (2 * k + 1) * LANES:(2 * k + 2) * LANES] = odd.astype(BF16)


def _proj_sample_body(h_ref, g_ref, w_ref, wgate_ref, cos_ref, sin_ref,
                      ka_ref, va_ref, bkv_ref, ckv_ref, win_ref,
                      qa16_ref, qb16_ref, qb32_ref, qcz16_ref, gate_ref):
    tm = h_ref.shape[0]
    u = (_rms(h_ref[...]) * g_ref[...]).astype(BF16)
    cos = cos_ref[...]
    sin = sin_ref[...]
    lo32, lo64 = _rope_rows((tm, LANES))
    _proj_queries(u, w_ref, cos, sin, lo32, lo64, qa16_ref, qb16_ref, qb32_ref, qcz16_ref)
    p = _nn(u, w_ref[:, _O_KA:_O_KA + A_W])
    for k in range(A_W // LANES):
        ka_ref[:, k * LANES:(k + 1) * LANES] = _rope_lanes(p[:, k * LANES:(k + 1) * LANES], cos, sin, lo32)
    va_ref[...] = _nn(u, w_ref[:, _O_VA:_O_VA + A_W])
    p = _nn(u, w_ref[:, _O_KB:_O_KB + B_W])
    for k in range(B_W // LANES):
        bkv_ref[:, k * LANES:(k + 1) * LANES] = _rope_lanes(p[:, k * LANES:(k + 1) * LANES], cos, sin, lo32)
    bkv_ref[:, B_W:2 * B_W] = _nn(u, w_ref[:, _O_VB:_O_VB + B_W])
    p = _nn(u, w_ref[:, _O_KVC:_O_KVC + KVC_W])
    for k in range(KVC_W // LANES):
        x = p[:, k * LANES:(k + 1) * LANES]
        r = jnp.where(lo64, _rope_lanes(x, cos, sin, lo32), x)
        if k < 2:
            ckv_ref[:, k * LANES:(k + 1) * LANES] = r
        else:
            win_ref[...] = r
    gate_ref[...] = jax.nn.sigmoid(_nn(u, wgate_ref[...]))


def _proj_sample(h, g_mix, w_main, w_gate, cos, sin):
    m = h.shape[0]
    tm = TM_PROJ
    row = lambda w: pl.BlockSpec((tm, w), lambda i: (i, 0))
    full = lambda shape: pl.BlockSpec(shape, lambda i: (0, 0))
    outs = ((A_W, F32), (A_W, F32), (2 * B_W, F32), (4 * HEAD_DIM, F32), (2 * HEAD_DIM, F32),
            (A_W, BF16), (B_W, BF16), (B_W, F32), (2 * C_W, BF16), (LANES, F32))
    return pl.pallas_call(
        _proj_sample_body,
        out_shape=[jax.ShapeDtypeStruct((m, w), dt) for w, dt in outs],
        grid=(m // tm,),
        in_specs=[row(D_MODEL), full((1, D_MODEL)), full((D_MODEL, MAIN_W)), full((D_MODEL, LANES)),
                  full((tm, LANES)), full((tm, LANES))],
        out_specs=[row(w) for w, _ in outs],
        compiler_params=_params(1),
        name="in_proj_rope_sample",
    )(h, g_mix.reshape(1, D_MODEL), w_main, w_gate, cos, sin)


_KT_ROWS = A_W + 2 * B_W + KVC_W


def _proj_prompt_body(h_ref, g_ref, w_ref, wt_ref, wgate_ref, cos_ref, sin_ref, cos_t_ref, sin_t_ref,
                      _kat_in, _va_in, _bkvt_in, _ckvt_in, _wint_in,
                      kat_ref, va_ref, bkvt_ref, ckvt_ref, wint_ref,
                      qa16_ref, kat16_ref, va16_ref, qb16_ref, qb32_ref, bkvt16_ref,
                      qcz16_ref, selt16_ref, wint16_ref, gate_ref):
    tm = h_ref.shape[0]
    half = HEAD_DIM // 2
    u = (_rms(h_ref[...]) * g_ref[...]).astype(BF16)
    lo32, lo64 = _rope_rows((tm, LANES))
    _proj_queries(u, w_ref, cos_ref[...], sin_ref[...], lo32, lo64, qa16_ref, qb16_ref, qb32_ref, qcz16_ref)

    p = _nn(u, w_ref[:, _O_VA:_O_VA + A_W])
    va16_ref[...] = p.astype(BF16)
    for h in range(HEADS_A):
        va_ref[pl.ds(h, tm, stride=HEADS_A), :] = p[:, h * LANES:(h + 1) * LANES]

    cos_t = cos_t_ref[...]
    sin_t = sin_t_ref[...]

    def rope_t(x):
        x1, x2 = x[0:half], x[half:HEAD_DIM]
        return jnp.concatenate([x1 * cos_t - x2 * sin_t, x2 * cos_t + x1 * sin_t], axis=0)

    def store_t(f32_ref, b16_ref, row0, val):
        rows = val.shape[0]
        if f32_ref is not None:
            f32_ref[row0:row0 + rows, :] = val
        if b16_ref is not None:
            for t in range(tm // TQ):
                b16_ref[t, row0:row0 + rows, :] = val[:, t * TQ:(t + 1) * TQ].astype(BF16)

    pt = _nt(wt_ref[0:A_W, :], u)
    for g in range(A_W // HEAD_DIM):
        store_t(kat_ref, kat16_ref, g * HEAD_DIM, rope_t(pt[g * HEAD_DIM:(g + 1) * HEAD_DIM]))
    pt = _nt(wt_ref[A_W:A_W + 2 * B_W, :], u)
    for g in range(B_W // HEAD_DIM):
        store_t(bkvt_ref, bkvt16_ref, g * HEAD_DIM, rope_t(pt[g * HEAD_DIM:(g + 1) * HEAD_DIM]))
    store_t(bkvt_ref, bkvt16_ref, B_W, pt[B_W:2 * B_W])
    pt = _nt(wt_ref[A_W + 2 * B_W:_KT_ROWS, :], u)
    for g in range(KVC_W // HEAD_DIM):
        x = pt[g * HEAD_DIM:(g + 1) * HEAD_DIM]
        if g % 2 == 0:
            x = rope_t(x)
        if g < 2:
            store_t(ckvt_ref, None, g * HEAD_DIM, x)
        elif g < 4:
            store_t(ckvt_ref, None, g * HEAD_DIM, x)
            store_t(None, selt16_ref, (g - 2) * HEAD_DIM, x)
        else:
            store_t(wint_ref, wint16_ref, (g - 4) * HEAD_DIM, x)
    gate_ref[...] = jax.nn.sigmoid(_nn(u, wgate_ref[...]))


def _proj_prompt(h, g_mix, w_main, w_t, w_gate, tabs, n, s, l, stacked):
    cos, sin, cos_t, sin_t = tabs
    m = n * s
    tm = TM_PROJ
    per_seq = s // tm
    nt = tm // TQ
    row = lambda w: pl.BlockSpec((tm, w), lambda i: (i, 0))
    full = lambda shape: pl.BlockSpec(shape, lambda i: (0,) * len(shape))
    tab = pl.BlockSpec((tm, LANES), lambda i: (i % per_seq, 0))
    tab_t = pl.BlockSpec((HEAD_DIM // 2, tm), lambda i: (0, i % per_seq))
    feat = lambda w: pl.BlockSpec((None, None, w, tm), lambda i: (l, i // per_seq, 0, i % per_seq))
    tiles = lambda w: pl.BlockSpec((None, nt, w, TQ), lambda i: (i // per_seq, i % per_seq, 0, 0))
    sds = jax.ShapeDtypeStruct
    out_shape = [sds(a.shape, a.dtype) for a in stacked]
    out_shape += [sds((m, A_W), BF16), sds((n, s // TQ, A_W, TQ), BF16), sds((m, A_W), BF16),
                  sds((m, B_W), BF16), sds((m, B_W), F32), sds((n, s // TQ, 2 * B_W, TQ), BF16),
                  sds((m, 2 * C_W), BF16), sds((n, s // TQ, 2 * HEAD_DIM, TQ), BF16),
                  sds((n, s // TQ, 2 * HEAD_DIM, TQ), BF16), sds((m, LANES), F32)]
    out_specs = [feat(A_W), pl.BlockSpec((None, tm * HEADS_A, LANES), lambda i: (l, i, 0)), feat(2 * B_W),
                 feat(4 * HEAD_DIM), feat(2 * HEAD_DIM),
                 row(A_W), tiles(A_W), row(A_W), row(B_W), row(B_W), tiles(2 * B_W),
                 row(2 * C_W), tiles(2 * HEAD_DIM), tiles(2 * HEAD_DIM), row(LANES)]
    in_specs = [row(D_MODEL), full((1, D_MODEL)), full((D_MODEL, MAIN_W)), full((_KT_ROWS, D_MODEL)),
                full((D_MODEL, LANES)), tab, tab, tab_t, tab_t]
    n_in = len(in_specs)
    in_specs += [pl.BlockSpec(memory_space=pl.ANY)] * len(stacked)
    return pl.pallas_call(
        _proj_prompt_body,
        out_shape=out_shape,
        grid=(m // tm,),
        in_specs=in_specs,
        out_specs=out_specs,
        input_output_aliases={n_in + k: k for k in range(len(stacked))},
        compiler_params=_params(1),
        name="in_proj_rope_prompt",
    )(h, g_mix.reshape(1, D_MODEL), w_main, w_t, w_gate, cos, sin, cos_t, sin_t, *stacked)


def _rank_lower(x, n, width_iota):
    rank = jnp.zeros(x.shape, F32)
    for bp in range(n):
        col = x[:, bp:bp + 1]
        tie = jnp.where(bp < width_iota, 1.0, 0.0)
        rank = rank + jnp.where(col > x, 1.0, jnp.where(col == x, tie, 0.0))
    return rank


def _rank_lower_t(x, n, row_iota):
    rank = jnp.zeros(x.shape, F32)
    for bp in range(n):
        row = x[bp:bp + 1, :]
        tie = jnp.where(bp < row_iota, 1.0, 0.0)
        rank = rank + jnp.where(row > x, 1.0, jnp.where(row == x, tie, 0.0))
    return rank


def _head_rms_scale(o, gh, lam_init):
    return _rms(o) * gh * (1.0 - lam_init)


def _attn_a_prompt_body(q_ref, kt_ref, v_ref, lam_ref, gh_ref, o_ref, *, lam_init):
    i = pl.program_id(1)
    tq = q_ref.shape[0]
    lam = _lam_value(lam_ref, lam_init)
    gh = gh_ref[...]
    lane = _iota((tq, LANES), 1)
    causal = _iota((tq, tq), 1) <= _iota((tq, tq), 0)
    head_sl = [slice(h * LANES, (h + 1) * LANES) for h in range(HEADS_A)]
    chains = [(h, c) for h in range(HEADS_A) for c in range(2)]
    qms = []
    for h, c in chains:
        q2 = q_ref[:, head_sl[h]]
        in_c = (lane >= c * HEAD_DIM) & (lane < (c + 1) * HEAD_DIM)
        qms.append(jnp.where(in_c, q2, jnp.zeros_like(q2)))

    def rows(j):
        return pl.ds(pl.multiple_of(j * tq, tq), tq)

    ones = jnp.ones((tq, LANES), BF16)

    def v_ext(j, h):
        return jnp.concatenate([v_ref[rows(j), head_sl[h]], ones], axis=1)

    for h0 in range(0, HEADS_A, A_HEADS_PER_LOOP):
        group = [(2 * h + c, h) for h in range(h0, h0 + A_HEADS_PER_LOOP) for c in range(2)]
        state = []
        for idx, h in group:
            s = jnp.where(causal, _nn(qms[idx], kt_ref[i, head_sl[h], :]), NEG_INF)
            m = jnp.max(s, axis=-1, keepdims=True)
            p = jnp.exp((s - m).astype(BF16))
            state += [m, _nn(p, v_ext(i, h))]

        def body(j, carry, group=group):
            out = []
            for k, (idx, h) in enumerate(group):
                m, acc = carry[2 * k:2 * k + 2]
                s = _nn(qms[idx], kt_ref[j, head_sl[h], :])
                m_new = jnp.maximum(m, jnp.max(s, axis=-1, keepdims=True))
                p = jnp.exp((s - m_new).astype(BF16))
                out += [m_new, jnp.exp(m - m_new) * acc + _nn(p, v_ext(j, h))]
            return tuple(out)

        state = lax.fori_loop(0, i, body, tuple(state))
        for k in range(A_HEADS_PER_LOOP):
            a0, a1 = state[4 * k + 1], state[4 * k + 3]
            o0 = a0[:, :LANES] / a0[:, LANES:]
            o1 = a1[:, :LANES] / a1[:, LANES:]
            o_ref[:, head_sl[h0 + k]] = _head_rms_scale(o0 - lam * o1, gh, lam_init).astype(BF16)


def _attn_a_prompt(q16, kt16, v16, lam_p, g_head, n, s, lidx):
    nq = s // TQ
    qspec = pl.BlockSpec((TQ, A_W), lambda b, i: (b * nq + i, 0))
    return pl.pallas_call(
        functools.partial(_attn_a_prompt_body, lam_init=_lam_init(lidx)),
        out_shape=jax.ShapeDtypeStruct((n * s, A_W), BF16),
        grid=(n, nq),
        in_specs=[qspec,
                  pl.BlockSpec((None, nq, A_W, TQ), lambda b, i: (b, 0, 0, 0)),
                  pl.BlockSpec((s, A_W), lambda b, i: (b, 0)),
                  pl.BlockSpec((4, HEAD_DIM), lambda b, i: (0, 0)),
                  pl.BlockSpec((1, 2 * HEAD_DIM), lambda b, i: (0, 0))],
        out_specs=qspec,
        compiler_params=_params(2),
        name="diff_attn_prompt",
    )(q16, kt16, v16, lam_p, g_head.reshape(1, 2 * HEAD_DIM))


def _block_means_t(blocks):
    feats = blocks[0].shape[0]
    lane = _iota((feats, LANES), 1)
    out = jnp.zeros((feats, LANES), F32)
    for b, blk in enumerate(blocks):
        out = jnp.where(lane == b, jnp.sum(blk, axis=1, keepdims=True) * (1.0 / MOBA_BLOCK), out)
    return out


def _attn_b_prompt_body(q_ref, q32_ref, k32t_ref, kvt_ref, o_ref, kmean_ref):
    i = pl.program_id(1)
    tq = q_ref.shape[0]
    nb = k32t_ref.shape[1] // MOBA_BLOCK

    @pl.when(i == 0)
    def _():
        kmean_ref[...] = _block_means_t([k32t_ref[:, b * MOBA_BLOCK:(b + 1) * MOBA_BLOCK] for b in range(nb)])

    lane = _iota((tq, LANES), 1)
    causal = _iota((tq, tq), 1) <= _iota((tq, tq), 0)
    nb_rows = -(-nb // 8) * 8
    blk_t = _iota((nb_rows, tq), 0)
    past_t = blk_t < i
    ksl = [slice((h // 2) * LANES, (h // 2 + 1) * LANES) for h in range(HEADS_B)]
    vsl = [slice(B_W + (h // 2) * LANES, B_W + (h // 2 + 1) * LANES) for h in range(HEADS_B)]
    ones_t = jnp.ones((HEAD_DIM, tq), BF16)

    def v_ones(j, h):
        vt = kvt_ref[j, vsl[h], :]
        if h % 2 == 0:
            return jnp.concatenate([vt[0:HEAD_DIM], ones_t], axis=0)
        return jnp.concatenate([ones_t, vt[HEAD_DIM:2 * HEAD_DIM]], axis=0)

    qms, chosen, state = [], [], []
    for h in range(HEADS_B):
        q2 = q_ref[:, ksl[h]]
        in_h = (lane >= (h % 2) * HEAD_DIM) & (lane < (h % 2 + 1) * HEAD_DIM)
        qms.append(jnp.where(in_h, q2, jnp.zeros_like(q2)))
        gate = _nn_precise(jnp.where(in_h, q32_ref[:, ksl[h]], 0.0), kmean_ref[ksl[h], :])
        gate_t = jnp.where(past_t, gate.T[0:nb_rows], NEG_INF)
        pick_t = jnp.where((_rank_lower_t(gate_t, nb, blk_t) < MOBA_TOPK) & past_t, 1.0, 0.0)
        chosen.append(jnp.concatenate([pick_t, jnp.zeros((LANES - nb_rows, tq), F32)], axis=0).T)
        s = jnp.where(causal, _nn(qms[h], kvt_ref[i, ksl[h], :]), NEG_INF)
        m = jnp.max(s, axis=-1, keepdims=True)
        state += [m, _nt(jnp.exp((s - m).astype(BF16)), v_ones(i, h))]

    def body(j, carry):
        out = []
        for h in range(HEADS_B):
            m, acc = carry[2 * h:2 * h + 2]
            use = jnp.max(jnp.where(lane == j, chosen[h], 0.0), axis=1, keepdims=True)
            s = jnp.where(use > 0.5, _nn(qms[h], kvt_ref[j, ksl[h], :]), NEG_INF)
            m_new = jnp.maximum(m, jnp.max(s, axis=-1, keepdims=True))
            p = jnp.exp((s - m_new).astype(BF16))
            out += [m_new, jnp.exp(m - m_new) * acc + _nt(p, v_ones(j, h))]
        return tuple(out)

    state = lax.fori_loop(0, i, body, tuple(state))
    for pair in range(HEADS_B // 2):
        a0, a1 = state[4 * pair + 1], state[4 * pair + 3]
        o0 = a0 / pltpu.roll(a0, HEAD_DIM, 1)
        o1 = a1 / pltpu.roll(a1, HEAD_DIM, 1)
        o_ref[:, ksl[2 * pair]] = jnp.where(lane < HEAD_DIM, o0, o1).astype(BF16)


def _attn_b_prompt(q16, q32, bkvt32, bkvt16, n, s, l):
    nq = s // TQ
    assert TQ == MOBA_BLOCK and s // MOBA_BLOCK <= LANES
    qspec = pl.BlockSpec((TQ, B_W), lambda b, i: (b * nq + i, 0))
    return pl.pallas_call(
        _attn_b_prompt_body,
        out_shape=jax.ShapeDtypeStruct((n * s, B_W), BF16),
        grid=(n, nq),
        in_specs=[qspec, qspec,
                  pl.BlockSpec((None, None, B_W, s), lambda b, i: (l, b, 0, 0)),
                  pl.BlockSpec((None, nq, 2 * B_W, TQ), lambda b, i: (b, 0, 0, 0))],
        out_specs=qspec,
        scratch_shapes=[pltpu.VMEM((B_W, LANES), F32)],
        compiler_params=_params(2),
        name="moba_attn_prompt",
    )(q16, q32, bkvt32, bkvt16)


_CMP_ROWS = 128


def _compress_core(xs_ref, pe_ref, w1_ref, w2_ref):
    half = CMP_LEN // 2
    n_chunk = xs_ref.shape[0] // CMP_STRIDE
    acc = [None, None]
    def rows(r):
        return (xs_ref[pl.ds(r % half, n_chunk, stride=CMP_STRIDE), :] + pe_ref[r:r + 1, :]).astype(BF16)

    for r in range(0, CMP_LEN, 2):
        t = _nn(jnp.concatenate([rows(r), rows(r + 1)], axis=1), w1_ref[r // 2])
        acc[r // half] = t if acc[r // half] is None else acc[r // half] + t
    pre = acc[0] + pltpu.roll(acc[1], n_chunk - 1, 0)
    hid = jax.nn.gelu(pre).astype(BF16)
    return _nn(hid, w2_ref[...]).astype(BF16)


def _compress_prompt_body(xt_ref, pe_ref, w1_ref, w2_ref, o_ref, xs_ref):
    for j in range(xt_ref.shape[1] // LANES):
        xs_ref[j * LANES:(j + 1) * LANES, :] = xt_ref[:, j * LANES:(j + 1) * LANES].T
    o_ref[...] = _compress_core(xs_ref, pe_ref, w1_ref, w2_ref)


def _compress_weights(w_cmp1, w_cmp2, cmp_pos):
    w1 = w_cmp1.reshape(2, CMP_LEN, HEAD_DIM, CMP_HIDDEN)
    z1 = jnp.zeros((CMP_LEN, HEAD_DIM, CMP_HIDDEN), F32)
    top = jnp.concatenate([w1[0], z1], axis=2)
    bot = jnp.concatenate([z1, w1[1]], axis=2)
    w1c = jnp.concatenate([top, bot], axis=1).astype(BF16)
    w1c = w1c.reshape(CMP_LEN // 2, 2 * LANES, 2 * CMP_HIDDEN)
    z2 = jnp.zeros((CMP_HIDDEN, HEAD_DIM), F32)
    w2c = jnp.concatenate([jnp.concatenate([w_cmp2[0], z2], axis=1),
                           jnp.concatenate([z2, w_cmp2[1]], axis=1)], axis=0).astype(BF16)
    pe = jnp.concatenate([cmp_pos[0], cmp_pos[1]], axis=1)
    return pe, w1c, w2c


def _const_specs(index):
    return [pl.BlockSpec((CMP_LEN, LANES), index(2)),
            pl.BlockSpec((CMP_LEN // 2, 2 * LANES, 2 * CMP_HIDDEN), index(3)),
            pl.BlockSpec((2 * CMP_HIDDEN, LANES), index(2))]


def _compress_prompt(ckvt32, cw, n, s, l):
    assert s // CMP_STRIDE == _CMP_ROWS
    zero = lambda nd: (lambda b: (0,) * nd)
    return pl.pallas_call(
        _compress_prompt_body,
        out_shape=jax.ShapeDtypeStruct((n * _CMP_ROWS, LANES), BF16),
        grid=(n,),
        in_specs=[pl.BlockSpec((None, None, LANES, s), lambda b: (l, b, 0, 0))] + _const_specs(zero),
        out_specs=pl.BlockSpec((_CMP_ROWS, LANES), lambda b: (b, 0)),
        scratch_shapes=[pltpu.VMEM((s, LANES), F32)],
        compiler_params=_params(1),
        name="nsa_compress_prompt",
    )(ckvt32, *cw)


def _page_index(l, g, group, j, row_block, b, pt):
    return (l, pt[b * group + g, j], row_block, 0)


def _page_specs(l, n_pages, rows, row_block=0, group=1):
    return [pl.BlockSpec((None, None, rows, PAGE_SIZE), functools.partial(_page_index, l, g, group, j, row_block))
            for g in range(group) for j in range(n_pages)]


def _overlap_matrix(t_len):
    n_cmp = (t_len - CMP_LEN) // CMP_STRIDE + 1
    nsb = -(-t_len // SEL_BLOCK)
    starts = np.arange(n_cmp) * CMP_STRIDE
    sb = np.arange(nsb) * SEL_BLOCK
    ov = np.clip(np.minimum(starts[:, None] + CMP_LEN, sb[None, :] + SEL_BLOCK)
                 - np.maximum(starts[:, None], sb[None, :]), 0, None) / CMP_STRIDE
    out = np.zeros((_CMP_ROWS, LANES), np.float32)
    out[:n_cmp, :nsb] = ov
    return jnp.asarray(out, BF16), nsb


def _expand_matrix(s):
    nt = s // TQ
    e = np.zeros((nt, LANES, TQ), np.float32)
    for j in range(nt):
        for k in range(TQ):
            e[j, (j * TQ + k) // SEL_BLOCK, k] = 1.0
    return jnp.asarray(e, BF16)


def _nsa_flags(p_sum, ovl, own, lane, nsb):
    imp = None
    for part in _split3(p_sum):
        t = _nn(part, ovl)
        imp = t if imp is None else imp + t
    forced = (lane == 0) | (lane == own) | (lane == own - 1)
    imp = jnp.where(lane > own, NEG_INF, jnp.where(forced, jnp.inf, imp))
    rank = _rank_lower(imp, nsb, lane)
    return jnp.where((rank < SEL_TOPK) & (lane <= own), 1.0, 0.0)


def _nsa_flags_t(p_sum, ovl, pos0, nsb):
    tq = p_sum.shape[0]
    imp = None
    for part in _split3(p_sum):
        t = _nn(part, ovl)
        imp = t if imp is None else imp + t
    rows = -(-nsb // 8) * 8
    imp_t = imp.T[0:rows]
    blk = _iota((rows, tq), 0)
    own = (pos0 + _iota((rows, tq), 1)) // SEL_BLOCK
    forced = (blk == 0) | (blk == own) | (blk == own - 1)
    imp_t = jnp.where(blk > own, NEG_INF, jnp.where(forced, jnp.inf, imp_t))
    rank = _rank_lower_t(imp_t, nsb, blk)
    flag_t = jnp.where((rank < SEL_TOPK) & (blk <= own), 1.0, 0.0)
    flag_t = jnp.concatenate([flag_t, jnp.zeros((LANES - rows, tq), F32)], axis=0)
    return flag_t.T


def _masked_probs(s, mask):
    s = jnp.where(mask, s, NEG_INF)
    m = jnp.max(s, axis=-1, keepdims=True)
    m = jnp.where(m > NEG_INF, m, 0.0)
    e = jnp.exp(s - m)
    d = jnp.sum(e, axis=-1, keepdims=True)
    return e / jnp.where(d > 0, d, 1.0)


def _attn_c_prompt_body(qz_ref, gate_ref, selt_ref, wint_ref, kvc_ref, ovl_ref, exp_ref, o_ref, *, nsb):
    i = pl.program_id(1)
    tq = gate_ref.shape[0]
    nh = HEADS_C
    q4 = jnp.concatenate([qz_ref[:, h * LANES:(h + 1) * LANES] for h in range(nh)], axis=0)
    lane = _iota((tq, LANES), 1)
    pos = i * tq + _iota((tq, LANES), 0)
    rr = _iota((tq, tq), 0)
    cc = _iota((tq, tq), 1)
    causal = cc <= rr

    kvc = kvc_ref[...]
    cmp_ok = (CMP_STRIDE * lane + (CMP_LEN - 1)) <= pos
    s = _nt(q4, kvc).reshape(nh, tq, LANES)
    p = _masked_probs(s, cmp_ok[None])
    o_cmp = _nn(p.reshape(nh * tq, LANES).astype(BF16), kvc)
    p_sum = p[0] + p[1] + p[2] + p[3]

    flag = _nsa_flags_t(p_sum, ovl_ref[...], i * tq, nsb).astype(BF16)

    ones_t = jnp.ones((HEAD_DIM, tq), BF16)

    def pv_tile(kj):
        return jnp.concatenate([ones_t, kj[HEAD_DIM:2 * HEAD_DIM]], axis=0)

    kd = selt_ref[i]
    ok = (_nn(flag, exp_ref[i]) > 0.5) & causal
    s = jnp.where(ok[None], _nn(q4, kd).reshape(nh, tq, tq), NEG_INF)
    m = jnp.max(s, axis=-1, keepdims=True)
    p = jnp.exp((s - m).astype(BF16))
    acc = _nt(p.reshape(nh * tq, tq), pv_tile(kd)).reshape(nh, tq, LANES)

    def body(j, carry):
        m, acc = carry
        kj = selt_ref[j]
        ok = _nn(flag, exp_ref[j]) > 0.5
        s = jnp.where(ok[None], _nn(q4, kj).reshape(nh, tq, tq), NEG_INF)
        m_new = jnp.maximum(m, jnp.max(s, axis=-1, keepdims=True))
        p = jnp.exp((s - m_new).astype(BF16))
        pv = _nt(p.reshape(nh * tq, tq), pv_tile(kj)).reshape(nh, tq, LANES)
        return m_new, jnp.exp(m - m_new) * acc + pv

    m, acc = lax.fori_loop(0, i, body, (m, acc))
    o_sel = acc / pltpu.roll(acc, HEAD_DIM, 2)

    assert WINDOW == 2 * tq
    w2 = wint_ref[jnp.maximum(i - 2, 0)]
    w1 = wint_ref[jnp.maximum(i - 1, 0)]
    w0 = wint_ref[i]
    s2 = jnp.where(((cc >= rr) & (i >= 2))[None], _nn(q4, w2).reshape(nh, tq, tq), NEG_INF)
    s1 = jnp.where(i >= 1, _nn(q4, w1).reshape(nh, tq, tq), NEG_INF)
    s0 = jnp.where(causal[None], _nn(q4, w0).reshape(nh, tq, tq), NEG_INF)
    s_all = jnp.concatenate([s2, s1, s0], axis=-1)
    m = jnp.max(s_all, -1, keepdims=True)
    e = jnp.exp((s_all - m).astype(BF16)).reshape(nh * tq, 3 * tq)
    acc = (_nt(e[:, 0:tq], pv_tile(w2)) + _nt(e[:, tq:2 * tq], pv_tile(w1))
           + _nt(e[:, 2 * tq:3 * tq], pv_tile(w0)))
    o_win = (acc / pltpu.roll(acc, HEAD_DIM, 1)).reshape(nh, tq, LANES)

    o_cmp = o_cmp.reshape(nh, tq, LANES)
    g = gate_ref[...]
    heads = []
    for h in range(nh):
        heads.append(g[:, 3 * h:3 * h + 1] * o_cmp[h] + g[:, 3 * h + 1:3 * h + 2] * o_sel[h]
                     + g[:, 3 * h + 2:3 * h + 3] * o_win[h])
    for pair in range(nh // 2):
        both = jnp.where(lane < HEAD_DIM, pltpu.roll(heads[2 * pair], HEAD_DIM, 1), heads[2 * pair + 1])
        o_ref[:, pair * LANES:(pair + 1) * LANES] = both.astype(BF16)


def _attn_c_prompt(qz16, gates, selt16, wint16, kvcmp16, n, s):
    nq = s // TQ
    ovl, nsb = _overlap_matrix(s)
    expand = _expand_matrix(s)
    qrow = lambda w: pl.BlockSpec((TQ, w), lambda b, i: (b * nq + i, 0))
    tiles = pl.BlockSpec((None, nq, 2 * HEAD_DIM, TQ), lambda b, i: (b, 0, 0, 0))
    return pl.pallas_call(
        functools.partial(_attn_c_prompt_body, nsb=nsb),
        out_shape=jax.ShapeDtypeStruct((n * s, C_W), BF16),
        grid=(n, nq),
        in_specs=[qrow(2 * C_W), qrow(LANES), tiles, tiles,
                  pl.BlockSpec((_CMP_ROWS, LANES), lambda b, i: (b, 0)),
                  pl.BlockSpec((_CMP_ROWS, LANES), lambda b, i: (0, 0)),
                  pl.BlockSpec((nq, LANES, TQ), lambda b, i: (0, 0, 0))],
        out_specs=qrow(C_W),
        compiler_params=_params(2),
        name="nsa_attn_prompt",
    )(qz16, gates, selt16, wint16, kvcmp16, ovl, expand)


def _pad_page(x):
    rows, w = x.shape
    return jnp.concatenate([x, jnp.zeros((PAGE_SIZE - rows, w), x.dtype)], axis=0)


def _new_page_mask(n_rows, nq):
    r = _iota((n_rows, PAGE_SIZE), 0) % nq
    t = _iota((n_rows, PAGE_SIZE), 1)
    return t <= r


def _dec_a_body(pt_ref, q_ref, kn_ref, vn_ref, lam_ref, gh_ref, *refs, lam_init):
    n_pages = (len(refs) - 1) // 2
    kp, vp, o_ref = refs[:n_pages], refs[n_pages:2 * n_pages], refs[-1]
    nq = q_ref.shape[0]
    rows = 2 * HEADS_A * nq
    per_head = 2 * nq
    lam = _lam_value(lam_ref, lam_init)
    qt = jnp.concatenate([q_ref[...]] * (2 * HEADS_A), axis=0)
    diag = (_iota((rows, A_W), 0) // nq) == (_iota((rows, A_W), 1) // HEAD_DIM)
    qbd = jnp.where(diag, qt, jnp.zeros_like(qt))
    kn = _pad_page(kn_ref[...]).astype(BF16)
    vn = _pad_page(vn_ref[...]).astype(BF16)
    s_new = jnp.where(_new_page_mask(rows, nq), _nt(qbd, kn), NEG_INF)
    scores = [_nn(qbd, kp[j][...].astype(BF16)) for j in range(n_pages)]
    m = jnp.max(s_new, axis=1, keepdims=True)
    for s in scores:
        m = jnp.maximum(m, jnp.max(s, axis=1, keepdims=True))
    e = jnp.exp(s_new - m)
    d = jnp.sum(e, axis=1, keepdims=True)
    e = e.astype(BF16)
    accs = [_nn(e[h * per_head:(h + 1) * per_head], vn[:, h * LANES:(h + 1) * LANES]) for h in range(HEADS_A)]
    for j in range(n_pages):
        e = jnp.exp(scores[j] - m)
        d = d + jnp.sum(e, axis=1, keepdims=True)
        e = e.astype(BF16)
        for h in range(HEADS_A):
            vh = vp[j][pl.ds(h, PAGE_SIZE, stride=HEADS_A), :].astype(BF16)
            accs[h] = accs[h] + _nn(e[h * per_head:(h + 1) * per_head], vh)
    gh = gh_ref[...]
    for h in range(HEADS_A):
        on = accs[h] / d[h * per_head:(h + 1) * per_head]
        o = on[0:nq] - lam * on[nq:2 * nq]
        o_ref[:, h * LANES:(h + 1) * LANES] = _head_rms_scale(o, gh, lam_init).astype(BF16)


def _dec_a(q16, kn32, vn32, lam_p, g_head, cache_kt, cache_v4, page_table, l, nq):
    n, n_pages = page_table.shape
    row = lambda w: pl.BlockSpec((nq, w), lambda b, pt: (b, 0))
    grid_spec = pltpu.PrefetchScalarGridSpec(
        num_scalar_prefetch=1, grid=(n,),
        in_specs=[row(A_W), row(A_W), row(A_W),
                  pl.BlockSpec((4, HEAD_DIM), lambda b, pt: (0, 0)),
                  pl.BlockSpec((1, 2 * HEAD_DIM), lambda b, pt: (0, 0))]
                 + _page_specs(l, n_pages, A_W) + _page_specs(l, n_pages, A_W),
        out_specs=row(A_W))
    return pl.pallas_call(
        functools.partial(_dec_a_body, lam_init=_lam_init(l)),
        out_shape=jax.ShapeDtypeStruct((n * nq, A_W), BF16),
        grid_spec=grid_spec,
        compiler_params=_params(1),
        name="diff_attn_decode",
    )(page_table, q16, kn32, vn32, lam_p, g_head.reshape(1, 2 * HEAD_DIM),
      *([cache_kt] * n_pages), *([cache_v4] * n_pages))


def _dec_b_body(pt_ref, q_ref, q32_ref, kvn_ref, *refs, group):
    pages, o_ref = refs[:-1], refs[-1]
    n_pages = len(pages) // group
    nq = q_ref.shape[0] // group
    for g in range(group):
        r = slice(g * nq, (g + 1) * nq)
        o_ref[r, :] = _dec_b_one(q_ref[r, :], q32_ref[r, :], kvn_ref[r, :],
                                 pages[g * n_pages:(g + 1) * n_pages]).astype(BF16)


def _dec_b_one(q, q32, kvn_rows, pages):
    n_pages = len(pages)
    nq = q.shape[0]
    rows = HEADS_B * nq
    pages_per_blk = MOBA_BLOCK // PAGE_SIZE
    nb = n_pages // pages_per_blk
    diag = (_iota((rows, B_W), 0) // nq) == (_iota((rows, B_W), 1) // HEAD_DIM)
    qt = jnp.concatenate([q] * HEADS_B, axis=0)
    qbd = jnp.where(diag, qt, jnp.zeros_like(qt))
    qbd32 = jnp.where(diag, jnp.concatenate([q32] * HEADS_B, axis=0), 0.0)

    blocks = []
    for b in range(nb):
        blocks.append(jnp.concatenate([pages[j][0:B_W, :] for j in range(b * pages_per_blk, (b + 1) * pages_per_blk)],
                                      axis=1))
    kmean_t = _block_means_t(blocks)
    lane = _iota((rows, LANES), 1)
    gate = _nn_precise(qbd32, kmean_t)
    gate = jnp.where(lane < nb, gate, NEG_INF)
    chosen = jnp.where((_rank_lower(gate, nb, lane) < MOBA_TOPK) & (lane < nb), 1.0, 0.0)

    kvn = _pad_page(kvn_rows).astype(BF16)
    s_new = jnp.where(_new_page_mask(rows, nq), _nt(qbd, kvn[:, 0:B_W]), NEG_INF)
    scores = []
    for j in range(n_pages):
        b = j // pages_per_blk
        s = _nn(qbd, pages[j][0:B_W, :].astype(BF16))
        scores.append(jnp.where(chosen[:, b:b + 1] > 0.5, s, NEG_INF))
    m = jnp.max(s_new, axis=1, keepdims=True)
    for s in scores:
        m = jnp.maximum(m, jnp.max(s, axis=1, keepdims=True))
    e = jnp.exp(s_new - m)
    d = jnp.sum(e, axis=1, keepdims=True)
    acc = _nn(e.astype(BF16), kvn[:, B_W:2 * B_W])
    for j in range(n_pages):
        e = jnp.exp(scores[j] - m)
        d = d + jnp.sum(e, axis=1, keepdims=True)
        acc = acc + _nt(e.astype(BF16), pages[j][B_W:2 * B_W, :].astype(BF16))
    on = acc / d
    head_of_lane = _iota((nq, B_W), 1) // HEAD_DIM
    o = jnp.zeros((nq, B_W), F32)
    for h in range(HEADS_B):
        o = o + jnp.where(head_of_lane == h, on[h * nq:(h + 1) * nq], 0.0)
    return o


def _dec_b(q16, q32, kvn32, cache_kvt, page_table, l, nq):
    n, n_pages = page_table.shape
    group = DEC_GROUP
    row = lambda w: pl.BlockSpec((group * nq, w), lambda b, pt: (b, 0))
    grid_spec = pltpu.PrefetchScalarGridSpec(
        num_scalar_prefetch=1, grid=(n // group,),
        in_specs=[row(B_W), row(B_W), row(2 * B_W)] + _page_specs(l, n_pages, 2 * B_W, group=group),
        out_specs=row(B_W))
    return pl.pallas_call(
        functools.partial(_dec_b_body, group=group),
        out_shape=jax.ShapeDtypeStruct((n * nq, B_W), BF16),
        grid_spec=grid_spec,
        compiler_params=_params(1),
        name="moba_attn_decode",
    )(page_table, q16, q32, kvn32, *([cache_kvt] * (n_pages * group)))


def _dec_c_body(pt_ref, qz_ref, gate_ref, ckvn_ref, winn_ref, pe_ref, w1_ref, w2_ref, ovl_ref, exp_ref, st_ref,
                *refs, nsb, q0, group):
    pages, o_ref, xs_ref = refs[:-2], refs[-2], refs[-1]
    n_pages = len(pages) // group
    nq = gate_ref.shape[0] // group
    for j, pg in enumerate(pages):
        xs_ref[j * PAGE_SIZE:(j + 1) * PAGE_SIZE, :] = pg[0:LANES, :].T
    kvc_all = _compress_core(xs_ref, pe_ref, w1_ref, w2_ref)
    ovl = ovl_ref[...]
    expand = exp_ref[...]
    for g in range(group):
        r = slice(g * nq, (g + 1) * nq)
        tiles = [pg[LANES:2 * LANES, :] for pg in pages[g * n_pages:(g + 1) * n_pages]]
        _dec_c_one(qz_ref[r, :], gate_ref[r, :], ckvn_ref[r, :], winn_ref[r, :],
                   kvc_all[g * _CMP_ROWS:(g + 1) * _CMP_ROWS], ovl, expand, st_ref[g], tiles, o_ref, r, nsb, q0)


def _dec_c_one(qz, g, ckvn_rows, winn_rows, kvc, ovl, expand, st32, tiles, o_ref, out_rows, nsb, q0):
    n_pages = len(tiles)
    nq = g.shape[0]
    nh = HEADS_C
    rows = nh * nq
    q4 = jnp.concatenate([qz[:, h * LANES:(h + 1) * LANES] for h in range(nh)], axis=0)
    lane = _iota((nq, LANES), 1)
    pos = q0 + _iota((nq, LANES), 0)
    lane4 = _iota((rows, LANES), 1)
    qrow4 = _iota((rows, LANES), 0) % nq
    new_ok = _new_page_mask(rows, nq)

    cmp_ok = (CMP_STRIDE * lane4 + (CMP_LEN - 1)) <= (q0 + qrow4)
    p = _masked_probs(_nt(q4, kvc), cmp_ok)
    o_cmp = _nn(p.astype(BF16), kvc)
    p_sum = p[0:nq]
    for h in range(1, nh):
        p_sum = p_sum + p[h * nq:(h + 1) * nq]

    own = pos // SEL_BLOCK
    flag = _nsa_flags(p_sum, ovl, own, lane, nsb)
    flag4 = jnp.concatenate([flag] * nh, axis=0)
    blk_per_page = PAGE_SIZE // SEL_BLOCK
    assert blk_per_page == 2

    ckvn = _pad_page(ckvn_rows).astype(BF16)
    kn = ckvn[:, LANES:2 * LANES]
    own_blk = n_pages * blk_per_page
    s_new = jnp.where(new_ok & (flag4[:, own_blk:own_blk + 1] > 0.5), _nt(q4, kn), NEG_INF)
    kt_all = jnp.concatenate([t.astype(BF16) for t in tiles], axis=1)
    ok = _nn(flag4.astype(BF16), expand) > 0.5
    s_old = jnp.where(ok, _nn(q4, kt_all), NEG_INF)
    m = jnp.maximum(jnp.max(s_new, axis=1, keepdims=True), jnp.max(s_old, axis=1, keepdims=True))
    e_new = jnp.exp(s_new - m)
    e_old = jnp.exp(s_old - m)
    d = jnp.sum(e_new, axis=1, keepdims=True) + jnp.sum(e_old, axis=1, keepdims=True)
    acc = _nn(e_new.astype(BF16), kn) + _nt(e_old.astype(BF16), kt_all)
    o_sel = acc / d

    wn = _pad_page(winn_rows).astype(BF16)
    s_new = jnp.where(new_ok, _nt(q4, wn), NEG_INF)
    st = st32.astype(BF16)
    key = _iota((rows, WINDOW), 1)
    ok = key >= (_iota((rows, WINDOW), 0) % nq)
    s_old = jnp.where(ok, _nn(q4, st), NEG_INF)
    m = jnp.maximum(jnp.max(s_new, axis=1, keepdims=True), jnp.max(s_old, axis=1, keepdims=True))
    e_new = jnp.exp(s_new - m)
    e_old = jnp.exp(s_old - m)
    d = jnp.sum(e_new, axis=1, keepdims=True) + jnp.sum(e_old, axis=1, keepdims=True)
    o_win = (_nn(e_new.astype(BF16), wn) + _nt(e_old.astype(BF16), st)) / d

    heads = []
    for h in range(nh):
        r = slice(h * nq, (h + 1) * nq)
        heads.append(g[:, 3 * h:3 * h + 1] * o_cmp[r] + g[:, 3 * h + 1:3 * h + 2] * o_sel[r]
                     + g[:, 3 * h + 2:3 * h + 3] * o_win[r])
    for pair in range(nh // 2):
        both = jnp.where(lane < HEAD_DIM, pltpu.roll(heads[2 * pair], HEAD_DIM, 1), heads[2 * pair + 1])
        o_ref[out_rows, pair * LANES:(pair + 1) * LANES] = both.astype(BF16)


def _dec_c(qz16, gates, ckvn32, winn32, cw, state_t, cache_ct, page_table, l, nq, q0):
    n, n_pages = page_table.shape
    assert state_t.shape[3] == WINDOW and q0 >= WINDOW and n_pages * PAGE_SIZE // CMP_STRIDE == _CMP_ROWS
    ovl, nsb = _overlap_matrix(q0 + nq)
    keys = np.arange(n_pages * PAGE_SIZE)
    expand = jnp.asarray(np.arange(LANES)[:, None] == keys[None, :] // SEL_BLOCK, BF16)
    group = NSA_DEC_GROUP
    row = lambda w: pl.BlockSpec((group * nq, w), lambda b, pt: (b, 0))
    zero = lambda nd: (lambda b, pt: (0,) * nd)
    grid_spec = pltpu.PrefetchScalarGridSpec(
        num_scalar_prefetch=1, grid=(n // group,),
        in_specs=[row(2 * C_W), row(LANES), row(4 * HEAD_DIM), row(2 * HEAD_DIM)] + _const_specs(zero)
                 + [pl.BlockSpec((_CMP_ROWS, LANES), lambda b, pt: (0, 0)),
                    pl.BlockSpec((LANES, n_pages * PAGE_SIZE), lambda b, pt: (0, 0)),
                    pl.BlockSpec((None, group, 2 * HEAD_DIM, WINDOW), lambda b, pt: (l, b, 0, 0))]
                 + _page_specs(l, n_pages, 4 * HEAD_DIM, group=group),
        out_specs=row(C_W),
        scratch_shapes=[pltpu.VMEM((group * n_pages * PAGE_SIZE, LANES), F32)])
    return pl.pallas_call(
        functools.partial(_dec_c_body, nsb=nsb, q0=q0, group=group),
        out_shape=jax.ShapeDtypeStruct((n * nq, C_W), BF16),
        grid_spec=grid_spec,
        compiler_params=_params(1),
        name="nsa_attn_decode",
    )(page_table, qz16, gates, ckvn32, winn32, *cw, ovl, expand, state_t, *([cache_ct] * (n_pages * group)))


WIN_OUT_GROUP = 8


def _window_out_body(st_ref, new_ref, o_ref):
    group = st_ref.shape[0]
    nq = new_ref.shape[0] // group
    lane = _iota((LANES, LANES), 1)
    for g in range(group):
        new = new_ref[g * nq:(g + 1) * nq, :]
        new_t = jnp.concatenate([new, jnp.zeros((LANES - nq, LANES), F32)], axis=0).T
        slabs = [st_ref[g, :, k * LANES:(k + 1) * LANES] for k in range(WINDOW // LANES)] + [new_t]
        rolled = [pltpu.roll(x, LANES - nq, 1) for x in slabs]
        for k in range(WINDOW // LANES):
            o_ref[g, :, k * LANES:(k + 1) * LANES] = jnp.where(lane < LANES - nq, rolled[k], rolled[k + 1])


def _window_out(state_t, new_rows, nq):
    depth, n = state_t.shape[:2]
    group = WIN_OUT_GROUP
    blk = pl.BlockSpec((None, group, 2 * HEAD_DIM, WINDOW), lambda l, b: (l, b, 0, 0))
    return pl.pallas_call(
        _window_out_body,
        out_shape=jax.ShapeDtypeStruct(state_t.shape, F32),
        grid=(depth, n // group),
        in_specs=[blk, pl.BlockSpec((None, group * nq, 2 * HEAD_DIM), lambda l, b: (l, b, 0))],
        out_specs=blk,
        compiler_params=_params(2),
        name="window_state_update",
    )(state_t, new_rows)


def _rope_tables(n_pos):
    inv = ROPE_THETA ** (-jnp.arange(0, HEAD_DIM, 2, dtype=F32) / HEAD_DIM)
    ang = jnp.arange(n_pos, dtype=F32)[:, None] * inv[None, :]
    cos, sin = jnp.cos(ang), jnp.sin(ang)
    cos128 = jnp.concatenate([cos, cos, cos, cos], axis=-1)
    sin128 = jnp.concatenate([-sin, sin, -sin, sin], axis=-1)
    return cos128, sin128, cos.T, sin.T


def kernel(x_prompt, x_sample, cache_a_k, cache_a_v, cache_b_kv, cache_c_kv, state_c_win, page_table,
           w_in, w_out, g_mix, g_ffn, w_ffn_gate, w_ffn_up, w_ffn_down, diff_lambda, g_diff_head,
           w_cmp1, w_cmp2, cmp_pos, g_final):
    n_p, s_p, _ = x_prompt.shape
    n_s, s_s, _ = x_sample.shape
    n_pages = page_table.shape[1]
    past_len = n_pages * cache_a_k.shape[2]
    n_phys = cache_a_k.shape[1]
    assert cache_a_k.shape[2] == PAGE_SIZE and s_p % TM_PROJ == 0 and TM_PROJ % s_s == 0

    cos, sin, cos_t, sin_t = _rope_tables(past_len + s_s)
    tabs_p = (cos[:s_p], sin[:s_p], cos_t[:, :s_p], sin_t[:, :s_p])
    reps = TM_PROJ // s_s
    cos_s = jnp.tile(cos[past_len:past_len + s_s], (reps, 1))
    sin_s = jnp.tile(sin[past_len:past_len + s_s], (reps, 1))

    ckt = cache_a_k.transpose(0, 1, 3, 4, 5, 2).reshape(DEPTH, n_phys, A_W, PAGE_SIZE)
    cv4 = cache_a_v.reshape(DEPTH, n_phys, PAGE_SIZE * HEADS_A, 2 * HEAD_DIM)
    cbt = cache_b_kv.transpose(0, 1, 3, 4, 5, 2).reshape(DEPTH, n_phys, 2 * B_W, PAGE_SIZE)
    cct = cache_c_kv.transpose(0, 1, 3, 4, 2).reshape(DEPTH, n_phys, 4 * HEAD_DIM, PAGE_SIZE)
    stt = state_c_win.transpose(0, 1, 3, 4, 2).reshape(DEPTH, n_s, 2 * HEAD_DIM, WINDOW)

    hp = x_prompt.reshape(n_p * s_p, D_MODEL)
    hs = x_sample.reshape(n_s * s_s, D_MODEL)
    ent_s = []
    stacked = tuple(jnp.zeros(shape, F32) for shape in (
        (DEPTH, n_p, A_W, s_p), (DEPTH, n_p * s_p * HEADS_A, LANES), (DEPTH, n_p, 2 * B_W, s_p),
        (DEPTH, n_p, 4 * HEAD_DIM, s_p), (DEPTH, n_p, 2 * HEAD_DIM, s_p)))
    wg = w_ffn_gate.astype(BF16)
    wu = w_ffn_up.astype(BF16)
    wd = w_ffn_down.astype(BF16)
    wo = w_out.astype(BF16)
    for l in range(DEPTH):
        w_main = w_in[l][:, :MAIN_W].astype(BF16)
        w_t = jnp.concatenate([w_in[l][:, _O_KA:_O_KA + A_W], w_in[l][:, _O_KB:_O_KB + 2 * B_W],
                               w_in[l][:, _O_KVC:_O_KVC + KVC_W]], axis=1).T.astype(BF16)
        w_gate = jnp.pad(w_in[l][:, MAIN_W:], ((0, 0), (0, LANES - GATE_W))).astype(BF16)
        cw = _compress_weights(w_cmp1[l], w_cmp2[l], cmp_pos[l])
        last = l == DEPTH - 1

        hp = _ffn(hp, g_ffn[l, 0], wg, wu, wd, l, 0)
        outs = _proj_prompt(hp, g_mix[l], w_main, w_t, w_gate, tabs_p, n_p, s_p, l, stacked)
        stacked = tuple(outs[:5])
        qa16, kat16, va16, qb16, qb32, bkvt16, qcz16, selt16, wint16, gates = outs[5:]
        oa = _attn_a_prompt(qa16, kat16, va16, diff_lambda[l], g_diff_head[l], n_p, s_p, l)
        ob = _attn_b_prompt(qb16, qb32, stacked[2], bkvt16, n_p, s_p, l)
        kvcmp = _compress_prompt(stacked[3], cw, n_p, s_p, l)
        oc = _attn_c_prompt(qcz16, gates, selt16, wint16, kvcmp, n_p, s_p)
        hp = _ffn(hp, g_ffn[l, 1], wg, wu, wd, l, 1, (oa, ob, oc, wo), g_final if last else None)

        hs = _ffn(hs, g_ffn[l, 0], wg, wu, wd, l, 0)
        (ka, va, bkv, ckv, win, qa16, qb16, qb32, qcz16, gates) = _proj_sample(
            hs, g_mix[l], w_main, w_gate, cos_s, sin_s)
        oa = _dec_a(qa16, ka, va, diff_lambda[l], g_diff_head[l], ckt, cv4, page_table, l, s_s)
        ob = _dec_b(qb16, qb32, bkv, cbt, page_table, l, s_s)
        oc = _dec_c(qcz16, gates, ckv, win, cw, stt, cct, page_table, l, s_s, past_len)
        hs = _ffn(hs, g_ffn[l, 1], wg, wu, wd, l, 1, (oa, ob, oc, wo), g_final if last else None)
        ent_s.append((ka.reshape(n_s, s_s, HEADS_A, 2, HEAD_DIM), va.reshape(n_s, s_s, HEADS_A, 2 * HEAD_DIM),
                      bkv.reshape(n_s, s_s, 2, HEADS_B, HEAD_DIM), ckv.reshape(n_s, s_s, 4, HEAD_DIM), win))

    st_ = lambda i: jnp.stack([e[i] for e in ent_s], axis=0)
    kat, va, bkvt, ckvt, wint = stacked
    win_keep = min(WINDOW, s_p)
    assert state_c_win.shape[2] == WINDOW
    win_s = _window_out(stt, st_(4), s_s)
    return (hp.reshape(n_p, s_p, D_MODEL), hs.reshape(n_s, s_s, D_MODEL),
            kat.reshape(DEPTH, n_p, HEADS_A, 2, HEAD_DIM, s_p).transpose(0, 1, 5, 2, 3, 4), st_(0),
            va.reshape(DEPTH, n_p, s_p, HEADS_A, 2 * HEAD_DIM), st_(1),
            bkvt.reshape(DEPTH, n_p, 2, HEADS_B, HEAD_DIM, s_p).transpose(0, 1, 5, 2, 3, 4), st_(2),
            ckvt.reshape(DEPTH, n_p, 4, HEAD_DIM, s_p).transpose(0, 1, 4, 2, 3), st_(3),
            wint[:, :, :, s_p - win_keep:].reshape(DEPTH, n_p, 2, HEAD_DIM, win_keep).transpose(0, 1, 4, 2, 3),
            win_s.reshape(DEPTH, n_s, 2, HEAD_DIM, WINDOW).transpose(0, 1, 4, 2, 3))
```

```python
import functools
import math

import numpy as np
import jax
import jax.numpy as jnp
from jax import lax
from jax.experimental import pallas as pl
from jax.experimental.pallas import tpu as pltpu

F32 = jnp.float32
BF16 = jnp.bfloat16

D_MODEL = 1024
DEPTH = 2
HEAD_DIM = 64
HEADS_A = 4
HEADS_B = 4
HEADS_C = 4
D_FF = 2816
ROPE_THETA = 10000.0
MOBA_BLOCK = 256
MOBA_TOPK = 3
CMP_LEN = 32
CMP_STRIDE = 16
CMP_HIDDEN = 4 * HEAD_DIM
SEL_BLOCK = 64
SEL_TOPK = 16
WINDOW = 512
RMS_EPS = 1e-6
PAGE_SIZE = 128

A_W = HEADS_A * 2 * HEAD_DIM
B_W = HEADS_B * HEAD_DIM
C_W = HEADS_C * HEAD_DIM
KVC_W = 6 * HEAD_DIM
GATE_W = 3 * HEADS_C
MAIN_W = 3 * A_W + 3 * B_W + C_W + KVC_W
LANES = 128
QK_SCALE = HEAD_DIM ** -0.5
NEG_INF = float("-inf")
VMEM_LIMIT = 56 * 1024 * 1024

TM_FFN = 512
TF_FFN = 256
TM_PROJ = 512
TQ = 256
NSA_DEC_GROUP = 2
DEC_GROUP = 4
A_DEC_GROUP = 2
A_HEADS_PER_LOOP = 4

_O_QA, _O_KA, _O_VA = 0, A_W, 2 * A_W
_O_QB = 3 * A_W
_O_KB, _O_VB = _O_QB + B_W, _O_QB + 2 * B_W
_O_QC = _O_QB + 3 * B_W
_O_KVC = _O_QC + C_W


def _nn(a, b):
    return jnp.dot(a, b, preferred_element_type=F32)


def _nt(a, b):
    return lax.dot_general(a, b, (((1,), (1,)), ((), ())), preferred_element_type=F32)


def _split3(x):
    hi = x.astype(BF16)
    r1 = x - hi.astype(F32)
    mid = r1.astype(BF16)
    lo = (r1 - mid.astype(F32)).astype(BF16)
    return hi, mid, lo


def _nn_precise(a, b):
    a_hi, a_mid, _ = _split3(a)
    b_hi, b_mid, _ = _split3(b)
    return _nn(a_hi, b_hi) + (_nn(a_hi, b_mid) + _nn(a_mid, b_hi))


def _rms(x):
    return x * lax.rsqrt(jnp.mean(x * x, axis=-1, keepdims=True) + RMS_EPS)


def _iota(shape, dim):
    return lax.broadcasted_iota(jnp.int32, shape, dim)


def _params(n_axes):
    return pltpu.CompilerParams(dimension_semantics=("arbitrary",) * n_axes,
                                vmem_limit_bytes=VMEM_LIMIT)


def _lam_value(lam_ref, lam_init):
    lp = lam_ref[...]
    a = jnp.sum(lp[0:1] * lp[1:2], axis=1, keepdims=True)
    b = jnp.sum(lp[2:3] * lp[3:4], axis=1, keepdims=True)
    return jnp.exp(a) - jnp.exp(b) + lam_init


def _lam_init(lidx):
    return 0.8 - 0.6 * math.exp(-0.3 * lidx)


def _ffn_body(*refs, mixed, final):
    refs = list(refs)
    x_ref = refs.pop(0)
    x = x_ref[...]
    if mixed:
        oa_ref, ob_ref, oc_ref, wo_ref = refs[:4]
        refs = refs[4:]
        x = x + _nn(oa_ref[...], wo_ref[0:A_W, :])
        x = x + _nn(ob_ref[...], wo_ref[A_W:A_W + B_W, :])
        x = x + _nn(oc_ref[...], wo_ref[A_W + B_W:, :])
    g_ref, wg_ref, wu_ref, wd_ref = refs[:4]
    o_ref = refs[-1]
    xn = (_rms(x) * g_ref[...]).astype(BF16)
    acc = jnp.zeros_like(x)
    for f in range(D_FF // TF_FFN):
        sl = slice(f * TF_FFN, (f + 1) * TF_FFN)
        g = _nn(xn, wg_ref[:, sl])
        u = _nn(xn, wu_ref[:, sl])
        a = (g * jax.nn.sigmoid(g) * u).astype(BF16)
        acc = acc + _nn(a, wd_ref[sl, :])
    y = x + 0.5 * acc
    if final:
        y = _rms(y) * refs[4][...]
    o_ref[...] = y


def _ffn(x, g, wg, wu, wd, l, k, mix=None, g_final=None):
    m = x.shape[0]
    row = lambda w: pl.BlockSpec((TM_FFN, w), lambda i: (i, 0))
    vec = pl.BlockSpec((1, D_MODEL), lambda i: (0, 0))
    full = lambda r, c: pl.BlockSpec((None, None, r, c), lambda i: (l, k, 0, 0))
    in_specs, args = [row(D_MODEL)], [x]
    if mix is not None:
        oa, ob, oc, w_out = mix
        in_specs += [row(A_W), row(B_W), row(C_W), pl.BlockSpec((None, D_MODEL, D_MODEL), lambda i: (l, 0, 0))]
        args += [oa, ob, oc, w_out]
    in_specs += [vec, full(D_MODEL, D_FF), full(D_MODEL, D_FF), full(D_FF, D_MODEL)]
    args += [g.reshape(1, D_MODEL), wg, wu, wd]
    if g_final is not None:
        in_specs.append(vec)
        args.append(g_final.reshape(1, D_MODEL))
    return pl.pallas_call(
        functools.partial(_ffn_body, mixed=mix is not None, final=g_final is not None),
        out_shape=jax.ShapeDtypeStruct((m, D_MODEL), F32),
        grid=(m // TM_FFN,),
        in_specs=in_specs,
        out_specs=row(D_MODEL),
        compiler_params=_params(1),
        name="ffn_half",
    )(*args)


def _rope_rows(lane_shape):
    lane = _iota(lane_shape, 1)
    return (lane % HEAD_DIM) < (HEAD_DIM // 2), lane < HEAD_DIM


def _rope_lanes(x, cos, sin, lo32):
    sh = jnp.where(lo32, pltpu.roll(x, LANES - HEAD_DIM // 2, 1), pltpu.roll(x, HEAD_DIM // 2, 1))
    return x * cos + sh * sin


def _proj_queries(u, w_ref, cos, sin, lo32, lo64, qa16_ref, qb16_ref, qb32_ref, qcz16_ref):
    p = _nn(u, w_ref[:, _O_QA:_O_QA + A_W])
    for k in range(A_W // LANES):
        qa16_ref[:, k * LANES:(k + 1) * LANES] = (
            _rope_lanes(p[:, k * LANES:(k + 1) * LANES], cos, sin, lo32) * QK_SCALE).astype(BF16)
    p = _nn(u, w_ref[:, _O_QB:_O_QB + B_W])
    for k in range(B_W // LANES):
        r = _rope_lanes(p[:, k * LANES:(k + 1) * LANES], cos, sin, lo32) * QK_SCALE
        qb32_ref[:, k * LANES:(k + 1) * LANES] = r
        qb16_ref[:, k * LANES:(k + 1) * LANES] = r.astype(BF16)
    p = _nn(u, w_ref[:, _O_QC:_O_QC + C_W])
    for k in range(C_W // LANES):
        r = _rope_lanes(p[:, k * LANES:(k + 1) * LANES], cos, sin, lo32) * QK_SCALE
        even = jnp.where(lo64, r, 0.0)
        odd = jnp.where(lo64, pltpu.roll(r, HEAD_DIM, 1), 0.0)
        qcz16_ref[:, (2 * k) * LANES:(2 * k + 1) * LANES] = even.astype(BF16)
        qcz16_ref[:, (2 * k + 1) * LANES:(2 * k + 2) * LANES] = odd.astype(BF16)


def _proj_sample_body(h_ref, g_ref, w_ref, wgate_ref, cos_ref, sin_ref,
                      ka_ref, va_ref, bkv_ref, ckv_ref, win_ref,
                      qa16_ref, qb16_ref, qb32_ref, qcz16_ref, gate_ref):
    tm = h_ref.shape[0]
    u = (_rms(h_ref[...]) * g_ref[...]).astype(BF16)
    cos = cos_ref[...]
    sin = sin_ref[...]
    lo32, lo64 = _rope_rows((tm, LANES))
    _proj_queries(u, w_ref, cos, sin, lo32, lo64, qa16_ref, qb16_ref, qb32_ref, qcz16_ref)
    p = _nn(u, w_ref[:, _O_KA:_O_KA + A_W])
    for k in range(A_W // LANES):
        ka_ref[:, k * LANES:(k + 1) * LANES] = _rope_lanes(p[:, k * LANES:(k + 1) * LANES], cos, sin, lo32)
    va_ref[...] = _nn(u, w_ref[:, _O_VA:_O_VA + A_W])
    p = _nn(u, w_ref[:, _O_KB:_O_KB + B_W])
    for k in range(B_W // LANES):
        bkv_ref[:, k * LANES:(k + 1) * LANES] = _rope_lanes(p[:, k * LANES:(k + 1) * LANES], cos, sin, lo32)
    bkv_ref[:, B_W:2 * B_W] = _nn(u, w_ref[:, _O_VB:_O_VB + B_W])
    p = _nn(u, w_ref[:, _O_KVC:_O_KVC + KVC_W])
    for k in range(KVC_W // LANES):
        x = p[:, k * LANES:(k + 1) * LANES]
        r = jnp.where(lo64, _rope_lanes(x, cos, sin, lo32), x)
        if k < 2:
            ckv_ref[:, k * LANES:(k + 1) * LANES] = r
        else:
            win_ref[...] = r
    gate_ref[...] = jax.nn.sigmoid(_nn(u, wgate_ref[...]))


def _proj_sample(h, g_mix, w_main, w_gate, cos, sin):
    m = h.shape[0]
    tm = TM_PROJ
    row = lambda w: pl.BlockSpec((tm, w), lambda i: (i, 0))
    full = lambda shape: pl.BlockSpec(shape, lambda i: (0, 0))
    outs = ((A_W, F32), (A_W, F32), (2 * B_W, F32), (4 * HEAD_DIM, F32), (2 * HEAD_DIM, F32),
            (A_W, BF16), (B_W, BF16), (B_W, F32), (2 * C_W, BF16), (LANES, F32))
    return pl.pallas_call(
        _proj_sample_body,
        out_shape=[jax.ShapeDtypeStruct((m, w), dt) for w, dt in outs],
        grid=(m // tm,),
        in_specs=[row(D_MODEL), full((1, D_MODEL)), full((D_MODEL, MAIN_W)), full((D_MODEL, LANES)),
                  full((tm, LANES)), full((tm, LANES))],
        out_specs=[row(w) for w, _ in outs],
        compiler_params=_params(1),
        name="in_proj_rope_sample",
    )(h, g_mix.reshape(1, D_MODEL), w_main, w_gate, cos, sin)


_KT_ROWS = A_W + 2 * B_W + KVC_W


def _proj_prompt_body(h_ref, g_ref, w_ref, wt_ref, wgate_ref, cos_ref, sin_ref, cos_t_ref, sin_t_ref,
                      _kat_in, _va_in, _bkvt_in, _ckvt_in, _wint_in,
                      kat_ref, va_ref, bkvt_ref, ckvt_ref, wint_ref,
                      qa16_ref, kat16_ref, va16_ref, qb16_ref, qb32_ref, bkvt16_ref,
                      qcz16_ref, selt16_ref, wint16_ref, gate_ref):
    tm = h_ref.shape[0]
    half = HEAD_DIM // 2
    u = (_rms(h_ref[...]) * g_ref[...]).astype(BF16)
    lo32, lo64 = _rope_rows((tm, LANES))
    _proj_queries(u, w_ref, cos_ref[...], sin_ref[...], lo32, lo64, qa16_ref, qb16_ref, qb32_ref, qcz16_ref)

    p = _nn(u, w_ref[:, _O_VA:_O_VA + A_W])
    va16_ref[...] = p.astype(BF16)
    for h in range(HEADS_A):
        va_ref[pl.ds(h, tm, stride=HEADS_A), :] = p[:, h * LANES:(h + 1) * LANES]

    cos_t = cos_t_ref[...]
    sin_t = sin_t_ref[...]

    def rope_t(x):
        x1, x2 = x[0:half], x[half:HEAD_DIM]
        return jnp.concatenate([x1 * cos_t - x2 * sin_t, x2 * cos_t + x1 * sin_t], axis=0)

    def store_t(f32_ref, b16_ref, row0, val):
        rows = val.shape[0]
        if f32_ref is not None:
            f32_ref[row0:row0 + rows, :] = val
        if b16_ref is not None:
            for t in range(tm // TQ):
                b16_ref[t, row0:row0 + rows, :] = val[:, t * TQ:(t + 1) * TQ].astype(BF16)

    pt = _nt(wt_ref[0:A_W, :], u)
    for g in range(A_W // HEAD_DIM):
        store_t(kat_ref, kat16_ref, g * HEAD_DIM, rope_t(pt[g * HEAD_DIM:(g + 1) * HEAD_DIM]))
    pt = _nt(wt_ref[A_W:A_W + 2 * B_W, :], u)
    for g in range(B_W // HEAD_DIM):
        store_t(bkvt_ref, bkvt16_ref, g * HEAD_DIM, rope_t(pt[g * HEAD_DIM:(g + 1) * HEAD_DIM]))
    store_t(bkvt_ref, bkvt16_ref, B_W, pt[B_W:2 * B_W])
    pt = _nt(wt_ref[A_W + 2 * B_W:_KT_ROWS, :], u)
    for g in range(KVC_W // HEAD_DIM):
        x = pt[g * HEAD_DIM:(g + 1) * HEAD_DIM]
        if g % 2 == 0:
            x = rope_t(x)
        if g < 2:
            store_t(ckvt_ref, None, g * HEAD_DIM, x)
        elif g < 4:
            store_t(ckvt_ref, None, g * HEAD_DIM, x)
            store_t(None, selt16_ref, (g - 2) * HEAD_DIM, x)
        else:
            store_t(wint_ref, wint16_ref, (g - 4) * HEAD_DIM, x)
    gate_ref[...] = jax.nn.sigmoid(_nn(u, wgate_ref[...]))


def _proj_prompt(h, g_mix, w_main, w_t, w_gate, tabs, n, s, l, stacked):
    cos, sin, cos_t, sin_t = tabs
    m = n * s
    tm = TM_PROJ
    per_seq = s // tm
    nt = tm // TQ
    row = lambda w: pl.BlockSpec((tm, w), lambda i: (i, 0))
    full = lambda shape: pl.BlockSpec(shape, lambda i: (0,) * len(shape))
    tab = pl.BlockSpec((tm, LANES), lambda i: (i % per_seq, 0))
    tab_t = pl.BlockSpec((HEAD_DIM // 2, tm), lambda i: (0, i % per_seq))
    feat = lambda w: pl.BlockSpec((None, None, w, tm), lambda i: (l, i // per_seq, 0, i % per_seq))
    tiles = lambda w: pl.BlockSpec((None, nt, w, TQ), lambda i: (i // per_seq, i % per_seq, 0, 0))
    sds = jax.ShapeDtypeStruct
    out_shape = [sds(a.shape, a.dtype) for a in stacked]
    out_shape += [sds((m, A_W), BF16), sds((n, s // TQ, A_W, TQ), BF16), sds((m, A_W), BF16),
                  sds((m, B_W), BF16), sds((m, B_W), F32), sds((n, s // TQ, 2 * B_W, TQ), BF16),
                  sds((m, 2 * C_W), BF16), sds((n, s // TQ, 2 * HEAD_DIM, TQ), BF16),
                  sds((n, s // TQ, 2 * HEAD_DIM, TQ), BF16), sds((m, LANES), F32)]
    out_specs = [feat(A_W), pl.BlockSpec((None, tm * HEADS_A, LANES), lambda i: (l, i, 0)), feat(2 * B_W),
                 feat(4 * HEAD_DIM), feat(2 * HEAD_DIM),
                 row(A_W), tiles(A_W), row(A_W), row(B_W), row(B_W), tiles(2 * B_W),
                 row(2 * C_W), tiles(2 * HEAD_DIM), tiles(2 * HEAD_DIM), row(LANES)]
    in_specs = [row(D_MODEL), full((1, D_MODEL)), full((D_MODEL, MAIN_W)), full((_KT_ROWS, D_MODEL)),
                full((D_MODEL, LANES)), tab, tab, tab_t, tab_t]
    n_in = len(in_specs)
    in_specs += [pl.BlockSpec(memory_space=pl.ANY)] * len(stacked)
    return pl.pallas_call(
        _proj_prompt_body,
        out_shape=out_shape,
        grid=(m // tm,),
        in_specs=in_specs,
        out_specs=out_specs,
        input_output_aliases={n_in + k: k for k in range(len(stacked))},
        compiler_params=_params(1),
        name="in_proj_rope_prompt",
    )(h, g_mix.reshape(1, D_MODEL), w_main, w_t, w_gate, cos, sin, cos_t, sin_t, *stacked)


def _rank_lower(x, n, width_iota):
    rank = jnp.zeros(x.shape, F32)
    for bp in range(n):
        col = x[:, bp:bp + 1]
        tie = jnp.where(bp < width_iota, 1.0, 0.0)
        rank = rank + jnp.where(col > x, 1.0, jnp.where(col == x, tie, 0.0))
    return rank


def _rank_lower_t(x, n, row_iota):
    rank = jnp.zeros(x.shape, F32)
    for bp in range(n):
        row = x[bp:bp + 1, :]
        tie = jnp.where(bp < row_iota, 1.0, 0.0)
        rank = rank + jnp.where(row > x, 1.0, jnp.where(row == x, tie, 0.0))
    return rank


def _head_rms_scale(o, gh, lam_init):
    return _rms(o) * gh * (1.0 - lam_init)


def _attn_a_prompt_body(q_ref, kt_ref, v_ref, lam_ref, gh_ref, o_ref, *, lam_init):
    i = pl.program_id(1)
    tq = q_ref.shape[0]
    lam = _lam_value(lam_ref, lam_init)
    gh = gh_ref[...]
    lane = _iota((tq, LANES), 1)
    causal = _iota((tq, tq), 1) <= _iota((tq, tq), 0)
    head_sl = [slice(h * LANES, (h + 1) * LANES) for h in range(HEADS_A)]
    chains = [(h, c) for h in range(HEADS_A) for c in range(2)]
    qms = []
    for h, c in chains:
        q2 = q_ref[:, head_sl[h]]
        in_c = (lane >= c * HEAD_DIM) & (lane < (c + 1) * HEAD_DIM)
        qms.append(jnp.where(in_c, q2, jnp.zeros_like(q2)))

    def rows(j):
        return pl.ds(pl.multiple_of(j * tq, tq), tq)

    ones = jnp.ones((tq, LANES), BF16)

    def v_ext(j, h):
        return jnp.concatenate([v_ref[rows(j), head_sl[h]], ones], axis=1)

    for h0 in range(0, HEADS_A, A_HEADS_PER_LOOP):
        group = [(2 * h + c, h) for h in range(h0, h0 + A_HEADS_PER_LOOP) for c in range(2)]
        state = []
        for idx, h in group:
            s = jnp.where(causal, _nn(qms[idx], kt_ref[i, head_sl[h], :]), NEG_INF)
            m = jnp.max(s, axis=-1, keepdims=True)
            p = jnp.exp((s - m).astype(BF16))
            state += [m, _nn(p, v_ext(i, h))]

        def body(j, carry, group=group):
            out = []
            for k, (idx, h) in enumerate(group):
                m, acc = carry[2 * k:2 * k + 2]
                s = _nn(qms[idx], kt_ref[j, head_sl[h], :])
                m_new = jnp.maximum(m, jnp.max(s, axis=-1, keepdims=True))
                p = jnp.exp((s - m_new).astype(BF16))
                out += [m_new, jnp.exp(m - m_new) * acc + _nn(p, v_ext(j, h))]
            return tuple(out)

        state = lax.fori_loop(0, i, body, tuple(state))
        for k in range(A_HEADS_PER_LOOP):
            a0, a1 = state[4 * k + 1], state[4 * k + 3]
            o0 = a0[:, :LANES] / a0[:, LANES:]
            o1 = a1[:, :LANES] / a1[:, LANES:]
            o_ref[:, head_sl[h0 + k]] = _head_rms_scale(o0 - lam * o1, gh, lam_init).astype(BF16)


def _attn_a_prompt(q16, kt16, v16, lam_p, g_head, n, s, lidx):
    nq = s // TQ
    qspec = pl.BlockSpec((TQ, A_W), lambda b, i: (b * nq + i, 0))
    return pl.pallas_call(
        functools.partial(_attn_a_prompt_body, lam_init=_lam_init(lidx)),
        out_shape=jax.ShapeDtypeStruct((n * s, A_W), BF16),
        grid=(n, nq),
        in_specs=[qspec,
                  pl.BlockSpec((None, nq, A_W, TQ), lambda b, i: (b, 0, 0, 0)),
                  pl.BlockSpec((s, A_W), lambda b, i: (b, 0)),
                  pl.BlockSpec((4, HEAD_DIM), lambda b, i: (0, 0)),
                  pl.BlockSpec((1, 2 * HEAD_DIM), lambda b, i: (0, 0))],
        out_specs=qspec,
        compiler_params=_params(2),
        name="diff_attn_prompt",
    )(q16, kt16, v16, lam_p, g_head.reshape(1, 2 * HEAD_DIM))


def _block_means_t(blocks):
    feats = blocks[0].shape[0]
    lane = _iota((feats, LANES), 1)
    out = jnp.zeros((feats, LANES), F32)
    for b, blk in enumerate(blocks):
        out = jnp.where(lane == b, jnp.sum(blk, axis=1, keepdims=True) * (1.0 / MOBA_BLOCK), out)
    return out


def _attn_b_prompt_body(q_ref, q32_ref, k32t_ref, kvt_ref, o_ref, kmean_ref):
    i = pl.program_id(1)
    tq = q_ref.shape[0]
    nb = k32t_ref.shape[1] // MOBA_BLOCK

    @pl.when(i == 0)
    def _():
        kmean_ref[...] = _block_means_t([k32t_ref[:, b * MOBA_BLOCK:(b + 1) * MOBA_BLOCK] for b in range(nb)])

    lane = _iota((tq, LANES), 1)
    causal = _iota((tq, tq), 1) <= _iota((tq, tq), 0)
    nb_rows = -(-nb // 8) * 8
    blk_t = _iota((nb_rows, tq), 0)
    past_t = blk_t < i
    ksl = [slice((h // 2) * LANES, (h // 2 + 1) * LANES) for h in range(HEADS_B)]
    vsl = [slice(B_W + (h // 2) * LANES, B_W + (h // 2 + 1) * LANES) for h in range(HEADS_B)]
    ones_t = jnp.ones((HEAD_DIM, tq), BF16)

    def v_ones(j, h):
        vt = kvt_ref[j, vsl[h], :]
        if h % 2 == 0:
            return jnp.concatenate([vt[0:HEAD_DIM], ones_t], axis=0)
        return jnp.concatenate([ones_t, vt[HEAD_DIM:2 * HEAD_DIM]], axis=0)

    qms, chosen, state = [], [], []
    for h in range(HEADS_B):
        q2 = q_ref[:, ksl[h]]
        in_h = (lane >= (h % 2) * HEAD_DIM) & (lane < (h % 2 + 1) * HEAD_DIM)
        qms.append(jnp.where(in_h, q2, jnp.zeros_like(q2)))
        gate = _nn_precise(jnp.where(in_h, q32_ref[:, ksl[h]], 0.0), kmean_ref[ksl[h], :])
        gate_t = jnp.where(past_t, gate.T[0:nb_rows], NEG_INF)
        pick_t = jnp.where((_rank_lower_t(gate_t, nb, blk_t) < MOBA_TOPK) & past_t, 1.0, 0.0)
        chosen.append(jnp.concatenate([pick_t, jnp.zeros((LANES - nb_rows, tq), F32)], axis=0).T)
        s = jnp.where(causal, _nn(qms[h], kvt_ref[i, ksl[h], :]), NEG_INF)
        m = jnp.max(s, axis=-1, keepdims=True)
        state += [m, _nt(jnp.exp((s - m).astype(BF16)), v_ones(i, h))]

    def body(j, carry):
        out = []
        for h in range(HEADS_B):
            m, acc = carry[2 * h:2 * h + 2]
            use = jnp.max(jnp.where(lane == j, chosen[h], 0.0), axis=1, keepdims=True)
            s = jnp.where(use > 0.5, _nn(qms[h], kvt_ref[j, ksl[h], :]), NEG_INF)
            m_new = jnp.maximum(m, jnp.max(s, axis=-1, keepdims=True))
            p = jnp.exp((s - m_new).astype(BF16))
            out += [m_new, jnp.exp(m - m_new) * acc + _nt(p, v_ones(j, h))]
        return tuple(out)

    state = lax.fori_loop(0, i, body, tuple(state))
    for pair in range(HEADS_B // 2):
        a0, a1 = state[4 * pair + 1], state[4 * pair + 3]
        o0 = a0 / pltpu.roll(a0, HEAD_DIM, 1)
        o1 = a1 / pltpu.roll(a1, HEAD_DIM, 1)
        o_ref[:, ksl[2 * pair]] = jnp.where(lane < HEAD_DIM, o0, o1).astype(BF16)


def _attn_b_prompt(q16, q32, bkvt32, bkvt16, n, s, l):
    nq = s // TQ
    assert TQ == MOBA_BLOCK and s // MOBA_BLOCK <= LANES
    qspec = pl.BlockSpec((TQ, B_W), lambda b, i: (b * nq + i, 0))
    return pl.pallas_call(
        _attn_b_prompt_body,
        out_shape=jax.ShapeDtypeStruct((n * s, B_W), BF16),
        grid=(n, nq),
        in_specs=[qspec, qspec,
                  pl.BlockSpec((None, None, B_W, s), lambda b, i: (l, b, 0, 0)),
                  pl.BlockSpec((None, nq, 2 * B_W, TQ), lambda b, i: (b, 0, 0, 0))],
        out_specs=qspec,
        scratch_shapes=[pltpu.VMEM((B_W, LANES), F32)],
        compiler_params=_params(2),
        name="moba_attn_prompt",
    )(q16, q32, bkvt32, bkvt16)


_CMP_ROWS = 128


def _compress_core(xs_ref, pe_ref, w1_ref, w2_ref):
    half = CMP_LEN // 2
    n_chunk = xs_ref.shape[0] // CMP_STRIDE
    acc = [None, None]
    def rows(r):
        return (xs_ref[pl.ds(r % half, n_chunk, stride=CMP_STRIDE), :] + pe_ref[r:r + 1, :]).astype(BF16)

    for r in range(0, CMP_LEN, 2):
        t = _nn(jnp.concatenate([rows(r), rows(r + 1)], axis=1), w1_ref[r // 2])
        acc[r // half] = t if acc[r // half] is None else acc[r // half] + t
    pre = acc[0] + pltpu.roll(acc[1], n_chunk - 1, 0)
    hid = jax.nn.gelu(pre).astype(BF16)
    return _nn(hid, w2_ref[...]).astype(BF16)


def _compress_prompt_body(xt_ref, pe_ref, w1_ref, w2_ref, o_ref, xs_ref):
    for j in range(xt_ref.shape[1] // LANES):
        xs_ref[j * LANES:(j + 1) * LANES, :] = xt_ref[:, j * LANES:(j + 1) * LANES].T
    o_ref[...] = _compress_core(xs_ref, pe_ref, w1_ref, w2_ref)


def _compress_weights(w_cmp1, w_cmp2, cmp_pos):
    w1 = w_cmp1.reshape(2, CMP_LEN, HEAD_DIM, CMP_HIDDEN)
    z1 = jnp.zeros((CMP_LEN, HEAD_DIM, CMP_HIDDEN), F32)
    top = jnp.concatenate([w1[0], z1], axis=2)
    bot = jnp.concatenate([z1, w1[1]], axis=2)
    w1c = jnp.concatenate([top, bot], axis=1).astype(BF16)
    w1c = w1c.reshape(CMP_LEN // 2, 2 * LANES, 2 * CMP_HIDDEN)
    z2 = jnp.zeros((CMP_HIDDEN, HEAD_DIM), F32)
    w2c = jnp.concatenate([jnp.concatenate([w_cmp2[0], z2], axis=1),
                           jnp.concatenate([z2, w_cmp2[1]], axis=1)], axis=0).astype(BF16)
    pe = jnp.concatenate([cmp_pos[0], cmp_pos[1]], axis=1)
    return pe, w1c, w2c


def _const_specs(index):
    return [pl.BlockSpec((CMP_LEN, LANES), index(2)),
            pl.BlockSpec((CMP_LEN // 2, 2 * LANES, 2 * CMP_HIDDEN), index(3)),
            pl.BlockSpec((2 * CMP_HIDDEN, LANES), index(2))]


def _compress_prompt(ckvt32, cw, n, s, l):
    assert s // CMP_STRIDE == _CMP_ROWS
    zero = lambda nd: (lambda b: (0,) * nd)
    return pl.pallas_call(
        _compress_prompt_body,
        out_shape=jax.ShapeDtypeStruct((n * _CMP_ROWS, LANES), BF16),
        grid=(n,),
        in_specs=[pl.BlockSpec((None, None, LANES, s), lambda b: (l, b, 0, 0))] + _const_specs(zero),
        out_specs=pl.BlockSpec((_CMP_ROWS, LANES), lambda b: (b, 0)),
        scratch_shapes=[pltpu.VMEM((s, LANES), F32)],
        compiler_params=_params(1),
        name="nsa_compress_prompt",
    )(ckvt32, *cw)


def _page_index(l, g, group, j, row_block, b, pt):
    return (l, pt[b * group + g, j], row_block, 0)


def _page_specs(l, n_pages, rows, row_block=0, group=1):
    return [pl.BlockSpec((None, None, rows, PAGE_SIZE), functools.partial(_page_index, l, g, group, j, row_block))
            for g in range(group) for j in range(n_pages)]


def _overlap_matrix(t_len):
    n_cmp = (t_len - CMP_LEN) // CMP_STRIDE + 1
    nsb = -(-t_len // SEL_BLOCK)
    starts = np.arange(n_cmp) * CMP_STRIDE
    sb = np.arange(nsb) * SEL_BLOCK
    ov = np.clip(np.minimum(starts[:, None] + CMP_LEN, sb[None, :] + SEL_BLOCK)
                 - np.maximum(starts[:, None], sb[None, :]), 0, None) / CMP_STRIDE
    out = np.zeros((_CMP_ROWS, LANES), np.float32)
    out[:n_cmp, :nsb] = ov
    return jnp.asarray(out, BF16), nsb


def _expand_matrix(s):
    nt = s // TQ
    e = np.zeros((nt, LANES, TQ), np.float32)
    for j in range(nt):
        for k in range(TQ):
            e[j, (j * TQ + k) // SEL_BLOCK, k] = 1.0
    return jnp.asarray(e, BF16)


def _nsa_flags(p_sum, ovl, own, lane, nsb):
    imp = None
    for part in _split3(p_sum):
        t = _nn(part, ovl)
        imp = t if imp is None else imp + t
    forced = (lane == 0) | (lane == own) | (lane == own - 1)
    imp = jnp.where(lane > own, NEG_INF, jnp.where(forced, jnp.inf, imp))
    rank = _rank_lower(imp, nsb, lane)
    return jnp.where((rank < SEL_TOPK) & (lane <= own), 1.0, 0.0)


def _nsa_flags_t(p_sum, ovl, pos0, nsb):
    tq = p_sum.shape[0]
    imp = None
    for part in _split3(p_sum):
        t = _nn(part, ovl)
        imp = t if imp is None else imp + t
    rows = -(-nsb // 8) * 8
    imp_t = imp.T[0:rows]
    blk = _iota((rows, tq), 0)
    own = (pos0 + _iota((rows, tq), 1)) // SEL_BLOCK
    forced = (blk == 0) | (blk == own) | (blk == own - 1)
    imp_t = jnp.where(blk > own, NEG_INF, jnp.where(forced, jnp.inf, imp_t))
    rank = _rank_lower_t(imp_t, nsb, blk)
    flag_t = jnp.where((rank < SEL_TOPK) & (blk <= own), 1.0, 0.0)
    flag_t = jnp.concatenate([flag_t, jnp.zeros((LANES - rows, tq), F32)], axis=0)
    return flag_t.T


def _masked_probs(s, mask):
    s = jnp.where(mask, s, NEG_INF)
    m = jnp.max(s, axis=-1, keepdims=True)
    m = jnp.where(m > NEG_INF, m, 0.0)
    e = jnp.exp(s - m)
    d = jnp.sum(e, axis=-1, keepdims=True)
    return e / jnp.where(d > 0, d, 1.0)


def _attn_c_prompt_body(qz_ref, gate_ref, selt_ref, wint_ref, kvc_ref, ovl_ref, exp_ref, o_ref, *, nsb):
    i = pl.program_id(1)
    tq = gate_ref.shape[0]
    nh = HEADS_C
    q4 = jnp.concatenate([qz_ref[:, h * LANES:(h + 1) * LANES] for h in range(nh)], axis=0)
    lane = _iota((tq, LANES), 1)
    pos = i * tq + _iota((tq, LANES), 0)
    rr = _iota((tq, tq), 0)
    cc = _iota((tq, tq), 1)
    causal = cc <= rr

    kvc = kvc_ref[...]
    cmp_ok = (CMP_STRIDE * lane + (CMP_LEN - 1)) <= pos
    s = _nt(q4, kvc).reshape(nh, tq, LANES)
    p = _masked_probs(s, cmp_ok[None])
    o_cmp = _nn(p.reshape(nh * tq, LANES).astype(BF16), kvc)
    p_sum = p[0] + p[1] + p[2] + p[3]

    flag = _nsa_flags_t(p_sum, ovl_ref[...], i * tq, nsb).astype(BF16)

    ones_t = jnp.ones((HEAD_DIM, tq), BF16)

    def pv_tile(kj):
        return jnp.concatenate([ones_t, kj[HEAD_DIM:2 * HEAD_DIM]], axis=0)

    kd = selt_ref[i]
    ok = (_nn(flag, exp_ref[i]) > 0.5) & causal
    s = jnp.where(ok[None], _nn(q4, kd).reshape(nh, tq, tq), NEG_INF)
    m = jnp.max(s, axis=-1, keepdims=True)
    p = jnp.exp((s - m).astype(BF16))
    acc = _nt(p.reshape(nh * tq, tq), pv_tile(kd)).reshape(nh, tq, LANES)

    def body(j, carry):
        m, acc = carry
        kj = selt_ref[j]
        ok = _nn(flag, exp_ref[j]) > 0.5
        s = jnp.where(ok[None], _nn(q4, kj).reshape(nh, tq, tq), NEG_INF)
        m_new = jnp.maximum(m, jnp.max(s, axis=-1, keepdims=True))
        p = jnp.exp((s - m_new).astype(BF16))
        pv = _nt(p.reshape(nh * tq, tq), pv_tile(kj)).reshape(nh, tq, LANES)
        return m_new, jnp.exp(m - m_new) * acc + pv

    m, acc = lax.fori_loop(0, i, body, (m, acc))
    o_sel = acc / pltpu.roll(acc, HEAD_DIM, 2)

    assert WINDOW == 2 * tq
    w2 = wint_ref[jnp.maximum(i - 2, 0)]
    w1 = wint_ref[jnp.maximum(i - 1, 0)]
    w0 = wint_ref[i]
    s2 = jnp.where(((cc >= rr) & (i >= 2))[None], _nn(q4, w2).reshape(nh, tq, tq), NEG_INF)
    s1 = jnp.where(i >= 1, _nn(q4, w1).reshape(nh, tq, tq), NEG_INF)
    s0 = jnp.where(causal[None], _nn(q4, w0).reshape(nh, tq, tq), NEG_INF)
    s_all = jnp.concatenate([s2, s1, s0], axis=-1)
    m = jnp.max(s_all, -1, keepdims=True)
    e = jnp.exp((s_all - m).astype(BF16)).reshape(nh * tq, 3 * tq)
    acc = (_nt(e[:, 0:tq], pv_tile(w2)) + _nt(e[:, tq:2 * tq], pv_tile(w1))
           + _nt(e[:, 2 * tq:3 * tq], pv_tile(w0)))
    o_win = (acc / pltpu.roll(acc, HEAD_DIM, 1)).reshape(nh, tq, LANES)

    o_cmp = o_cmp.reshape(nh, tq, LANES)
    g = gate_ref[...]
    heads = []
    for h in range(nh):
        heads.append(g[:, 3 * h:3 * h + 1] * o_cmp[h] + g[:, 3 * h + 1:3 * h + 2] * o_sel[h]
                     + g[:, 3 * h + 2:3 * h + 3] * o_win[h])
    for pair in range(nh // 2):
        both = jnp.where(lane < HEAD_DIM, pltpu.roll(heads[2 * pair], HEAD_DIM, 1), heads[2 * pair + 1])
        o_ref[:, pair * LANES:(pair + 1) * LANES] = both.astype(BF16)


def _attn_c_prompt(qz16, gates, selt16, wint16, kvcmp16, n, s):
    nq = s // TQ
    ovl, nsb = _overlap_matrix(s)
    expand = _expand_matrix(s)
    qrow = lambda w: pl.BlockSpec((TQ, w), lambda b, i: (b * nq + i, 0))
    tiles = pl.BlockSpec((None, nq, 2 * HEAD_DIM, TQ), lambda b, i: (b, 0, 0, 0))
    return pl.pallas_call(
        functools.partial(_attn_c_prompt_body, nsb=nsb),
        out_shape=jax.ShapeDtypeStruct((n * s, C_W), BF16),
        grid=(n, nq),
        in_specs=[qrow(2 * C_W), qrow(LANES), tiles, tiles,
                  pl.BlockSpec((_CMP_ROWS, LANES), lambda b, i: (b, 0)),
                  pl.BlockSpec((_CMP_ROWS, LANES), lambda b, i: (0, 0)),
                  pl.BlockSpec((nq, LANES, TQ), lambda b, i: (0, 0, 0))],
        out_specs=qrow(C_W),
        compiler_params=_params(2),
        name="nsa_attn_prompt",
    )(qz16, gates, selt16, wint16, kvcmp16, ovl, expand)


def _pad_page(x):
    rows, w = x.shape
    return jnp.concatenate([x, jnp.zeros((PAGE_SIZE - rows, w), x.dtype)], axis=0)


def _new_page_mask(n_rows, nq):
    r = _iota((n_rows, PAGE_SIZE), 0) % nq
    t = _iota((n_rows, PAGE_SIZE), 1)
    return t <= r


def _dec_a_body(pt_ref, q_ref, kn_ref, vn_ref, lam_ref, gh_ref, *refs, lam_init, group):
    n_pages = (len(refs) - 1) // (2 * group)
    o_ref = refs[-1]
    nq = q_ref.shape[0] // group
    lam = _lam_value(lam_ref, lam_init)
    gh = gh_ref[...]
    for g in range(group):
        r = slice(g * nq, (g + 1) * nq)
        kp = refs[g * n_pages:(g + 1) * n_pages]
        vp = refs[(group + g) * n_pages:(group + g + 1) * n_pages]
        _dec_a_one(q_ref[r, :], kn_ref[r, :], vn_ref[r, :], lam, gh, kp, vp, o_ref, r, lam_init)


def _dec_a_one(q, kn_rows, vn_rows, lam, gh, kp, vp, o_ref, out_rows, lam_init):
    n_pages = len(kp)
    nq = q.shape[0]
    rows = 2 * HEADS_A * nq
    per_head = 2 * nq
    qt = jnp.concatenate([q] * (2 * HEADS_A), axis=0)
    diag = (_iota((rows, A_W), 0) // nq) == (_iota((rows, A_W), 1) // HEAD_DIM)
    qbd = jnp.where(diag, qt, jnp.zeros_like(qt))
    kn = _pad_page(kn_rows).astype(BF16)
    vn = _pad_page(vn_rows).astype(BF16)
    s_new = jnp.where(_new_page_mask(rows, nq), _nt(qbd, kn), NEG_INF)
    scores = [_nn(qbd, kp[j][...].astype(BF16)) for j in range(n_pages)]
    m = jnp.max(s_new, axis=1, keepdims=True)
    for s in scores:
        m = jnp.maximum(m, jnp.max(s, axis=1, keepdims=True))
    e = jnp.exp(s_new - m)
    d = jnp.sum(e, axis=1, keepdims=True)
    e = e.astype(BF16)
    accs = [_nn(e[h * per_head:(h + 1) * per_head], vn[:, h * LANES:(h + 1) * LANES]) for h in range(HEADS_A)]
    for j in range(n_pages):
        e = jnp.exp(scores[j] - m)
        d = d + jnp.sum(e, axis=1, keepdims=True)
        e = e.astype(BF16)
        for h in range(HEADS_A):
            vh = vp[j][pl.ds(h, PAGE_SIZE, stride=HEADS_A), :].astype(BF16)
            accs[h] = accs[h] + _nn(e[h * per_head:(h + 1) * per_head], vh)
    for h in range(HEADS_A):
        on = accs[h] / d[h * per_head:(h + 1) * per_head]
        o = on[0:nq] - lam * on[nq:2 * nq]
        o_ref[out_rows, h * LANES:(h + 1) * LANES] = _head_rms_scale(o, gh, lam_init).astype(BF16)


def _dec_a(q16, kn32, vn32, lam_p, g_head, cache_kt, cache_v4, page_table, l, nq):
    n, n_pages = page_table.shape
    group = A_DEC_GROUP
    row = lambda w: pl.BlockSpec((group * nq, w), lambda b, pt: (b, 0))
    grid_spec = pltpu.PrefetchScalarGridSpec(
        num_scalar_prefetch=1, grid=(n // group,),
        in_specs=[row(A_W), row(A_W), row(A_W),
                  pl.BlockSpec((4, HEAD_DIM), lambda b, pt: (0, 0)),
                  pl.BlockSpec((1, 2 * HEAD_DIM), lambda b, pt: (0, 0))]
                 + _page_specs(l, n_pages, A_W, group=group) + _page_specs(l, n_pages, A_W, group=group),
        out_specs=row(A_W))
    return pl.pallas_call(
        functools.partial(_dec_a_body, lam_init=_lam_init(l), group=group),
        out_shape=jax.ShapeDtypeStruct((n * nq, A_W), BF16),
        grid_spec=grid_spec,
        compiler_params=_params(1),
        name="diff_attn_decode",
    )(page_table, q16, kn32, vn32, lam_p, g_head.reshape(1, 2 * HEAD_DIM),
      *([cache_kt] * (n_pages * group)), *([cache_v4] * (n_pages * group)))


def _dec_b_body(pt_ref, q_ref, q32_ref, kvn_ref, *refs, group):
    pages, o_ref = refs[:-1], refs[-1]
    n_pages = len(pages) // group
    nq = q_ref.shape[0] // group
    for g in range(group):
        r = slice(g * nq, (g + 1) * nq)
        o_ref[r, :] = _dec_b_one(q_ref[r, :], q32_ref[r, :], kvn_ref[r, :],
                                 pages[g * n_pages:(g + 1) * n_pages]).astype(BF16)


def _dec_b_one(q, q32, kvn_rows, pages):
    n_pages = len(pages)
    nq = q.shape[0]
    rows = HEADS_B * nq
    pages_per_blk = MOBA_BLOCK // PAGE_SIZE
    nb = n_pages // pages_per_blk
    diag = (_iota((rows, B_W), 0) // nq) == (_iota((rows, B_W), 1) // HEAD_DIM)
    qt = jnp.concatenate([q] * HEADS_B, axis=0)
    qbd = jnp.where(diag, qt, jnp.zeros_like(qt))
    qbd32 = jnp.where(diag, jnp.concatenate([q32] * HEADS_B, axis=0), 0.0)

    blocks = []
    for b in range(nb):
        blocks.append(jnp.concatenate([pages[j][0:B_W, :] for j in range(b * pages_per_blk, (b + 1) * pages_per_blk)],
                                      axis=1))
    kmean_t = _block_means_t(blocks)
    lane = _iota((rows, LANES), 1)
    gate = _nn_precise(qbd32, kmean_t)
    gate = jnp.where(lane < nb, gate, NEG_INF)
    chosen = jnp.where((_rank_lower(gate, nb, lane) < MOBA_TOPK) & (lane < nb), 1.0, 0.0)

    kvn = _pad_page(kvn_rows).astype(BF16)
    s_new = jnp.where(_new_page_mask(rows, nq), _nt(qbd, kvn[:, 0:B_W]), NEG_INF)
    scores = []
    for j in range(n_pages):
        b = j // pages_per_blk
        s = _nn(qbd, pages[j][0:B_W, :].astype(BF16))
        scores.append(jnp.where(chosen[:, b:b + 1] > 0.5, s, NEG_INF))
    m = jnp.max(s_new, axis=1, keepdims=True)
    for s in scores:
        m = jnp.maximum(m, jnp.max(s, axis=1, keepdims=True))
    e = jnp.exp(s_new - m)
    d = jnp.sum(e, axis=1, keepdims=True)
    acc = _nn(e.astype(BF16), kvn[:, B_W:2 * B_W])
    for j in range(n_pages):
        e = jnp.exp(scores[j] - m)
        d = d + jnp.sum(e, axis=1, keepdims=True)
        acc = acc + _nt(e.astype(BF16), pages[j][B_W:2 * B_W, :].astype(BF16))
    on = acc / d
    head_of_lane = _iota((nq, B_W), 1) // HEAD_DIM
    o = jnp.zeros((nq, B_W), F32)
    for h in range(HEADS_B):
        o = o + jnp.where(head_of_lane == h, on[h * nq:(h + 1) * nq], 0.0)
    return o


def _dec_b(q16, q32, kvn32, cache_kvt, page_table, l, nq):
    n, n_pages = page_table.shape
    group = DEC_GROUP
    row = lambda w: pl.BlockSpec((group * nq, w), lambda b, pt: (b, 0))
    grid_spec = pltpu.PrefetchScalarGridSpec(
        num_scalar_prefetch=1, grid=(n // group,),
        in_specs=[row(B_W), row(B_W), row(2 * B_W)] + _page_specs(l, n_pages, 2 * B_W, group=group),
        out_specs=row(B_W))
    return pl.pallas_call(
        functools.partial(_dec_b_body, group=group),
        out_shape=jax.ShapeDtypeStruct((n * nq, B_W), BF16),
        grid_spec=grid_spec,
        compiler_params=_params(1),
        name="moba_attn_decode",
    )(page_table, q16, q32, kvn32, *([cache_kvt] * (n_pages * group)))


def _dec_c_body(pt_ref, qz_ref, gate_ref, ckvn_ref, winn_ref, pe_ref, w1_ref, w2_ref, ovl_ref, exp_ref, st_ref,
                *refs, nsb, q0, group):
    pages, o_ref, xs_ref = refs[:-2], refs[-2], refs[-1]
    n_pages = len(pages) // group
    nq = gate_ref.shape[0] // group
    for j, pg in enumerate(pages):
        xs_ref[j * PAGE_SIZE:(j + 1) * PAGE_SIZE, :] = pg[0:LANES, :].T
    kvc_all = _compress_core(xs_ref, pe_ref, w1_ref, w2_ref)
    ovl = ovl_ref[...]
    expand = exp_ref[...]
    for g in range(group):
        r = slice(g * nq, (g + 1) * nq)
        tiles = [pg[LANES:2 * LANES, :] for pg in pages[g * n_pages:(g + 1) * n_pages]]
        _dec_c_one(qz_ref[r, :], gate_ref[r, :], ckvn_ref[r, :], winn_ref[r, :],
                   kvc_all[g * _CMP_ROWS:(g + 1) * _CMP_ROWS], ovl, expand, st_ref[g], tiles, o_ref, r, nsb, q0)


def _dec_c_one(qz, g, ckvn_rows, winn_rows, kvc, ovl, expand, st32, tiles, o_ref, out_rows, nsb, q0):
    n_pages = len(tiles)
    nq = g.shape[0]
    nh = HEADS_C
    rows = nh * nq
    q4 = jnp.concatenate([qz[:, h * LANES:(h + 1) * LANES] for h in range(nh)], axis=0)
    lane = _iota((nq, LANES), 1)
    pos = q0 + _iota((nq, LANES), 0)
    lane4 = _iota((rows, LANES), 1)
    qrow4 = _iota((rows, LANES), 0) % nq
    new_ok = _new_page_mask(rows, nq)

    cmp_ok = (CMP_STRIDE * lane4 + (CMP_LEN - 1)) <= (q0 + qrow4)
    p = _masked_probs(_nt(q4, kvc), cmp_ok)
    o_cmp = _nn(p.astype(BF16), kvc)
    p_sum = p[0:nq]
    for h in range(1, nh):
        p_sum = p_sum + p[h * nq:(h + 1) * nq]

    own = pos // SEL_BLOCK
    flag = _nsa_flags(p_sum, ovl, own, lane, nsb)
    flag4 = jnp.concatenate([flag] * nh, axis=0)
    blk_per_page = PAGE_SIZE // SEL_BLOCK
    assert blk_per_page == 2

    ckvn = _pad_page(ckvn_rows).astype(BF16)
    kn = ckvn[:, LANES:2 * LANES]
    own_blk = n_pages * blk_per_page
    s_new = jnp.where(new_ok & (flag4[:, own_blk:own_blk + 1] > 0.5), _nt(q4, kn), NEG_INF)
    kt_all = jnp.concatenate([t.astype(BF16) for t in tiles], axis=1)
    ok = _nn(flag4.astype(BF16), expand) > 0.5
    s_old = jnp.where(ok, _nn(q4, kt_all), NEG_INF)
    m = jnp.maximum(jnp.max(s_new, axis=1, keepdims=True), jnp.max(s_old, axis=1, keepdims=True))
    e_new = jnp.exp(s_new - m)
    e_old = jnp.exp(s_old - m)
    d = jnp.sum(e_new, axis=1, keepdims=True) + jnp.sum(e_old, axis=1, keepdims=True)
    acc = _nn(e_new.astype(BF16), kn) + _nt(e_old.astype(BF16), kt_all)
    o_sel = acc / d

    wn = _pad_page(winn_rows).astype(BF16)
    s_new = jnp.where(new_ok, _nt(q4, wn), NEG_INF)
    st = st32.astype(BF16)
    key = _iota((rows, WINDOW), 1)
    ok = key >= (_iota((rows, WINDOW), 0) % nq)
    s_old = jnp.where(ok, _nn(q4, st), NEG_INF)
    m = jnp.maximum(jnp.max(s_new, axis=1, keepdims=True), jnp.max(s_old, axis=1, keepdims=True))
    e_new = jnp.exp(s_new - m)
    e_old = jnp.exp(s_old - m)
    d = jnp.sum(e_new, axis=1, keepdims=True) + jnp.sum(e_old, axis=1, keepdims=True)
    o_win = (_nn(e_new.astype(BF16), wn) + _nt(e_old.astype(BF16), st)) / d

    heads = []
    for h in range(nh):
        r = slice(h * nq, (h + 1) * nq)
        heads.append(g[:, 3 * h:3 * h + 1] * o_cmp[r] + g[:, 3 * h + 1:3 * h + 2] * o_sel[r]
                     + g[:, 3 * h + 2:3 * h + 3] * o_win[r])
    for pair in range(nh // 2):
        both = jnp.where(lane < HEAD_DIM, pltpu.roll(heads[2 * pair], HEAD_DIM, 1), heads[2 * pair + 1])
        o_ref[out_rows, pair * LANES:(pair + 1) * LANES] = both.astype(BF16)


def _dec_c(qz16, gates, ckvn32, winn32, cw, state_t, cache_ct, page_table, l, nq, q0):
    n, n_pages = page_table.shape
    assert state_t.shape[3] == WINDOW and q0 >= WINDOW and n_pages * PAGE_SIZE // CMP_STRIDE == _CMP_ROWS
    ovl, nsb = _overlap_matrix(q0 + nq)
    keys = np.arange(n_pages * PAGE_SIZE)
    expand = jnp.asarray(np.arange(LANES)[:, None] == keys[None, :] // SEL_BLOCK, BF16)
    group = NSA_DEC_GROUP
    row = lambda w: pl.BlockSpec((group * nq, w), lambda b, pt: (b, 0))
    zero = lambda nd: (lambda b, pt: (0,) * nd)
    grid_spec = pltpu.PrefetchScalarGridSpec(
        num_scalar_prefetch=1, grid=(n // group,),
        in_specs=[row(2 * C_W), row(LANES), row(4 * HEAD_DIM), row(2 * HEAD_DIM)] + _const_specs(zero)
                 + [pl.BlockSpec((_CMP_ROWS, LANES), lambda b, pt: (0, 0)),
                    pl.BlockSpec((LANES, n_pages * PAGE_SIZE), lambda b, pt: (0, 0)),
                    pl.BlockSpec((None, group, 2 * HEAD_DIM, WINDOW), lambda b, pt: (l, b, 0, 0))]
                 + _page_specs(l, n_pages, 4 * HEAD_DIM, group=group),
        out_specs=row(C_W),
        scratch_shapes=[pltpu.VMEM((group * n_pages * PAGE_SIZE, LANES), F32)])
    return pl.pallas_call(
        functools.partial(_dec_c_body, nsb=nsb, q0=q0, group=group),
        out_shape=jax.ShapeDtypeStruct((n * nq, C_W), BF16),
        grid_spec=grid_spec,
        compiler_params=_params(1),
        name="nsa_attn_decode",
    )(page_table, qz16, gates, ckvn32, winn32, *cw, ovl, expand, state_t, *([cache_ct] * (n_pages * group)))


WIN_OUT_GROUP = 8


def _window_out_body(st_ref, new_ref, o_ref):
    group = st_ref.shape[0]
    nq = new_ref.shape[0] // group
    lane = _iota((LANES, LANES), 1)
    for g in range(group):
        new = new_ref[g * nq:(g + 1) * nq, :]
        new_t = jnp.concatenate([new, jnp.zeros((LANES - nq, LANES), F32)], axis=0).T
        slabs = [st_ref[g, :, k * LANES:(k + 1) * LANES] for k in range(WINDOW // LANES)] + [new_t]
        rolled = [pltpu.roll(x, LANES - nq, 1) for x in slabs]
        for k in range(WINDOW // LANES):
            o_ref[g, :, k * LANES:(k + 1) * LANES] = jnp.where(lane < LANES - nq, rolled[k], rolled[k + 1])


def _window_out(state_t, new_rows, nq):
    depth, n = state_t.shape[:2]
    group = WIN_OUT_GROUP
    blk = pl.BlockSpec((None, group, 2 * HEAD_DIM, WINDOW), lambda l, b: (l, b, 0, 0))
    return pl.pallas_call(
        _window_out_body,
        out_shape=jax.ShapeDtypeStruct(state_t.shape, F32),
        grid=(depth, n // group),
        in_specs=[blk, pl.BlockSpec((None, group * nq, 2 * HEAD_DIM), lambda l, b: (l, b, 0))],
        out_specs=blk,
        compiler_params=_params(2),
        name="window_state_update",
    )(state_t, new_rows)


def _rope_tables(n_pos):
    inv = ROPE_THETA ** (-jnp.arange(0, HEAD_DIM, 2, dtype=F32) / HEAD_DIM)
    ang = jnp.arange(n_pos, dtype=F32)[:, None] * inv[None, :]
    cos, sin = jnp.cos(ang), jnp.sin(ang)
    cos128 = jnp.concatenate([cos, cos, cos, cos], axis=-1)
    sin128 = jnp.concatenate([-sin, sin, -sin, sin], axis=-1)
    return cos128, sin128, cos.T, sin.T


def kernel(x_prompt, x_sample, cache_a_k, cache_a_v, cache_b_kv, cache_c_kv, state_c_win, page_table,
           w_in, w_out, g_mix, g_ffn, w_ffn_gate, w_ffn_up, w_ffn_down, diff_lambda, g_diff_head,
           w_cmp1, w_cmp2, cmp_pos, g_final):
    n_p, s_p, _ = x_prompt.shape
    n_s, s_s, _ = x_sample.shape
    n_pages = page_table.shape[1]
    past_len = n_pages * cache_a_k.shape[2]
    n_phys = cache_a_k.shape[1]
    assert cache_a_k.shape[2] == PAGE_SIZE and s_p % TM_PROJ == 0 and TM_PROJ % s_s == 0

    cos, sin, cos_t, sin_t = _rope_tables(past_len + s_s)
    tabs_p = (cos[:s_p], sin[:s_p], cos_t[:, :s_p], sin_t[:, :s_p])
    reps = TM_PROJ // s_s
    cos_s = jnp.tile(cos[past_len:past_len + s_s], (reps, 1))
    sin_s = jnp.tile(sin[past_len:past_len + s_s], (reps, 1))

    ckt = cache_a_k.transpose(0, 1, 3, 4, 5, 2).reshape(DEPTH, n_phys, A_W, PAGE_SIZE)
    cv4 = cache_a_v.reshape(DEPTH, n_phys, PAGE_SIZE * HEADS_A, 2 * HEAD_DIM)
    cbt = cache_b_kv.transpose(0, 1, 3, 4, 5, 2).reshape(DEPTH, n_phys, 2 * B_W, PAGE_SIZE)
    cct = cache_c_kv.transpose(0, 1, 3, 4, 2).reshape(DEPTH, n_phys, 4 * HEAD_DIM, PAGE_SIZE)
    stt = state_c_win.transpose(0, 1, 3, 4, 2).reshape(DEPTH, n_s, 2 * HEAD_DIM, WINDOW)

    hp = x_prompt.reshape(n_p * s_p, D_MODEL)
    hs = x_sample.reshape(n_s * s_s, D_MODEL)
    ent_s = []
    stacked = tuple(jnp.zeros(shape, F32) for shape in (
        (DEPTH, n_p, A_W, s_p), (DEPTH, n_p * s_p * HEADS_A, LANES), (DEPTH, n_p, 2 * B_W, s_p),
        (DEPTH, n_p, 4 * HEAD_DIM, s_p), (DEPTH, n_p, 2 * HEAD_DIM, s_p)))
    wg = w_ffn_gate.astype(BF16)
    wu = w_ffn_up.astype(BF16)
    wd = w_ffn_down.astype(BF16)
    wo = w_out.astype(BF16)
    for l in range(DEPTH):
        w_main = w_in[l][:, :MAIN_W].astype(BF16)
        w_t = jnp.concatenate([w_in[l][:, _O_KA:_O_KA + A_W], w_in[l][:, _O_KB:_O_KB + 2 * B_W],
                               w_in[l][:, _O_KVC:_O_KVC + KVC_W]], axis=1).T.astype(BF16)
        w_gate = jnp.pad(w_in[l][:, MAIN_W:], ((0, 0), (0, LANES - GATE_W))).astype(BF16)
        cw = _compress_weights(w_cmp1[l], w_cmp2[l], cmp_pos[l])
        last = l == DEPTH - 1

        hp = _ffn(hp, g_ffn[l, 0], wg, wu, wd, l, 0)
        outs = _proj_prompt(hp, g_mix[l], w_main, w_t, w_gate, tabs_p, n_p, s_p, l, stacked)
        stacked = tuple(outs[:5])
        qa16, kat16, va16, qb16, qb32, bkvt16, qcz16, selt16, wint16, gates = outs[5:]
        oa = _attn_a_prompt(qa16, kat16, va16, diff_lambda[l], g_diff_head[l], n_p, s_p, l)
        ob = _attn_b_prompt(qb16, qb32, stacked[2], bkvt16, n_p, s_p, l)
        kvcmp = _compress_prompt(stacked[3], cw, n_p, s_p, l)
        oc = _attn_c_prompt(qcz16, gates, selt16, wint16, kvcmp, n_p, s_p)
        hp = _ffn(hp, g_ffn[l, 1], wg, wu, wd, l, 1, (oa, ob, oc, wo), g_final if last else None)

        hs = _ffn(hs, g_ffn[l, 0], wg, wu, wd, l, 0)
        (ka, va, bkv, ckv, win, qa16, qb16, qb32, qcz16, gates) = _proj_sample(
            hs, g_mix[l], w_main, w_gate, cos_s, sin_s)
        oa = _dec_a(qa16, ka, va, diff_lambda[l], g_diff_head[l], ckt, cv4, page_table, l, s_s)
        ob = _dec_b(qb16, qb32, bkv, cbt, page_table, l, s_s)
        oc = _dec_c(qcz16, gates, ckv, win, cw, stt, cct, page_table, l, s_s, past_len)
        hs = _ffn(hs, g_ffn[l, 1], wg, wu, wd, l, 1, (oa, ob, oc, wo), g_final if last else None)
        ent_s.append((ka.reshape(n_s, s_s, HEADS_A, 2, HEAD_DIM), va.reshape(n_s, s_s, HEADS_A, 2 * HEAD_DIM),
                      bkv.reshape(n_s, s_s, 2, HEADS_B, HEAD_DIM), ckv.reshape(n_s, s_s, 4, HEAD_DIM), win))

    st_ = lambda i: jnp.stack([e[i] for e in ent_s], axis=0)
    kat, va, bkvt, ckvt, wint = stacked
    win_keep = min(WINDOW, s_p)
    assert state_c_win.shape[2] == WINDOW
    win_s = _window_out(stt, st_(4), s_s)
    return (hp.reshape(n_p, s_p, D_MODEL), hs.reshape(n_s, s_s, D_MODEL),
            kat.reshape(DEPTH, n_p, HEADS_A, 2, HEAD_DIM, s_p).transpose(0, 1, 5, 2, 3, 4), st_(0),
            va.reshape(DEPTH, n_p, s_p, HEADS_A, 2 * HEAD_DIM), st_(1),
            bkvt.reshape(DEPTH, n_p, 2, HEADS_B, HEAD_DIM, s_p).transpose(0, 1, 5, 2, 3, 4), st_(2),
            ckvt.reshape(DEPTH, n_p, 4, HEAD_DIM, s_p).transpose(0, 1, 4, 2, 3), st_(3),
            wint[:, :, :, s_p - win_keep:].reshape(DEPTH, n_p, 2, HEAD_DIM, win_keep).transpose(0, 1, 4, 2, 3),
            win_s.reshape(DEPTH, n_s, 2, HEAD_DIM, WINDOW).transpose(0, 1, 4, 2, 3))
```
